```python
import math
import jax, jax.numpy as jnp
from jax import lax
import numpy as np

D_MODEL = 2048
BATCH = 4
SEQ = 4096
DEPTH = 2

D_FF = 5632
FFN_RES = 0.5
ATTN_HEADS = 8
ATTN_KV_HEADS = 2
ATTN_HEAD_DIM = 128
ATTN_REP = ATTN_HEADS // ATTN_KV_HEADS
IDX_HEADS = 16
IDX_HEAD_DIM = 64
TOPK_MAX = 256
Q_BLOCK = 128
GLA_HEADS = 4
GLA_DK = 128
GLA_DV = 256
GLA_GATE_RANK = 16
GLA_GATE_NORM = 16.0
GLA_CHUNK = 64
ROPE_THETA = 500000.0
ATTN_ROT = ATTN_HEAD_DIM // 4
IDX_ROT = IDX_HEAD_DIM // 4
EPS = 1e-6
NEG_INF = -1e30

ATTN_Q = ATTN_HEADS * ATTN_HEAD_DIM
ATTN_KV = ATTN_KV_HEADS * ATTN_HEAD_DIM
IDX_Q = IDX_HEADS * IDX_HEAD_DIM
GLA_K = GLA_HEADS * GLA_DK
GLA_V = GLA_HEADS * GLA_DV
IN_SPLITS = (ATTN_Q, ATTN_KV, ATTN_KV, IDX_Q, IDX_HEAD_DIM, IDX_HEADS,
             GLA_K, GLA_K, GLA_V, GLA_GATE_RANK, GLA_V, D_MODEL, D_MODEL)
IN_COLS = int(sum(IN_SPLITS))
IN_OFFSETS = tuple(int(v) for v in np.cumsum(IN_SPLITS)[:-1])

kernel_name = "hybrid_dsa_gla_macaron"


def rmsnorm(x, g):
    xf = x.astype(jnp.float32)
    xf = xf * lax.rsqrt(jnp.mean(xf * xf, axis=-1, keepdims=True) + EPS)
    return (xf * g.astype(jnp.float32)).astype(x.dtype)


def swiglu(h, w_gate, w_up, w_down):
    return (jax.nn.silu(h @ w_gate) * (h @ w_up)) @ w_down


def rope_tables(length, rot):
    inv = ROPE_THETA ** (-jnp.arange(0, rot, 2, dtype=jnp.float32) / rot)
    ang = jnp.arange(length, dtype=jnp.float32)[:, None] * inv[None, :]
    return jnp.cos(ang), jnp.sin(ang)


def rope_partial(x, cos, sin):
    half = cos.shape[-1]
    x1, x2, xp = x[..., :half], x[..., half:2 * half], x[..., 2 * half:]
    c, s = cos[None, :, None, :], sin[None, :, None, :]
    return jnp.concatenate([x1 * c - x2 * s, x2 * c + x1 * s, xp], axis=-1).astype(x.dtype)


def dsa_attention(q, k, v, qi, ki, wi):
    B, L = q.shape[0], q.shape[1]
    nb = L // Q_BLOCK
    topk = min(TOPK_MAX, L // 4)
    keypos = jnp.arange(L, dtype=jnp.int32)
    ki32 = ki.astype(jnp.float32)
    gather = jax.vmap(lambda kv_b, idx_b: kv_b[idx_b])

    def to_blocks(a):
        return jnp.moveaxis(a.reshape((B, nb, Q_BLOCK) + a.shape[2:]), 1, 0)

    def block(args):
        start, qb, qib, wb = args
        t = start + jnp.arange(Q_BLOCK, dtype=jnp.int32)
        rel = jax.nn.relu(jnp.einsum('bqhd,bsd->bqhs', qib.astype(jnp.float32), ki32)
                          * (IDX_HEAD_DIM ** -0.5))
        score = jnp.einsum('bqhs,bqh->bqs', rel, wb.astype(jnp.float32))
        causal = keypos[None, :] <= t[:, None]
        score = jnp.where(causal[None], score, NEG_INF)
        _, sel = lax.top_k(score, topk)
        k_sel = gather(k, sel)
        v_sel = gather(v, sel)
        qg = qb.reshape(B, Q_BLOCK, ATTN_KV_HEADS, ATTN_REP, ATTN_HEAD_DIM)
        s = jnp.einsum('bqgrd,bqkgd->bqgrk', qg, k_sel).astype(jnp.float32) * (ATTN_HEAD_DIM ** -0.5)
        valid = sel <= t[None, :, None]
        s = jnp.where(valid[:, :, None, None, :], s, NEG_INF)
        p = jax.nn.softmax(s, axis=-1).astype(v.dtype)
        o = jnp.einsum('bqgrk,bqkgd->bqgrd', p, v_sel)
        return o.reshape(B, Q_BLOCK, ATTN_Q)

    starts = jnp.arange(nb, dtype=jnp.int32) * Q_BLOCK
    out = lax.map(block, (starts, to_blocks(q), to_blocks(qi), to_blocks(wi)))
    return jnp.moveaxis(out, 0, 1).reshape(B, L, ATTN_Q)


def gla_chunked(q, k, v, log_a):
    B, L, H, dk = q.shape
    dv = v.shape[-1]
    C = GLA_CHUNK
    n = L // C

    def chunks(a):
        return a.reshape(B, n, C, H, a.shape[-1]).transpose(0, 3, 1, 2, 4).astype(jnp.float32)

    q = chunks(q) * (dk ** -0.5)
    k, v, g = chunks(k), chunks(v), chunks(log_a)
    b = jnp.cumsum(g, axis=3)
    b_last = b[:, :, :, -1:, :]
    qe = q * jnp.exp(b - b_last)
    kd = k * jnp.exp(b_last - b)
    tri = jnp.tril(jnp.ones((C, C), dtype=bool))
    A = jnp.where(tri, jnp.einsum('bhnid,bhnjd->bhnij', qe, kd), 0.0)
    o_intra = jnp.einsum('bhnij,bhnje->bhnie', A, v)
    kv = jnp.einsum('bhnjd,bhnje->bhnde', kd, v)
    decay = jnp.exp(b_last[:, :, :, 0, :])

    def step(state, inp):
        kv_n, dec_n = inp
        return dec_n[..., None] * state + kv_n, state

    s0 = jnp.zeros((B, H, dk, dv), jnp.float32)
    _, s_prev = lax.scan(step, s0, (jnp.moveaxis(kv, 2, 0), jnp.moveaxis(decay, 2, 0)))
    s_prev = jnp.moveaxis(s_prev, 0, 2)
    o_inter = jnp.einsum('bhnid,bhnde->bhnie', q * jnp.exp(b), s_prev)
    o = o_intra + o_inter
    return o.transpose(0, 2, 3, 1, 4).reshape(B, L, H, dv)


def mixer(h, w_in, gla_w_a2, gla_b_a, gla_norm, w_up_attn, w_up_gla, w_out, cos_a, sin_a, cos_i, sin_i):
    B, L, _ = h.shape
    proj = h @ w_in
    (aq, ak, av, iq, ik, iw, gq, gk, gv, ga, gr, m_a, m_g) = jnp.split(proj, IN_OFFSETS, axis=-1)
    aq = rope_partial(aq.reshape(B, L, ATTN_HEADS, ATTN_HEAD_DIM), cos_a, sin_a)
    ak = rope_partial(ak.reshape(B, L, ATTN_KV_HEADS, ATTN_HEAD_DIM), cos_a, sin_a)
    av = av.reshape(B, L, ATTN_KV_HEADS, ATTN_HEAD_DIM)
    iq = rope_partial(iq.reshape(B, L, IDX_HEADS, IDX_HEAD_DIM), cos_i, sin_i)
    ik = rope_partial(ik.reshape(B, L, 1, IDX_HEAD_DIM), cos_i, sin_i)[:, :, 0]
    iw = iw * (IDX_HEADS ** -0.5)
    o_attn = dsa_attention(aq, ak, av, iq, ik, iw)
    log_a = jax.nn.log_sigmoid((ga @ gla_w_a2 + gla_b_a).astype(jnp.float32)) / GLA_GATE_NORM
    o_g = gla_chunked(gq.reshape(B, L, GLA_HEADS, GLA_DK), gk.reshape(B, L, GLA_HEADS, GLA_DK),
                      gv.reshape(B, L, GLA_HEADS, GLA_DV), log_a.reshape(B, L, GLA_HEADS, GLA_DK))
    o_g = rmsnorm(o_g, gla_norm) * jax.nn.silu(gr.reshape(B, L, GLA_HEADS, GLA_DV).astype(jnp.float32))
    o_g = o_g.reshape(B, L, GLA_V).astype(h.dtype)
    y = jax.nn.sigmoid(m_a) * (o_attn @ w_up_attn) + jax.nn.sigmoid(m_g) * (o_g @ w_up_gla)
    return y @ w_out


def setup_inputs(seed: int = 0) -> dict:
    key = jax.random.key(seed)
    ks = jax.random.split(key, 20)

    def dense(k, shape, fan_in):
        return jax.random.normal(k, shape, jnp.float32) * (fan_in ** -0.5)

    def gain(k, shape):
        return 1.0 + 0.01 * jax.random.normal(k, shape, jnp.float32)

    D, F = D_MODEL, D_FF
    return {
        "x": jax.random.normal(ks[0], (BATCH, SEQ, D), jnp.float32),
        "ffn1_norm": gain(ks[1], (DEPTH, D)),
        "ffn1_w_gate": dense(ks[2], (DEPTH, D, F), D),
        "ffn1_w_up": dense(ks[3], (DEPTH, D, F), D),
        "ffn1_w_down": dense(ks[4], (DEPTH, F, D), F),
        "mix_norm": gain(ks[5], (DEPTH, D)),
        "w_in": dense(ks[6], (DEPTH, D, IN_COLS), D),
        "gla_w_a2": dense(ks[7], (DEPTH, GLA_GATE_RANK, GLA_K), GLA_GATE_RANK),
        "gla_b_a": 0.01 * jax.random.normal(ks[8], (DEPTH, GLA_K), jnp.float32),
        "gla_norm": gain(ks[9], (DEPTH, GLA_DV)),
        "w_up_attn": dense(ks[10], (DEPTH, ATTN_Q, D), ATTN_Q),
        "w_up_gla": dense(ks[11], (DEPTH, GLA_V, D), GLA_V),
        "w_out": dense(ks[12], (DEPTH, D, D), D),
        "ffn2_norm": gain(ks[13], (DEPTH, D)),
        "ffn2_w_gate": dense(ks[14], (DEPTH, D, F), D),
        "ffn2_w_up": dense(ks[15], (DEPTH, D, F), D),
        "ffn2_w_down": dense(ks[16], (DEPTH, F, D), F),
        "final_norm": gain(ks[17], (D,)),
    }


def reference(x, ffn1_norm, ffn1_w_gate, ffn1_w_up, ffn1_w_down, mix_norm, w_in, gla_w_a2, gla_b_a,
              gla_norm, w_up_attn, w_up_gla, w_out, ffn2_norm, ffn2_w_gate, ffn2_w_up, ffn2_w_down,
              final_norm):
    L = x.shape[1]
    cos_a, sin_a = rope_tables(L, ATTN_ROT)
    cos_i, sin_i = rope_tables(L, IDX_ROT)
    for l in range(DEPTH):
        x = x + FFN_RES * swiglu(rmsnorm(x, ffn1_norm[l]), ffn1_w_gate[l], ffn1_w_up[l], ffn1_w_down[l])
        x = x + mixer(rmsnorm(x, mix_norm[l]), w_in[l], gla_w_a2[l], gla_b_a[l], gla_norm[l],
                      w_up_attn[l], w_up_gla[l], w_out[l], cos_a, sin_a, cos_i, sin_i)
        x = x + FFN_RES * swiglu(rmsnorm(x, ffn2_norm[l]), ffn2_w_gate[l], ffn2_w_up[l], ffn2_w_down[l])
    return rmsnorm(x, final_norm)
```

```python
import functools

import jax
import jax.numpy as jnp
import numpy as np
from jax import lax
from jax.experimental import pallas as pl
from jax.experimental.pallas import tpu as pltpu

F32 = jnp.float32
BF16 = jnp.bfloat16

D_MODEL = 2048
D_FF = 5632
FFN_RES = 0.5
ATTN_HEADS = 8
ATTN_KV_HEADS = 2
ATTN_HEAD_DIM = 128
ATTN_REP = ATTN_HEADS // ATTN_KV_HEADS
IDX_HEADS = 16
IDX_HEAD_DIM = 64
TOPK_MAX = 256
GLA_HEADS = 4
GLA_DK = 128
GLA_DV = 256
GLA_GATE_RANK = 16
GLA_GATE_NORM = 16.0
GLA_CHUNK = 64
ROPE_THETA = 500000.0
ATTN_ROT = ATTN_HEAD_DIM // 4
IDX_ROT = IDX_HEAD_DIM // 4
EPS = 1e-6
NEG_INF = -1e30
INT_MIN = -(2 ** 31)

ATTN_Q = ATTN_HEADS * ATTN_HEAD_DIM
ATTN_KV = ATTN_KV_HEADS * ATTN_HEAD_DIM
IDX_Q = IDX_HEADS * IDX_HEAD_DIM
GLA_K = GLA_HEADS * GLA_DK
GLA_V = GLA_HEADS * GLA_DV
IN_SPLITS = (ATTN_Q, ATTN_KV, ATTN_KV, IDX_Q, IDX_HEAD_DIM, IDX_HEADS,
             GLA_K, GLA_K, GLA_V, GLA_GATE_RANK, GLA_V, D_MODEL, D_MODEL)
IN_OFFSETS = tuple(int(v) for v in np.cumsum(IN_SPLITS)[:-1])

LANES = 128
COL_AQ = 0
COL_IQ = COL_AQ + ATTN_Q
COL_GV = COL_IQ + IDX_Q
COL_GR = COL_GV + GLA_V
COL_MA = COL_GR + GLA_V
COL_MG = COL_MA + D_MODEL
COL_GQ = COL_MG + D_MODEL
COL_GK = COL_GQ + GLA_K
COL_AK = COL_GK + GLA_K
COL_AV = COL_AK + ATTN_KV
COL_SM = COL_AV + ATTN_KV
SM_IK = 0
SM_IW = SM_IK + IDX_HEAD_DIM
SM_GA = SM_IW + IDX_HEADS
PROJ_TN = 1024
PROJ_COLS = -(-(COL_SM + LANES) // PROJ_TN) * PROJ_TN

VMEM_LIMIT = 56 * 1024 * 1024


def _cparams(sem):
    return pltpu.CompilerParams(dimension_semantics=sem, vmem_limit_bytes=VMEM_LIMIT)


def _dot(a, b):
    return jnp.dot(a, b, preferred_element_type=F32)


def _dot_nt(a, b):
    return lax.dot_general(a, b, (((1,), (1,)), ((), ())), preferred_element_type=F32)


def _rms(x, g):
    return x * lax.rsqrt(jnp.mean(x * x, axis=-1, keepdims=True) + EPS) * g


def _ffn_body(x_ref, g_ref, wg_ref, wu_ref, wd_ref, fg_ref, o_ref, h_ref, acc_ref, *, final_norm):
    f = pl.program_id(1)

    @pl.when(f == 0)
    def _():
        h_ref[...] = _rms(x_ref[...], g_ref[...]).astype(BF16)
        acc_ref[...] = jnp.zeros_like(acc_ref)

    h = h_ref[...]
    gate = _dot(h, wg_ref[...])
    up = _dot(h, wu_ref[...])
    act = (gate * jax.nn.sigmoid(gate) * up).astype(BF16)
    acc_ref[...] += _dot(act, wd_ref[...])

    @pl.when(f == pl.num_programs(1) - 1)
    def _():
        y = x_ref[...] + FFN_RES * acc_ref[...]
        if final_norm:
            y = _rms(y, fg_ref[...])
        o_ref[...] = y


def _ffn(x, g, wg, wu, wd, fg, *, final_norm, tm=512, tf=512):
    n, d = x.shape
    f = wg.shape[1]
    return pl.pallas_call(
        functools.partial(_ffn_body, final_norm=final_norm),
        grid=(n // tm, f // tf),
        in_specs=[
            pl.BlockSpec((tm, d), lambda i, j: (i, 0)),
            pl.BlockSpec((1, d), lambda i, j: (0, 0)),
            pl.BlockSpec((d, tf), lambda i, j: (0, j)),
            pl.BlockSpec((d, tf), lambda i, j: (0, j)),
            pl.BlockSpec((tf, d), lambda i, j: (j, 0)),
            pl.BlockSpec((1, d), lambda i, j: (0, 0)),
        ],
        out_specs=pl.BlockSpec((tm, d), lambda i, j: (i, 0)),
        out_shape=jax.ShapeDtypeStruct((n, d), F32),
        scratch_shapes=[pltpu.VMEM((tm, d), BF16), pltpu.VMEM((tm, d), F32)],
        compiler_params=_cparams(("parallel", "arbitrary")),
        name="ffn",
    )(x, g, wg, wu, wd, fg)


def _proj_body(x_ref, g_ref, w_ref, o_ref, h_ref):
    @pl.when(pl.program_id(1) == 0)
    def _():
        h_ref[...] = _rms(x_ref[...], g_ref[...]).astype(BF16)

    o_ref[...] = _dot(h_ref[...], w_ref[...]).astype(o_ref.dtype)


def _proj(x, g, w, *, tm=512, tn=PROJ_TN):
    n, d = x.shape
    c = w.shape[1]
    return pl.pallas_call(
        _proj_body,
        grid=(n // tm, c // tn),
        in_specs=[
            pl.BlockSpec((tm, d), lambda i, j: (i, 0)),
            pl.BlockSpec((1, d), lambda i, j: (0, 0)),
            pl.BlockSpec((d, tn), lambda i, j: (0, j)),
        ],
        out_specs=pl.BlockSpec((tm, tn), lambda i, j: (i, j)),
        out_shape=jax.ShapeDtypeStruct((n, c), BF16),
        scratch_shapes=[pltpu.VMEM((tm, d), BF16)],
        compiler_params=_cparams(("parallel", "arbitrary")),
        name="proj",
    )(x, g, w)


def _rope_tables(length, head_dim, rot):
    half = rot // 2
    inv = ROPE_THETA ** (-jnp.arange(0, rot, 2, dtype=F32) / rot)
    ang = jnp.arange(length, dtype=F32)[:, None] * inv[None, :]
    cos, sin = jnp.cos(ang), jnp.sin(ang)
    zeros = jnp.zeros((length, head_dim - rot), F32)
    zh = jnp.zeros((length, half), F32)
    c = jnp.concatenate([cos, cos, zeros + 1.0], axis=1)
    s1 = jnp.concatenate([-sin, zh, zeros], axis=1)
    s2 = jnp.concatenate([zh, sin, zeros], axis=1)
    reps = LANES // head_dim
    return tuple(jnp.tile(t, (1, reps)) for t in (c, s1, s2))


def _rope_lanes(x, c, s1, s2, half):
    return x * c + pltpu.roll(x, LANES - half, 1) * s1 + pltpu.roll(x, half, 1) * s2


def _rope_body(aq_ref, iq_ref, ak_ref, sm_ref, ca_ref, sa1_ref, sa2_ref, ci_ref, si1_ref, si2_ref,
               aqo_ref, iqo_ref, ako_ref, smo_ref):
    ca, sa1, sa2 = ca_ref[...], sa1_ref[...], sa2_ref[...]
    ci, si1, si2 = ci_ref[...], si1_ref[...], si2_ref[...]
    ha, hi = ATTN_ROT // 2, IDX_ROT // 2
    for j in range(ATTN_Q // LANES):
        sl = slice(j * LANES, (j + 1) * LANES)
        r = _rope_lanes(aq_ref[:, sl].astype(F32), ca, sa1, sa2, ha)
        aqo_ref[:, sl] = (r * (ATTN_HEAD_DIM ** -0.5)).astype(BF16)
    for j in range(IDX_Q // LANES):
        sl = slice(j * LANES, (j + 1) * LANES)
        r = _rope_lanes(iq_ref[:, sl].astype(F32), ci, si1, si2, hi)
        iqo_ref[:, sl] = (r * (IDX_HEAD_DIM ** -0.5)).astype(BF16)
    for j in range(ATTN_KV // LANES):
        sl = slice(j * LANES, (j + 1) * LANES)
        ako_ref[:, sl] = _rope_lanes(ak_ref[:, sl].astype(F32), ca, sa1, sa2, ha).astype(BF16)
    sm = sm_ref[...].astype(F32)
    lane = lax.broadcasted_iota(jnp.int32, sm.shape, 1)
    roped = _rope_lanes(sm, ci, si1, si2, hi)
    smo_ref[...] = jnp.where(lane < IDX_HEAD_DIM, roped, sm).astype(BF16)


def _rope(proj, seq, tabs_a, tabs_i, *, tr=512):
    n = proj.shape[0]
    nl = seq // tr
    tab = pl.BlockSpec((tr, LANES), lambda i: (i % nl, 0))
    return pl.pallas_call(
        _rope_body,
        grid=(n // tr,),
        in_specs=[
            pl.BlockSpec((tr, ATTN_Q), lambda i: (i, COL_AQ // ATTN_Q)),
            pl.BlockSpec((tr, IDX_Q), lambda i: (i, COL_IQ // IDX_Q)),
            pl.BlockSpec((tr, ATTN_KV), lambda i: (i, COL_AK // ATTN_KV)),
            pl.BlockSpec((tr, LANES), lambda i: (i, COL_SM // LANES)),
            tab, tab, tab, tab, tab, tab,
        ],
        out_specs=[
            pl.BlockSpec((tr, ATTN_Q), lambda i: (i, 0)),
            pl.BlockSpec((tr, IDX_Q), lambda i: (i, 0)),
            pl.BlockSpec((tr, ATTN_KV), lambda i: (i, 0)),
            pl.BlockSpec((tr, LANES), lambda i: (i, 0)),
        ],
        out_shape=[
            jax.ShapeDtypeStruct((n, ATTN_Q), BF16),
            jax.ShapeDtypeStruct((n, IDX_Q), BF16),
            jax.ShapeDtypeStruct((n, ATTN_KV), BF16),
            jax.ShapeDtypeStruct((n, LANES), BF16),
        ],
        compiler_params=_cparams(("parallel",)),
        name="rope",
    )(proj, proj, proj, proj, *tabs_a, *tabs_i)


DSA_TQ = 128
DSA_TK = 256


def _sortable(x):
    i = pltpu.bitcast(x + 0.0, jnp.int32)
    return jnp.where(i < 0, i ^ jnp.int32(0x7FFFFFFF), i)


def _dsa_body(iq_ref, smq_ref, aq_ref, smk_ref, ak_ref, av_ref, o_ref,
              key_ref, m_ref, l_ref, acc_ref, *, topk):
    tq, tk = DSA_TQ, DSA_TK
    qi = pl.program_id(1)
    t0 = qi * tq
    nkb = (t0 + tq + tk - 1) // tk
    iq = iq_ref[...]
    iw = smq_ref[:, SM_IW:SM_IW + IDX_HEADS].astype(F32) * (IDX_HEADS ** -0.5)
    row = lax.broadcasted_iota(jnp.int32, (tq, tk), 0) + t0
    col = lax.broadcasted_iota(jnp.int32, (tq, tk), 1)

    def idx_block(kb, carry):
        k0 = pl.multiple_of(kb * tk, tk)
        ik = smk_ref[pl.ds(k0, tk), :][:, SM_IK:SM_IK + IDX_HEAD_DIM]
        score = jnp.zeros((tq, tk), F32)
        for h in range(IDX_HEADS):
            qh = iq[:, h * IDX_HEAD_DIM:(h + 1) * IDX_HEAD_DIM]
            score = score + jnp.maximum(_dot_nt(qh, ik), 0.0) * iw[:, h:h + 1]
        key = jnp.where(col + k0 <= row, _sortable(score), jnp.int32(INT_MIN))
        key_ref[kb] = key
        return carry

    lax.fori_loop(0, nkb, idx_block, 0)

    def bisect(it, thr):
        cand = thr + lax.shift_left(jnp.int32(1), jnp.int32(31) - it)

        def cnt_block(kb, c):
            return c + jnp.where(key_ref[kb] >= cand, 1.0, 0.0)

        cnt = lax.fori_loop(0, nkb, cnt_block, jnp.zeros((tq, tk), F32))
        cnt = jnp.sum(cnt, axis=1, keepdims=True)
        return jnp.where(cnt >= float(topk), cand, thr)

    thr = lax.fori_loop(0, 32, bisect, jnp.full((tq, 1), INT_MIN, jnp.int32))

    m_ref[...] = jnp.full_like(m_ref, NEG_INF)
    l_ref[...] = jnp.zeros_like(l_ref)
    acc_ref[...] = jnp.zeros_like(acc_ref)
    qs = []
    for g in range(ATTN_KV_HEADS):
        qs.append(jnp.concatenate(
            [aq_ref[:, (g * ATTN_REP + r) * ATTN_HEAD_DIM:(g * ATTN_REP + r + 1) * ATTN_HEAD_DIM]
             for r in range(ATTN_REP)], axis=0))

    def att_block(kb, carry):
        k0 = pl.multiple_of(kb * tk, tk)
        key = key_ref[kb]
        sel = (key >= thr) & (key != jnp.int32(INT_MIN))
        bias = jnp.where(sel, 0.0, NEG_INF)
        for g in range(ATTN_KV_HEADS):
            kg = ak_ref[pl.ds(k0, tk), g * ATTN_HEAD_DIM:(g + 1) * ATTN_HEAD_DIM]
            vg = av_ref[pl.ds(k0, tk), g * ATTN_HEAD_DIM:(g + 1) * ATTN_HEAD_DIM]
            s = _dot_nt(qs[g], kg).reshape(ATTN_REP, tq, tk) + bias[None]
            s = s.reshape(ATTN_REP * tq, tk)
            m_old = m_ref[g]
            m_new = jnp.maximum(m_old, jnp.max(s, axis=1, keepdims=True))
            alpha = jnp.exp(m_old - m_new)
            p = jnp.exp(s - m_new)
            l_ref[g] = alpha * l_ref[g] + jnp.sum(p, axis=1, keepdims=True)
            acc_ref[g] = alpha * acc_ref[g] + _dot(p.astype(BF16), vg)
            m_ref[g] = m_new
        return carry

    lax.fori_loop(0, nkb, att_block, 0)
    for g in range(ATTN_KV_HEADS):
        o = acc_ref[g] / l_ref[g]
        for r in range(ATTN_REP):
            hh = g * ATTN_REP + r
            o_ref[:, hh * ATTN_HEAD_DIM:(hh + 1) * ATTN_HEAD_DIM] = o[r * tq:(r + 1) * tq].astype(BF16)


def _dsa(proj, aq_r, iq_r, ak_r, sm_r, batch, seq):
    n = proj.shape[0]
    tq, tk = DSA_TQ, DSA_TK
    nq = seq // tq
    topk = min(TOPK_MAX, seq // 4)
    qmap = lambda b, i: (b * nq + i, 0)
    return pl.pallas_call(
        functools.partial(_dsa_body, topk=topk),
        grid=(batch, nq),
        in_specs=[
            pl.BlockSpec((tq, IDX_Q), qmap),
            pl.BlockSpec((tq, LANES), qmap),
            pl.BlockSpec((tq, ATTN_Q), qmap),
            pl.BlockSpec((seq, LANES), lambda b, i: (b, 0)),
            pl.BlockSpec((seq, ATTN_KV), lambda b, i: (b, 0)),
            pl.BlockSpec((seq, ATTN_KV), lambda b, i: (b, COL_AV // ATTN_KV)),
        ],
        out_specs=pl.BlockSpec((tq, ATTN_Q), qmap),
        out_shape=jax.ShapeDtypeStruct((n, ATTN_Q), BF16),
        scratch_shapes=[
            pltpu.VMEM((-(-seq // tk), tq, tk), jnp.int32),
            pltpu.VMEM((ATTN_KV_HEADS, ATTN_REP * tq, 1), F32),
            pltpu.VMEM((ATTN_KV_HEADS, ATTN_REP * tq, 1), F32),
            pltpu.VMEM((ATTN_KV_HEADS, ATTN_REP * tq, ATTN_HEAD_DIM), F32),
        ],
        compiler_params=_cparams(("parallel", "arbitrary")),
        name="dsa",
    )(iq_r, sm_r, aq_r, sm_r, ak_r, proj)


GLA_ROWS = 256


def _split3(x):
    hi = x.astype(BF16)
    r1 = x - hi.astype(F32)
    mid = r1.astype(BF16)
    lo = (r1 - mid.astype(F32)).astype(BF16)
    return hi, mid, lo


def _gla_body(q_ref, k_ref, v_ref, r_ref, sm_ref, wa_ref, ba_ref, gn_ref, o_ref, st_ref):
    rows, c = GLA_ROWS, GLA_CHUNK

    @pl.when(pl.program_id(2) == 0)
    def _():
        st_ref[...] = jnp.zeros_like(st_ref)

    ga = sm_ref[:, SM_GA:SM_GA + GLA_GATE_RANK]
    z = _dot(ga, wa_ref[...]) + ba_ref[...]
    g = -(jnp.maximum(-z, 0.0) + jnp.log1p(jnp.exp(-jnp.abs(z)))) * (1.0 / GLA_GATE_NORM)
    ri = lax.broadcasted_iota(jnp.int32, (rows, rows), 0)
    ci = lax.broadcasted_iota(jnp.int32, (rows, rows), 1)
    tri = jnp.where((ri // c == ci // c) & (ci <= ri), 1.0, 0.0).astype(BF16)
    hi, mid, lo = _split3(g)
    b = _dot(tri, hi) + _dot(tri, mid) + _dot(tri, lo)
    low = lax.broadcasted_iota(jnp.int32, (c, c), 1) <= lax.broadcasted_iota(jnp.int32, (c, c), 0)
    gn = gn_ref[...]
    for ch in range(rows // c):
        sl = slice(ch * c, (ch + 1) * c)
        bc = b[sl]
        bl = bc[c - 1:c, :]
        q = q_ref[sl, :].astype(F32) * (GLA_DK ** -0.5)
        k = k_ref[sl, :].astype(F32)
        v = v_ref[sl, :]
        qe = (q * jnp.exp(bc - bl)).astype(BF16)
        kd = (k * jnp.exp(bl - bc)).astype(BF16)
        qb = (q * jnp.exp(bc)).astype(BF16)
        a = jnp.where(low, _dot_nt(qe, kd), 0.0).astype(BF16)
        st = st_ref[...]
        o = _dot(a, v) + _dot_nt(qb, st.astype(BF16))
        vt = v.astype(F32).T.astype(BF16)
        st_ref[...] = st * jnp.exp(bl) + _dot(vt, kd)
        o = _rms(o, gn)
        gr = r_ref[sl, :].astype(F32)
        o_ref[sl, :] = (o * (gr * jax.nn.sigmoid(gr))).astype(BF16)


def _gla(proj, wa2, ba, gn, batch, seq):
    n = proj.shape[0]
    rows = GLA_ROWS
    ng = seq // rows
    rmap = lambda off, w: (lambda b, h, i: (b * ng + i, off // w + h))
    return pl.pallas_call(
        _gla_body,
        grid=(batch, GLA_HEADS, ng),
        in_specs=[
            pl.BlockSpec((rows, GLA_DK), rmap(COL_GQ, GLA_DK)),
            pl.BlockSpec((rows, GLA_DK), rmap(COL_GK, GLA_DK)),
            pl.BlockSpec((rows, GLA_DV), rmap(COL_GV, GLA_DV)),
            pl.BlockSpec((rows, GLA_DV), rmap(COL_GR, GLA_DV)),
            pl.BlockSpec((rows, LANES), lambda b, h, i: (b * ng + i, COL_SM // LANES)),
            pl.BlockSpec((GLA_GATE_RANK, GLA_DK), lambda b, h, i: (0, h)),
            pl.BlockSpec((1, GLA_DK), lambda b, h, i: (0, h)),
            pl.BlockSpec((1, GLA_DV), lambda b, h, i: (0, 0)),
        ],
        out_specs=pl.BlockSpec((rows, GLA_DV), lambda b, h, i: (b * ng + i, h)),
        out_shape=jax.ShapeDtypeStruct((n, GLA_V), BF16),
        scratch_shapes=[pltpu.VMEM((GLA_DV, GLA_DK), F32)],
        compiler_params=_cparams(("parallel", "parallel", "arbitrary")),
        name="gla",
    )(proj, proj, proj, proj, proj, wa2, ba, gn)


def _merge_body(x_ref, oa_ref, og_ref, ma_ref, mg_ref, wua_ref, wug_ref, wo_ref, o_ref):
    ya = jax.nn.sigmoid(ma_ref[...].astype(F32)) * _dot(oa_ref[...], wua_ref[...])
    yg = jax.nn.sigmoid(mg_ref[...].astype(F32)) * _dot(og_ref[...], wug_ref[...])
    y = (ya + yg).astype(BF16)
    o_ref[...] = x_ref[...] + _dot(y, wo_ref[...])


def _merge(x, o_attn, o_gla, proj, wua, wug, wo, *, tm=256):
    n, d = x.shape
    const = lambda i: (0, 0)
    return pl.pallas_call(
        _merge_body,
        grid=(n // tm,),
        in_specs=[
            pl.BlockSpec((tm, d), lambda i: (i, 0)),
            pl.BlockSpec((tm, ATTN_Q), lambda i: (i, 0)),
            pl.BlockSpec((tm, GLA_V), lambda i: (i, 0)),
            pl.BlockSpec((tm, d), lambda i: (i, COL_MA // D_MODEL)),
            pl.BlockSpec((tm, d), lambda i: (i, COL_MG // D_MODEL)),
            pl.BlockSpec((ATTN_Q, d), const),
            pl.BlockSpec((GLA_V, d), const),
            pl.BlockSpec((d, d), const),
        ],
        out_specs=pl.BlockSpec((tm, d), lambda i: (i, 0)),
        out_shape=jax.ShapeDtypeStruct((n, d), F32),
        compiler_params=_cparams(("parallel",)),
        name="merge",
    )(x, o_attn, o_gla, proj, proj, wua, wug, wo)


def _reorder_w_in(w):
    (aq, ak, av, iq, ik, iw, gq, gk, gv, ga, gr, m_a, m_g) = jnp.split(w, IN_OFFSETS, axis=-1)
    d = w.shape[0]
    small = jnp.concatenate(
        [ik, iw, ga, jnp.zeros((d, LANES - IDX_HEAD_DIM - IDX_HEADS - GLA_GATE_RANK), w.dtype)], axis=-1)
    cols = jnp.concatenate([aq, iq, gv, gr, m_a, m_g, gq, gk, ak, av, small], axis=-1)
    cols = jnp.pad(cols, ((0, 0), (0, PROJ_COLS - cols.shape[1])))
    return cols.astype(BF16)


def kernel(x, ffn1_norm, ffn1_w_gate, ffn1_w_up, ffn1_w_down, mix_norm, w_in, gla_w_a2, gla_b_a, gla_norm, w_up_attn, w_up_gla, w_out, ffn2_norm, ffn2_w_gate, ffn2_w_up, ffn2_w_down, final_norm):
    batch, seq, d = x.shape
    depth = w_in.shape[0]
    tabs_a = _rope_tables(seq, ATTN_HEAD_DIM, ATTN_ROT)
    tabs_i = _rope_tables(seq, IDX_HEAD_DIM, IDX_ROT)
    fg = final_norm.reshape(1, d)
    h = x.reshape(batch * seq, d)
    for l in range(depth):
        h = _ffn(h, ffn1_norm[l].reshape(1, d), ffn1_w_gate[l].astype(BF16), ffn1_w_up[l].astype(BF16),
                 ffn1_w_down[l].astype(BF16), fg, final_norm=False)
        proj = _proj(h, mix_norm[l].reshape(1, d), _reorder_w_in(w_in[l]))
        aq_r, iq_r, ak_r, sm_r = _rope(proj, seq, tabs_a, tabs_i)
        o_attn = _dsa(proj, aq_r, iq_r, ak_r, sm_r, batch, seq)
        o_gla = _gla(proj, gla_w_a2[l].astype(BF16), gla_b_a[l].reshape(1, GLA_K),
                     gla_norm[l].reshape(1, GLA_DV), batch, seq)
        h = _merge(h, o_attn, o_gla, proj, w_up_attn[l].astype(BF16), w_up_gla[l].astype(BF16),
                   w_out[l].astype(BF16))
        h = _ffn(h, ffn2_norm[l].reshape(1, d), ffn2_w_gate[l].astype(BF16), ffn2_w_up[l].astype(BF16),
                 ffn2_w_down[l].astype(BF16), fg, final_norm=(l == depth - 1))
    return h.reshape(batch, seq, d)
```

```python
import functools

import jax
import jax.numpy as jnp
import numpy as np
from jax import lax
from jax.experimental import pallas as pl
from jax.experimental.pallas import tpu as pltpu

F32 = jnp.float32
BF16 = jnp.bfloat16

D_MODEL = 2048
D_FF = 5632
FFN_RES = 0.5
ATTN_HEADS = 8
ATTN_KV_HEADS = 2
ATTN_HEAD_DIM = 128
ATTN_REP = ATTN_HEADS // ATTN_KV_HEADS
IDX_HEADS = 16
IDX_HEAD_DIM = 64
TOPK_MAX = 256
GLA_HEADS = 4
GLA_DK = 128
GLA_DV = 256
GLA_GATE_RANK = 16
GLA_GATE_NORM = 16.0
GLA_CHUNK = 64
ROPE_THETA = 500000.0
ATTN_ROT = ATTN_HEAD_DIM // 4
IDX_ROT = IDX_HEAD_DIM // 4
EPS = 1e-6
NEG_INF = -1e30
INT_MIN = -(2 ** 31)

ATTN_Q = ATTN_HEADS * ATTN_HEAD_DIM
ATTN_KV = ATTN_KV_HEADS * ATTN_HEAD_DIM
IDX_Q = IDX_HEADS * IDX_HEAD_DIM
GLA_K = GLA_HEADS * GLA_DK
GLA_V = GLA_HEADS * GLA_DV
IN_SPLITS = (ATTN_Q, ATTN_KV, ATTN_KV, IDX_Q, IDX_HEAD_DIM, IDX_HEADS,
             GLA_K, GLA_K, GLA_V, GLA_GATE_RANK, GLA_V, D_MODEL, D_MODEL)
IN_OFFSETS = tuple(int(v) for v in np.cumsum(IN_SPLITS)[:-1])

LANES = 128
COL_AQ = 0
COL_IQ = COL_AQ + ATTN_Q
COL_GV = COL_IQ + IDX_Q
COL_GR = COL_GV + GLA_V
COL_MA = COL_GR + GLA_V
COL_MG = COL_MA + D_MODEL
COL_GQ = COL_MG + D_MODEL
COL_GK = COL_GQ + GLA_K
COL_AK = COL_GK + GLA_K
COL_AV = COL_AK + ATTN_KV
COL_SM = COL_AV + ATTN_KV
SM_IK = 0
SM_IW = SM_IK + IDX_HEAD_DIM
SM_GA = SM_IW + IDX_HEADS
PROJ_TN = 1024
PROJ_COLS = -(-(COL_SM + LANES) // PROJ_TN) * PROJ_TN

DSA_TQ = 128
DSA_TK = 256

VMEM_LIMIT = 56 * 1024 * 1024


def _cparams(sem):
    return pltpu.CompilerParams(dimension_semantics=sem, vmem_limit_bytes=VMEM_LIMIT)


def _dot(a, b):
    return jnp.dot(a, b, preferred_element_type=F32)


def _dot_nt(a, b):
    return lax.dot_general(a, b, (((1,), (1,)), ((), ())), preferred_element_type=F32)


def _rms(x, g):
    return x * lax.rsqrt(jnp.mean(x * x, axis=-1, keepdims=True) + EPS) * g


def _ffn_body(x_ref, g_ref, wg_ref, wu_ref, wd_ref, fg_ref, o_ref, h_ref, acc_ref, *, final_norm):
    f = pl.program_id(1)

    @pl.when(f == 0)
    def _():
        h_ref[...] = _rms(x_ref[...], g_ref[...]).astype(BF16)
        acc_ref[...] = jnp.zeros_like(acc_ref)

    h = h_ref[...]
    gate = _dot(h, wg_ref[...])
    up = _dot(h, wu_ref[...])
    act = (gate * jax.nn.sigmoid(gate) * up).astype(BF16)
    acc_ref[...] += _dot(act, wd_ref[...])

    @pl.when(f == pl.num_programs(1) - 1)
    def _():
        y = x_ref[...] + FFN_RES * acc_ref[...]
        if final_norm:
            y = _rms(y, fg_ref[...])
        o_ref[...] = y


def _ffn(x, g, wg, wu, wd, fg, *, final_norm, tm=512, tf=512):
    n, d = x.shape
    f = wg.shape[1]
    return pl.pallas_call(
        functools.partial(_ffn_body, final_norm=final_norm),
        grid=(n // tm, f // tf),
        in_specs=[
            pl.BlockSpec((tm, d), lambda i, j: (i, 0)),
            pl.BlockSpec((1, d), lambda i, j: (0, 0)),
            pl.BlockSpec((d, tf), lambda i, j: (0, j)),
            pl.BlockSpec((d, tf), lambda i, j: (0, j)),
            pl.BlockSpec((tf, d), lambda i, j: (j, 0)),
            pl.BlockSpec((1, d), lambda i, j: (0, 0)),
        ],
        out_specs=pl.BlockSpec((tm, d), lambda i, j: (i, 0)),
        out_shape=jax.ShapeDtypeStruct((n, d), F32),
        scratch_shapes=[pltpu.VMEM((tm, d), BF16), pltpu.VMEM((tm, d), F32)],
        compiler_params=_cparams(("parallel", "arbitrary")),
        name="ffn",
    )(x, g, wg, wu, wd, fg)


def _proj_body(x_ref, g_ref, w_ref, o_ref, h_ref):
    @pl.when(pl.program_id(1) == 0)
    def _():
        h_ref[...] = _rms(x_ref[...], g_ref[...]).astype(BF16)

    o_ref[...] = _dot(h_ref[...], w_ref[...]).astype(o_ref.dtype)


def _proj(x, g, w, *, tm=512, tn=PROJ_TN):
    n, d = x.shape
    c = w.shape[1]
    return pl.pallas_call(
        _proj_body,
        grid=(n // tm, c // tn),
        in_specs=[
            pl.BlockSpec((tm, d), lambda i, j: (i, 0)),
            pl.BlockSpec((1, d), lambda i, j: (0, 0)),
            pl.BlockSpec((d, tn), lambda i, j: (0, j)),
        ],
        out_specs=pl.BlockSpec((tm, tn), lambda i, j: (i, j)),
        out_shape=jax.ShapeDtypeStruct((n, c), BF16),
        scratch_shapes=[pltpu.VMEM((tm, d), BF16)],
        compiler_params=_cparams(("parallel", "arbitrary")),
        name="proj",
    )(x, g, w)


def _rope_tables(length, head_dim, rot):
    half = rot // 2
    inv = ROPE_THETA ** (-jnp.arange(0, rot, 2, dtype=F32) / rot)
    ang = jnp.arange(length, dtype=F32)[:, None] * inv[None, :]
    cos, sin = jnp.cos(ang), jnp.sin(ang)
    zeros = jnp.zeros((length, head_dim - rot), F32)
    zh = jnp.zeros((length, half), F32)
    c = jnp.concatenate([cos, cos, zeros + 1.0], axis=1)
    s1 = jnp.concatenate([-sin, zh, zeros], axis=1)
    s2 = jnp.concatenate([zh, sin, zeros], axis=1)
    reps = LANES // head_dim
    return tuple(jnp.tile(t, (1, reps)) for t in (c, s1, s2))


def _rope_lanes(x, c, s1, s2, half):
    return x * c + pltpu.roll(x, LANES - half, 1) * s1 + pltpu.roll(x, half, 1) * s2


def _store_t(dst_ref, rows, val):
    dst_ref[rows, :] = val.T.astype(dst_ref.dtype)


def _rope_body(aq_ref, iq_ref, ak_ref, av_ref, sm_ref, ca_ref, sa1_ref, sa2_ref, ci_ref, si1_ref, si2_ref,
               aqt_ref, iqt_ref, smt_ref, ako_ref, smo_ref, avt_ref):
    ca, sa1, sa2 = ca_ref[...], sa1_ref[...], sa2_ref[...]
    ci, si1, si2 = ci_ref[...], si1_ref[...], si2_ref[...]
    ha, hi = ATTN_ROT // 2, IDX_ROT // 2
    tk = DSA_TK
    for j in range(ATTN_Q // LANES):
        sl = slice(j * LANES, (j + 1) * LANES)
        r = _rope_lanes(aq_ref[:, sl].astype(F32), ca, sa1, sa2, ha)
        _store_t(aqt_ref, sl, r * (ATTN_HEAD_DIM ** -0.5))
    for j in range(IDX_Q // LANES):
        sl = slice(j * LANES, (j + 1) * LANES)
        r = _rope_lanes(iq_ref[:, sl].astype(F32), ci, si1, si2, hi)
        _store_t(iqt_ref, sl, r * (IDX_HEAD_DIM ** -0.5))
    for j in range(ATTN_KV // LANES):
        sl = slice(j * LANES, (j + 1) * LANES)
        ako_ref[:, sl] = _rope_lanes(ak_ref[:, sl].astype(F32), ca, sa1, sa2, ha).astype(BF16)
        vt = av_ref[:, sl].astype(F32).T
        for c in range(vt.shape[1] // tk):
            avt_ref[c, sl, :] = vt[:, c * tk:(c + 1) * tk].astype(BF16)
    sm = sm_ref[...].astype(F32)
    lane = lax.broadcasted_iota(jnp.int32, sm.shape, 1)
    sm = jnp.where(lane < IDX_HEAD_DIM, _rope_lanes(sm, ci, si1, si2, hi), sm)
    smo_ref[...] = sm.astype(BF16)
    _store_t(smt_ref, slice(0, LANES), sm)


def _rope(proj, seq, tabs_a, tabs_i, *, tr=512):
    n = proj.shape[0]
    nl = seq // tr
    tk = DSA_TK
    tab = pl.BlockSpec((tr, LANES), lambda i: (i % nl, 0))
    return pl.pallas_call(
        _rope_body,
        grid=(n // tr,),
        in_specs=[
            pl.BlockSpec((tr, ATTN_Q), lambda i: (i, COL_AQ // ATTN_Q)),
            pl.BlockSpec((tr, IDX_Q), lambda i: (i, COL_IQ // IDX_Q)),
            pl.BlockSpec((tr, ATTN_KV), lambda i: (i, COL_AK // ATTN_KV)),
            pl.BlockSpec((tr, ATTN_KV), lambda i: (i, COL_AV // ATTN_KV)),
            pl.BlockSpec((tr, LANES), lambda i: (i, COL_SM // LANES)),
            tab, tab, tab, tab, tab, tab,
        ],
        out_specs=[
            pl.BlockSpec((ATTN_Q, tr), lambda i: (0, i)),
            pl.BlockSpec((IDX_Q, tr), lambda i: (0, i)),
            pl.BlockSpec((LANES, tr), lambda i: (0, i)),
            pl.BlockSpec((tr, ATTN_KV), lambda i: (i, 0)),
            pl.BlockSpec((tr, LANES), lambda i: (i, 0)),
            pl.BlockSpec((tr // tk, ATTN_KV, tk), lambda i: (i, 0, 0)),
        ],
        out_shape=[
            jax.ShapeDtypeStruct((ATTN_Q, n), BF16),
            jax.ShapeDtypeStruct((IDX_Q, n), BF16),
            jax.ShapeDtypeStruct((LANES, n), BF16),
            jax.ShapeDtypeStruct((n, ATTN_KV), BF16),
            jax.ShapeDtypeStruct((n, LANES), BF16),
            jax.ShapeDtypeStruct((n // tk, ATTN_KV, tk), BF16),
        ],
        compiler_params=_cparams(("parallel",)),
        name="rope",
    )(proj, proj, proj, proj, proj, *tabs_a, *tabs_i)


def _sortable(x):
    i = pltpu.bitcast(x + 0.0, jnp.int32)
    return jnp.where(i < 0, i ^ jnp.int32(0x7FFFFFFF), i)


def _colsum(x):
    r, c = x.shape
    return jnp.sum(jnp.sum(x.reshape(r // 8, 8, c), axis=0), axis=0, keepdims=True)


def _dsa_body(iqt_ref, smt_ref, aqt_ref, smk_ref, ak_ref, avt_ref, o_ref,
              key_ref, m_ref, l_ref, acc_ref, *, topk, seq):
    tq, tk = DSA_TQ, DSA_TK
    qi = pl.program_id(1)
    t0 = qi * tq
    nkb = (t0 + tq + tk - 1) // tk
    iwt = smt_ref[SM_IW:SM_IW + IDX_HEADS, :].astype(F32) * (IDX_HEADS ** -0.5)
    qpos = lax.broadcasted_iota(jnp.int32, (tk, tq), 1) + t0
    kiota = lax.broadcasted_iota(jnp.int32, (tk, tq), 0)

    def idx_block(kb, carry):
        k0 = pl.multiple_of(kb * tk, tk)
        ik = smk_ref[pl.ds(k0, tk), :][:, SM_IK:SM_IK + IDX_HEAD_DIM]
        score = jnp.zeros((tk, tq), F32)
        for h in range(IDX_HEADS):
            rel = _dot(ik, iqt_ref[h * IDX_HEAD_DIM:(h + 1) * IDX_HEAD_DIM, :])
            score = score + jnp.maximum(rel, 0.0) * iwt[h:h + 1, :]
        key_ref[kb] = jnp.where(kiota + k0 <= qpos, _sortable(score), jnp.int32(INT_MIN))
        return carry

    lax.fori_loop(0, nkb, idx_block, 0)

    @pl.when(nkb % 2 == 1)
    def _():
        key_ref[nkb] = jnp.full((tk, tq), INT_MIN, jnp.int32)

    def count(pred):
        def pair(i, c):
            for u in range(2):
                kb = 2 * i + u
                hit = jnp.where(pred(key_ref[kb], kb), 1.0, 0.0)
                c = c + jnp.sum(hit.reshape(tk // 8, 8, tq), axis=0)
            return c
        c = lax.fori_loop(0, (nkb + 1) // 2, pair, jnp.zeros((8, tq), F32))
        return jnp.sum(c, axis=0, keepdims=True)

    def bisect(it, thr):
        cand = thr + lax.shift_left(jnp.int32(1), jnp.int32(31) - it)
        return jnp.where(count(lambda k, kb: k >= cand) >= float(topk), cand, thr)

    thr = lax.fori_loop(0, 32, bisect, jnp.full((1, tq), INT_MIN, jnp.int32))

    n_gt = count(lambda k, kb: k > thr)
    n_ge = count(lambda k, kb: k >= thr)
    need = float(topk) - n_gt
    tied = (n_ge > float(topk)) & (thr > jnp.int32(INT_MIN))

    def tie_break():
        def step(it, x):
            cand = x + lax.shift_left(jnp.int32(1), jnp.int32(seq.bit_length() - 1) - it)
            below = count(lambda k, kb: (k == thr) & (kiota + kb * tk < cand))
            return jnp.where(below < need, cand, x)
        return lax.fori_loop(0, seq.bit_length(), step, jnp.zeros((1, tq), jnp.int32))

    jtie = lax.cond(jnp.max(jnp.where(tied, 1.0, 0.0)) > 0.0, tie_break,
                    lambda: jnp.zeros((1, tq), jnp.int32))
    jmax = jnp.where(tied, jtie, jnp.where(thr > jnp.int32(INT_MIN), jnp.int32(seq), jnp.int32(-1)))

    m_ref[...] = jnp.full_like(m_ref, NEG_INF)
    l_ref[...] = jnp.zeros_like(l_ref)
    acc_ref[...] = jnp.zeros_like(acc_ref)

    def att_block(kb, carry):
        k0 = pl.multiple_of(kb * tk, tk)
        key = key_ref[kb]
        sel = (key > thr) | ((key == thr) & (kiota + k0 <= jmax))
        bias = jnp.where(sel, 0.0, NEG_INF)
        bias = jnp.concatenate([bias] * ATTN_REP, axis=1)
        for g in range(ATTN_KV_HEADS):
            gs = slice(g * ATTN_HEAD_DIM, (g + 1) * ATTN_HEAD_DIM)
            qt = jnp.concatenate(
                [aqt_ref[(g * ATTN_REP + r) * ATTN_HEAD_DIM:(g * ATTN_REP + r + 1) * ATTN_HEAD_DIM, :]
                 for r in range(ATTN_REP)], axis=1)
            s = _dot(ak_ref[pl.ds(k0, tk), gs], qt) + bias
            m_old = m_ref[g]
            m_new = jnp.maximum(m_old, jnp.max(s, axis=0, keepdims=True))
            alpha = jnp.exp(m_old - m_new)
            p = jnp.exp(s - m_new)
            l_ref[g] = alpha * l_ref[g] + _colsum(p)
            acc_ref[g] = alpha * acc_ref[g] + _dot(avt_ref[kb, gs, :], p.astype(BF16))
            m_ref[g] = m_new
        return carry

    lax.fori_loop(0, nkb, att_block, 0)
    for g in range(ATTN_KV_HEADS):
        ot = acc_ref[g] / l_ref[g]
        for r in range(ATTN_REP):
            hh = g * ATTN_REP + r
            o_ref[:, hh * ATTN_HEAD_DIM:(hh + 1) * ATTN_HEAD_DIM] = ot[:, r * tq:(r + 1) * tq].T.astype(BF16)


def _dsa(aq_t, iq_t, sm_t, ak_r, sm_r, av_t, batch, seq):
    n = ak_r.shape[0]
    tq, tk = DSA_TQ, DSA_TK
    nq = seq // tq
    nkb = seq // tk
    topk = min(TOPK_MAX, seq // 4)
    qmap = lambda b, i: (0, b * nq + i)
    return pl.pallas_call(
        functools.partial(_dsa_body, topk=topk, seq=seq),
        grid=(batch, nq),
        in_specs=[
            pl.BlockSpec((IDX_Q, tq), qmap),
            pl.BlockSpec((LANES, tq), qmap),
            pl.BlockSpec((ATTN_Q, tq), qmap),
            pl.BlockSpec((seq, LANES), lambda b, i: (b, 0)),
            pl.BlockSpec((seq, ATTN_KV), lambda b, i: (b, 0)),
            pl.BlockSpec((nkb, ATTN_KV, tk), lambda b, i: (b, 0, 0)),
        ],
        out_specs=pl.BlockSpec((tq, ATTN_Q), lambda b, i: (b * nq + i, 0)),
        out_shape=jax.ShapeDtypeStruct((n, ATTN_Q), BF16),
        scratch_shapes=[
            pltpu.VMEM((nkb + nkb % 2, tk, tq), jnp.int32),
            pltpu.VMEM((ATTN_KV_HEADS, 1, ATTN_REP * tq), F32),
            pltpu.VMEM((ATTN_KV_HEADS, 1, ATTN_REP * tq), F32),
            pltpu.VMEM((ATTN_KV_HEADS, ATTN_HEAD_DIM, ATTN_REP * tq), F32),
        ],
        compiler_params=_cparams(("parallel", "arbitrary")),
        name="dsa",
    )(iq_t, sm_t, aq_t, sm_r, ak_r, av_t)


GLA_ROWS = 256


def _split3(x):
    hi = x.astype(BF16)
    r1 = x - hi.astype(F32)
    mid = r1.astype(BF16)
    lo = (r1 - mid.astype(F32)).astype(BF16)
    return hi, mid, lo


def _gla_body(q_ref, k_ref, v_ref, r_ref, sm_ref, wa_ref, ba_ref, gn_ref, o_ref, st_ref):
    rows, c = GLA_ROWS, GLA_CHUNK

    @pl.when(pl.program_id(2) == 0)
    def _():
        st_ref[...] = jnp.zeros_like(st_ref)

    ga = sm_ref[:, SM_GA:SM_GA + GLA_GATE_RANK]
    z = _dot(ga, wa_ref[...]) + ba_ref[...]
    g = -(jnp.maximum(-z, 0.0) + jnp.log1p(jnp.exp(-jnp.abs(z)))) * (1.0 / GLA_GATE_NORM)
    ri = lax.broadcasted_iota(jnp.int32, (rows, rows), 0)
    ci = lax.broadcasted_iota(jnp.int32, (rows, rows), 1)
    tri = jnp.where((ri // c == ci // c) & (ci <= ri), 1.0, 0.0).astype(BF16)
    hi, mid, lo = _split3(g)
    b = _dot(tri, hi) + _dot(tri, mid) + _dot(tri, lo)
    low = lax.broadcasted_iota(jnp.int32, (c, c), 1) <= lax.broadcasted_iota(jnp.int32, (c, c), 0)
    gn = gn_ref[...]
    for ch in range(rows // c):
        sl = slice(ch * c, (ch + 1) * c)
        bc = b[sl]
        bl = bc[c - 1:c, :]
        q = q_ref[sl, :].astype(F32) * (GLA_DK ** -0.5)
        k = k_ref[sl, :].astype(F32)
        v = v_ref[sl, :]
        qe = (q * jnp.exp(bc - bl)).astype(BF16)
        kd = (k * jnp.exp(bl - bc)).astype(BF16)
        qb = (q * jnp.exp(bc)).astype(BF16)
        a = jnp.where(low, _dot_nt(qe, kd), 0.0).astype(BF16)
        st = st_ref[...]
        o = _dot(a, v) + _dot_nt(qb, st.astype(BF16))
        vt = v.astype(F32).T.astype(BF16)
        st_ref[...] = st * jnp.exp(bl) + _dot(vt, kd)
        o = _rms(o, gn)
        gr = r_ref[sl, :].astype(F32)
        o_ref[sl, :] = (o * (gr * jax.nn.sigmoid(gr))).astype(BF16)


def _gla(proj, wa2, ba, gn, batch, seq):
    n = proj.shape[0]
    rows = GLA_ROWS
    ng = seq // rows
    rmap = lambda off, w: (lambda b, h, i: (b * ng + i, off // w + h))
    return pl.pallas_call(
        _gla_body,
        grid=(batch, GLA_HEADS, ng),
        in_specs=[
            pl.BlockSpec((rows, GLA_DK), rmap(COL_GQ, GLA_DK)),
            pl.BlockSpec((rows, GLA_DK), rmap(COL_GK, GLA_DK)),
            pl.BlockSpec((rows, GLA_DV), rmap(COL_GV, GLA_DV)),
            pl.BlockSpec((rows, GLA_DV), rmap(COL_GR, GLA_DV)),
            pl.BlockSpec((rows, LANES), lambda b, h, i: (b * ng + i, COL_SM // LANES)),
            pl.BlockSpec((GLA_GATE_RANK, GLA_DK), lambda b, h, i: (0, h)),
            pl.BlockSpec((1, GLA_DK), lambda b, h, i: (0, h)),
            pl.BlockSpec((1, GLA_DV), lambda b, h, i: (0, 0)),
        ],
        out_specs=pl.BlockSpec((rows, GLA_DV), lambda b, h, i: (b * ng + i, h)),
        out_shape=jax.ShapeDtypeStruct((n, GLA_V), BF16),
        scratch_shapes=[pltpu.VMEM((GLA_DV, GLA_DK), F32)],
        compiler_params=_cparams(("parallel", "parallel", "arbitrary")),
        name="gla",
    )(proj, proj, proj, proj, proj, wa2, ba, gn)


def _merge_body(x_ref, oa_ref, og_ref, ma_ref, mg_ref, wua_ref, wug_ref, wo_ref, o_ref):
    ya = jax.nn.sigmoid(ma_ref[...].astype(F32)) * _dot(oa_ref[...], wua_ref[...])
    yg = jax.nn.sigmoid(mg_ref[...].astype(F32)) * _dot(og_ref[...], wug_ref[...])
    y = (ya + yg).astype(BF16)
    o_ref[...] = x_ref[...] + _dot(y, wo_ref[...])


def _merge(x, o_attn, o_gla, proj, wua, wug, wo, *, tm=256):
    n, d = x.shape
    const = lambda i: (0, 0)
    return pl.pallas_call(
        _merge_body,
        grid=(n // tm,),
        in_specs=[
            pl.BlockSpec((tm, d), lambda i: (i, 0)),
            pl.BlockSpec((tm, ATTN_Q), lambda i: (i, 0)),
            pl.BlockSpec((tm, GLA_V), lambda i: (i, 0)),
            pl.BlockSpec((tm, d), lambda i: (i, COL_MA // D_MODEL)),
            pl.BlockSpec((tm, d), lambda i: (i, COL_MG // D_MODEL)),
            pl.BlockSpec((ATTN_Q, d), const),
            pl.BlockSpec((GLA_V, d), const),
            pl.BlockSpec((d, d), const),
        ],
        out_specs=pl.BlockSpec((tm, d), lambda i: (i, 0)),
        out_shape=jax.ShapeDtypeStruct((n, d), F32),
        compiler_params=_cparams(("parallel",)),
        name="merge",
    )(x, o_attn, o_gla, proj, proj, wua, wug, wo)


def _reorder_w_in(w):
    (aq, ak, av, iq, ik, iw, gq, gk, gv, ga, gr, m_a, m_g) = jnp.split(w, IN_OFFSETS, axis=-1)
    d = w.shape[0]
    small = jnp.concatenate(
        [ik, iw, ga, jnp.zeros((d, LANES - IDX_HEAD_DIM - IDX_HEADS - GLA_GATE_RANK), w.dtype)], axis=-1)
    cols = jnp.concatenate([aq, iq, gv, gr, m_a, m_g, gq, gk, ak, av, small], axis=-1)
    cols = jnp.pad(cols, ((0, 0), (0, PROJ_COLS - cols.shape[1])))
    return cols.astype(BF16)


def kernel(x, ffn1_norm, ffn1_w_gate, ffn1_w_up, ffn1_w_down, mix_norm, w_in, gla_w_a2, gla_b_a, gla_norm, w_up_attn, w_up_gla, w_out, ffn2_norm, ffn2_w_gate, ffn2_w_up, ffn2_w_down, final_norm):
    batch, seq, d = x.shape
    depth = w_in.shape[0]
    tabs_a = _rope_tables(seq, ATTN_HEAD_DIM, ATTN_ROT)
    tabs_i = _rope_tables(seq, IDX_HEAD_DIM, IDX_ROT)
    fg = final_norm.reshape(1, d)
    h = x.reshape(batch * seq, d)
    for l in range(depth):
        h = _ffn(h, ffn1_norm[l].reshape(1, d), ffn1_w_gate[l].astype(BF16), ffn1_w_up[l].astype(BF16),
                 ffn1_w_down[l].astype(BF16), fg, final_norm=False)
        proj = _proj(h, mix_norm[l].reshape(1, d), _reorder_w_in(w_in[l]))
        aq_t, iq_t, sm_t, ak_r, sm_r, av_t = _rope(proj, seq, tabs_a, tabs_i)
        o_attn = _dsa(aq_t, iq_t, sm_t, ak_r, sm_r, av_t, batch, seq)
        o_gla = _gla(proj, gla_w_a2[l].astype(BF16), gla_b_a[l].reshape(1, GLA_K),
                     gla_norm[l].reshape(1, GLA_DV), batch, seq)
        h = _merge(h, o_attn, o_gla, proj, w_up_attn[l].astype(BF16), w_up_gla[l].astype(BF16),
                   w_out[l].astype(BF16))
        h = _ffn(h, ffn2_norm[l].reshape(1, d), ffn2_w_gate[l].astype(BF16), ffn2_w_up[l].astype(BF16),
                 ffn2_w_down[l].astype(BF16), fg, final_norm=(l == depth - 1))
    return h.reshape(batch, seq, d)
```

```python
import functools

import jax
import jax.numpy as jnp
import numpy as np
from jax import lax
from jax.experimental import pallas as pl
from jax.experimental.pallas import tpu as pltpu

F32 = jnp.float32
BF16 = jnp.bfloat16

D_MODEL = 2048
D_FF = 5632
FFN_RES = 0.5
ATTN_HEADS = 8
ATTN_KV_HEADS = 2
ATTN_HEAD_DIM = 128
ATTN_REP = ATTN_HEADS // ATTN_KV_HEADS
IDX_HEADS = 16
IDX_HEAD_DIM = 64
TOPK_MAX = 256
GLA_HEADS = 4
GLA_DK = 128
GLA_DV = 256
GLA_GATE_RANK = 16
GLA_GATE_NORM = 16.0
GLA_CHUNK = 64
ROPE_THETA = 500000.0
ATTN_ROT = ATTN_HEAD_DIM // 4
IDX_ROT = IDX_HEAD_DIM // 4
EPS = 1e-6
NEG_INF = -1e30
INT_MIN = -(2 ** 31)
HALF_MIN = -(2 ** 15)
PACK16 = 16

ATTN_Q = ATTN_HEADS * ATTN_HEAD_DIM
ATTN_KV = ATTN_KV_HEADS * ATTN_HEAD_DIM
IDX_Q = IDX_HEADS * IDX_HEAD_DIM
GLA_K = GLA_HEADS * GLA_DK
GLA_V = GLA_HEADS * GLA_DV
IN_SPLITS = (ATTN_Q, ATTN_KV, ATTN_KV, IDX_Q, IDX_HEAD_DIM, IDX_HEADS,
             GLA_K, GLA_K, GLA_V, GLA_GATE_RANK, GLA_V, D_MODEL, D_MODEL)
IN_OFFSETS = tuple(int(v) for v in np.cumsum(IN_SPLITS)[:-1])

LANES = 128
COL_AQ = 0
COL_IQ = COL_AQ + ATTN_Q
COL_GV = COL_IQ + IDX_Q
COL_GR = COL_GV + GLA_V
COL_MA = COL_GR + GLA_V
COL_MG = COL_MA + D_MODEL
COL_GQ = COL_MG + D_MODEL
COL_GK = COL_GQ + GLA_K
COL_AK = COL_GK + GLA_K
COL_AV = COL_AK + ATTN_KV
COL_SM = COL_AV + ATTN_KV
SM_IK = 0
SM_IW = SM_IK + IDX_HEAD_DIM
SM_GA = SM_IW + IDX_HEADS
PROJ_TN = 1024
PROJ_COLS = -(-(COL_SM + LANES) // PROJ_TN) * PROJ_TN

DSA_TQ = 128
DSA_TK = 256
VT_ROWS = ATTN_HEAD_DIM + PACK16
LOG2E = 1.4426950408889634

VMEM_LIMIT = 56 * 1024 * 1024


def _cparams(sem):
    return pltpu.CompilerParams(dimension_semantics=sem, vmem_limit_bytes=VMEM_LIMIT)


def _dot(a, b):
    return jnp.dot(a, b, preferred_element_type=F32)


def _dot_nt(a, b):
    return lax.dot_general(a, b, (((1,), (1,)), ((), ())), preferred_element_type=F32)


def _rms(x, g):
    return x * lax.rsqrt(jnp.mean(x * x, axis=-1, keepdims=True) + EPS) * g


def _ffn_body(x_ref, g_ref, wg_ref, wu_ref, wd_ref, fg_ref, o_ref, h_ref, acc_ref, *, final_norm):
    f = pl.program_id(1)

    @pl.when(f == 0)
    def _():
        h_ref[...] = _rms(x_ref[...], g_ref[...]).astype(BF16)
        acc_ref[...] = jnp.zeros_like(acc_ref)

    h = h_ref[...]
    gate = _dot(h, wg_ref[...])
    up = _dot(h, wu_ref[...])
    act = (gate * jax.nn.sigmoid(gate) * up).astype(BF16)
    acc_ref[...] += _dot(act, wd_ref[...])

    @pl.when(f == pl.num_programs(1) - 1)
    def _():
        y = x_ref[...] + FFN_RES * acc_ref[...]
        if final_norm:
            y = _rms(y, fg_ref[...])
        o_ref[...] = y


def _ffn(x, g, wg, wu, wd, fg, *, final_norm, tm=512, tf=512):
    n, d = x.shape
    f = wg.shape[1]
    return pl.pallas_call(
        functools.partial(_ffn_body, final_norm=final_norm),
        grid=(n // tm, f // tf),
        in_specs=[
            pl.BlockSpec((tm, d), lambda i, j: (i, 0)),
            pl.BlockSpec((1, d), lambda i, j: (0, 0)),
            pl.BlockSpec((d, tf), lambda i, j: (0, j)),
            pl.BlockSpec((d, tf), lambda i, j: (0, j)),
            pl.BlockSpec((tf, d), lambda i, j: (j, 0)),
            pl.BlockSpec((1, d), lambda i, j: (0, 0)),
        ],
        out_specs=pl.BlockSpec((tm, d), lambda i, j: (i, 0)),
        out_shape=jax.ShapeDtypeStruct((n, d), F32),
        scratch_shapes=[pltpu.VMEM((tm, d), BF16), pltpu.VMEM((tm, d), F32)],
        compiler_params=_cparams(("parallel", "arbitrary")),
        name="ffn",
    )(x, g, wg, wu, wd, fg)


def _proj_body(x_ref, g_ref, w_ref, o_ref, h_ref):
    @pl.when(pl.program_id(1) == 0)
    def _():
        h_ref[...] = _rms(x_ref[...], g_ref[...]).astype(BF16)

    o_ref[...] = _dot(h_ref[...], w_ref[...]).astype(o_ref.dtype)


def _proj(x, g, w, *, tm=512, tn=PROJ_TN):
    n, d = x.shape
    c = w.shape[1]
    return pl.pallas_call(
        _proj_body,
        grid=(n // tm, c // tn),
        in_specs=[
            pl.BlockSpec((tm, d), lambda i, j: (i, 0)),
            pl.BlockSpec((1, d), lambda i, j: (0, 0)),
            pl.BlockSpec((d, tn), lambda i, j: (0, j)),
        ],
        out_specs=pl.BlockSpec((tm, tn), lambda i, j: (i, j)),
        out_shape=jax.ShapeDtypeStruct((n, c), BF16),
        scratch_shapes=[pltpu.VMEM((tm, d), BF16)],
        compiler_params=_cparams(("parallel", "arbitrary")),
        name="proj",
    )(x, g, w)


def _rope_tables(length, head_dim, rot):
    half = rot // 2
    inv = ROPE_THETA ** (-jnp.arange(0, rot, 2, dtype=F32) / rot)
    ang = jnp.arange(length, dtype=F32)[:, None] * inv[None, :]
    cos, sin = jnp.cos(ang), jnp.sin(ang)
    zeros = jnp.zeros((length, head_dim - rot), F32)
    zh = jnp.zeros((length, half), F32)
    c = jnp.concatenate([cos, cos, zeros + 1.0], axis=1)
    s1 = jnp.concatenate([-sin, zh, zeros], axis=1)
    s2 = jnp.concatenate([zh, sin, zeros], axis=1)
    reps = LANES // head_dim
    return tuple(jnp.tile(t, (1, reps)) for t in (c, s1, s2))


def _rope_lanes(x, c, s1, s2, half):
    return x * c + pltpu.roll(x, LANES - half, 1) * s1 + pltpu.roll(x, half, 1) * s2


def _store_t(dst_ref, rows, val):
    dst_ref[rows, :] = val.T.astype(dst_ref.dtype)


def _rope_body(aq_ref, iq_ref, ak_ref, av_ref, sm_ref, ca_ref, sa1_ref, sa2_ref, ci_ref, si1_ref, si2_ref,
               aqt_ref, iqt_ref, smt_ref, ako_ref, smo_ref, avt_ref):
    ca, sa1, sa2 = ca_ref[...], sa1_ref[...], sa2_ref[...]
    ci, si1, si2 = ci_ref[...], si1_ref[...], si2_ref[...]
    ha, hi = ATTN_ROT // 2, IDX_ROT // 2
    tk = DSA_TK
    for j in range(ATTN_Q // LANES):
        sl = slice(j * LANES, (j + 1) * LANES)
        r = _rope_lanes(aq_ref[:, sl].astype(F32), ca, sa1, sa2, ha)
        _store_t(aqt_ref, sl, r * (ATTN_HEAD_DIM ** -0.5 * LOG2E))
    for j in range(IDX_Q // LANES):
        sl = slice(j * LANES, (j + 1) * LANES)
        r = _rope_lanes(iq_ref[:, sl].astype(F32), ci, si1, si2, hi)
        _store_t(iqt_ref, sl, r * (IDX_HEAD_DIM ** -0.5))
    for j in range(ATTN_KV // LANES):
        sl = slice(j * LANES, (j + 1) * LANES)
        ako_ref[:, sl] = _rope_lanes(ak_ref[:, sl].astype(F32), ca, sa1, sa2, ha).astype(BF16)
        vt = av_ref[:, sl].astype(F32).T
        for c in range(vt.shape[1] // tk):
            avt_ref[c, j * VT_ROWS:j * VT_ROWS + LANES, :] = vt[:, c * tk:(c + 1) * tk].astype(BF16)
            avt_ref[c, j * VT_ROWS + LANES:(j + 1) * VT_ROWS, :] = jnp.ones((VT_ROWS - LANES, tk), BF16)
    sm = sm_ref[...].astype(F32)
    lane = lax.broadcasted_iota(jnp.int32, sm.shape, 1)
    sm = jnp.where(lane < IDX_HEAD_DIM, _rope_lanes(sm, ci, si1, si2, hi), sm)
    smo_ref[...] = sm.astype(BF16)
    _store_t(smt_ref, slice(0, LANES), sm)


def _rope(proj, seq, tabs_a, tabs_i, *, tr=512):
    n = proj.shape[0]
    nl = seq // tr
    tk = DSA_TK
    tab = pl.BlockSpec((tr, LANES), lambda i: (i % nl, 0))
    return pl.pallas_call(
        _rope_body,
        grid=(n // tr,),
        in_specs=[
            pl.BlockSpec((tr, ATTN_Q), lambda i: (i, COL_AQ // ATTN_Q)),
            pl.BlockSpec((tr, IDX_Q), lambda i: (i, COL_IQ // IDX_Q)),
            pl.BlockSpec((tr, ATTN_KV), lambda i: (i, COL_AK // ATTN_KV)),
            pl.BlockSpec((tr, ATTN_KV), lambda i: (i, COL_AV // ATTN_KV)),
            pl.BlockSpec((tr, LANES), lambda i: (i, COL_SM // LANES)),
            tab, tab, tab, tab, tab, tab,
        ],
        out_specs=[
            pl.BlockSpec((ATTN_Q, tr), lambda i: (0, i)),
            pl.BlockSpec((IDX_Q, tr), lambda i: (0, i)),
            pl.BlockSpec((LANES, tr), lambda i: (0, i)),
            pl.BlockSpec((tr, ATTN_KV), lambda i: (i, 0)),
            pl.BlockSpec((tr, LANES), lambda i: (i, 0)),
            pl.BlockSpec((tr // tk, ATTN_KV_HEADS * VT_ROWS, tk), lambda i: (i, 0, 0)),
        ],
        out_shape=[
            jax.ShapeDtypeStruct((ATTN_Q, n), BF16),
            jax.ShapeDtypeStruct((IDX_Q, n), BF16),
            jax.ShapeDtypeStruct((LANES, n), BF16),
            jax.ShapeDtypeStruct((n, ATTN_KV), BF16),
            jax.ShapeDtypeStruct((n, LANES), BF16),
            jax.ShapeDtypeStruct((n // tk, ATTN_KV_HEADS * VT_ROWS, tk), BF16),
        ],
        compiler_params=_cparams(("parallel",)),
        name="rope",
    )(proj, proj, proj, proj, proj, *tabs_a, *tabs_i)


def _sortable(x):
    i = pltpu.bitcast(x + 0.0, jnp.int32)
    return jnp.where(i < 0, i ^ jnp.int32(0x7FFFFFFF), i)


def _tree_sum(xs):
    while len(xs) > 1:
        xs = [xs[i] + xs[i + 1] for i in range(0, len(xs) - 1, 2)] + ([xs[-1]] if len(xs) % 2 else [])
    return xs[0]


def _colsum(x):
    r, c = x.shape
    return jnp.sum(jnp.sum(x.reshape(r // 8, 8, c), axis=0), axis=0, keepdims=True)


def _dsa_body(iqt_ref, smt_ref, aqt_ref, smk_ref, ak_ref, avt_ref, o_ref,
              key_ref, hi_ref, lo_ref, m_ref, al_ref, acc_ref, s0_ref, s1_ref, p0_ref, p1_ref, *, topk, seq):
    tq, tk = DSA_TQ, DSA_TK
    qi = pl.program_id(1)
    t0 = qi * tq
    nkb = (t0 + tq + tk - 1) // tk
    iwt = smt_ref[SM_IW:SM_IW + IDX_HEADS, :].astype(F32) * (IDX_HEADS ** -0.5)
    qpos = lax.broadcasted_iota(jnp.int32, (tk, tq), 1) + t0
    kiota = lax.broadcasted_iota(jnp.int32, (tk, tq), 0)

    def idx_block(kb, carry):
        k0 = pl.multiple_of(kb * tk, tk)
        ik = smk_ref[pl.ds(k0, tk), :][:, SM_IK:SM_IK + IDX_HEAD_DIM]
        score = jnp.zeros((tk, tq), F32)
        for h in range(0, IDX_HEADS, 2):
            qq = jnp.concatenate([iqt_ref[(h + u) * IDX_HEAD_DIM:(h + u + 1) * IDX_HEAD_DIM, :]
                                  for u in range(2)], axis=1)
            rel = jnp.maximum(_dot(ik, qq), 0.0)
            score = score + rel[:, :tq] * iwt[h:h + 1, :] + rel[:, tq:] * iwt[h + 1:h + 2, :]
        key = jnp.where(kiota + k0 <= qpos, _sortable(score), jnp.int32(INT_MIN))
        key_ref[kb] = key
        hi_ref[kb] = (key >> 16).astype(jnp.int16)
        lo_ref[kb] = ((key & 0xFFFF) + HALF_MIN).astype(jnp.int16)
        return carry

    lax.fori_loop(0, nkb, idx_block, 0)

    @pl.when(nkb % 2 == 1)
    def _():
        key_ref[nkb] = jnp.full((tk, tq), INT_MIN, jnp.int32)
        hi_ref[nkb] = jnp.full((tk, tq), HALF_MIN, jnp.int16)
        lo_ref[nkb] = jnp.full((tk, tq), HALF_MIN, jnp.int16)

    npair = (nkb + 1) // 2

    def count(pred):
        def pair(i, c):
            for u in range(2):
                kb = 2 * i + u
                hit = jnp.where(pred(key_ref[kb], kb), 1.0, 0.0)
                c = c + jnp.sum(hit.reshape(tk // 8, 8, tq), axis=0)
            return c
        c = lax.fori_loop(0, npair, pair, jnp.zeros((8, tq), F32))
        return jnp.sum(c, axis=0, keepdims=True)

    def pack16(v):
        return jnp.broadcast_to(v, (PACK16, tq)).astype(jnp.int16)

    def count16(ref, pred):
        one, zero = jnp.int16(1), jnp.int16(0)

        def pair(i, c):
            for u in range(2):
                blk = ref[2 * i + u]
                c = c + _tree_sum([jnp.where(pred(blk[j:j + PACK16]), one, zero)
                                   for j in range(0, tk, PACK16)])
            return c
        c = lax.fori_loop(0, npair, pair, jnp.zeros((PACK16, tq), jnp.int16))
        return jnp.sum(c.astype(F32), axis=0, keepdims=True)

    def bisect16(ref, want):
        def step(it, t):
            cand = t + lax.shift_left(jnp.int32(1), jnp.int32(15) - it)
            c16 = pack16(cand)
            return jnp.where(count16(ref, lambda b: b >= c16) >= want, cand, t)
        return lax.fori_loop(0, 16, step, jnp.full((1, tq), HALF_MIN, jnp.int32))

    kf = float(topk)
    t_hi = bisect16(hi_ref, kf)
    t_hi16 = pack16(t_hi)
    n_hi_gt = count16(hi_ref, lambda b: b > t_hi16)

    def mask_low(i, carry):
        for u in range(2):
            kb = 2 * i + u
            for j in range(0, tk, PACK16):
                rows = slice(j, j + PACK16)
                lo_ref[kb, rows, :] = jnp.where(hi_ref[kb, rows, :] == t_hi16, lo_ref[kb, rows, :],
                                                jnp.int16(HALF_MIN))
        return carry

    lax.fori_loop(0, npair, mask_low, 0)
    t_lo = bisect16(lo_ref, kf - n_hi_gt)
    t_lo16 = pack16(t_lo)
    thr = t_hi * 65536 + (t_lo - HALF_MIN)

    n_gt = n_hi_gt + count16(lo_ref, lambda b: b > t_lo16)
    n_ge = n_hi_gt + count16(lo_ref, lambda b: b >= t_lo16)
    need = kf - n_gt
    tied = (n_ge > kf) & (thr > jnp.int32(INT_MIN))

    def tie_break():
        def step(it, x):
            cand = x + lax.shift_left(jnp.int32(1), jnp.int32(seq.bit_length() - 1) - it)
            below = count(lambda k, kb: (k == thr) & (kiota + kb * tk < cand))
            return jnp.where(below < need, cand, x)
        return lax.fori_loop(0, seq.bit_length(), step, jnp.zeros((1, tq), jnp.int32))

    jtie = lax.cond(jnp.max(jnp.where(tied, 1.0, 0.0)) > 0.0, tie_break,
                    lambda: jnp.zeros((1, tq), jnp.int32))
    jmax = jnp.where(tied, jtie, jnp.where(thr > jnp.int32(INT_MIN), jnp.int32(seq), jnp.int32(-1)))

    last = nkb + nkb % 2 - 1

    def scores(kb, s_ref):
        kb = jnp.minimum(kb, last)
        kc = jnp.minimum(kb, seq // tk - 1)
        k0 = pl.multiple_of(kc * tk, tk)
        key = key_ref[kb]
        sel = (key > thr) | ((key == thr) & (kiota + kb * tk <= jmax))
        bias = jnp.where(sel, 0.0, NEG_INF)
        bias = jnp.concatenate([bias] * ATTN_REP, axis=1)
        for g in range(ATTN_KV_HEADS):
            kg = ak_ref[pl.ds(k0, tk), g * ATTN_HEAD_DIM:(g + 1) * ATTN_HEAD_DIM]
            qt = jnp.concatenate(
                [aqt_ref[(g * ATTN_REP + r) * ATTN_HEAD_DIM:(g * ATTN_REP + r + 1) * ATTN_HEAD_DIM, :]
                 for r in range(ATTN_REP)], axis=1)
            s_ref[g] = _dot(kg, qt) + bias

    def softmax(s_ref, p_ref):
        for g in range(ATTN_KV_HEADS):
            for r in range(ATTN_REP):
                cols = slice(r * tq, (r + 1) * tq)
                s = s_ref[g, :, cols]
                m_old = m_ref[g, :, cols]
                m_new = jnp.maximum(m_old, jnp.max(s, axis=0, keepdims=True))
                p_ref[g, :, cols] = jnp.exp2(s - m_new).astype(BF16)
                al_ref[g, :, cols] = jnp.exp2(m_old - m_new)
                m_ref[g, :, cols] = m_new

    def values(kb, p_ref):
        kc = jnp.clip(kb, 0, seq // tk - 1)
        for g in range(ATTN_KV_HEADS):
            vt = avt_ref[kc, g * VT_ROWS:(g + 1) * VT_ROWS, :]
            acc_ref[g] = al_ref[g] * acc_ref[g] + _dot(vt, p_ref[g])

    m_ref[...] = jnp.full_like(m_ref, NEG_INF)
    acc_ref[...] = jnp.zeros_like(acc_ref)
    al_ref[...] = jnp.ones_like(al_ref)
    p1_ref[...] = jnp.zeros_like(p1_ref)
    scores(0, s0_ref)

    def att_pair(i, carry):
        j = 2 * i
        values(j - 1, p1_ref)
        softmax(s0_ref, p0_ref)
        scores(j + 1, s1_ref)
        values(j, p0_ref)
        softmax(s1_ref, p1_ref)
        scores(j + 2, s0_ref)
        return carry

    lax.fori_loop(0, npair, att_pair, 0)
    values(last, p1_ref)
    for g in range(ATTN_KV_HEADS):
        acc = acc_ref[g]
        ot = acc[:ATTN_HEAD_DIM] / acc[ATTN_HEAD_DIM:ATTN_HEAD_DIM + 1]
        for r in range(ATTN_REP):
            hh = g * ATTN_REP + r
            o_ref[:, hh * ATTN_HEAD_DIM:(hh + 1) * ATTN_HEAD_DIM] = ot[:, r * tq:(r + 1) * tq].T.astype(BF16)


def _dsa(aq_t, iq_t, sm_t, ak_r, sm_r, av_t, batch, seq):
    n = ak_r.shape[0]
    tq, tk = DSA_TQ, DSA_TK
    nq = seq // tq
    nkb = seq // tk
    topk = min(TOPK_MAX, seq // 4)
    qmap = lambda b, i: (0, b * nq + i)
    return pl.pallas_call(
        functools.partial(_dsa_body, topk=topk, seq=seq),
        grid=(batch, nq),
        in_specs=[
            pl.BlockSpec((IDX_Q, tq), qmap),
            pl.BlockSpec((LANES, tq), qmap),
            pl.BlockSpec((ATTN_Q, tq), qmap),
            pl.BlockSpec((seq, LANES), lambda b, i: (b, 0)),
            pl.BlockSpec((seq, ATTN_KV), lambda b, i: (b, 0)),
            pl.BlockSpec((nkb, ATTN_KV_HEADS * VT_ROWS, tk), lambda b, i: (b, 0, 0)),
        ],
        out_specs=pl.BlockSpec((tq, ATTN_Q), lambda b, i: (b * nq + i, 0)),
        out_shape=jax.ShapeDtypeStruct((n, ATTN_Q), BF16),
        scratch_shapes=[
            pltpu.VMEM((nkb + nkb % 2, tk, tq), jnp.int32),
            pltpu.VMEM((nkb + nkb % 2, tk, tq), jnp.int16),
            pltpu.VMEM((nkb + nkb % 2, tk, tq), jnp.int16),
            pltpu.VMEM((ATTN_KV_HEADS, 1, ATTN_REP * tq), F32),
            pltpu.VMEM((ATTN_KV_HEADS, 1, ATTN_REP * tq), F32),
            pltpu.VMEM((ATTN_KV_HEADS, VT_ROWS, ATTN_REP * tq), F32),
            pltpu.VMEM((ATTN_KV_HEADS, tk, ATTN_REP * tq), F32),
            pltpu.VMEM((ATTN_KV_HEADS, tk, ATTN_REP * tq), F32),
            pltpu.VMEM((ATTN_KV_HEADS, tk, ATTN_REP * tq), BF16),
            pltpu.VMEM((ATTN_KV_HEADS, tk, ATTN_REP * tq), BF16),
        ],
        compiler_params=_cparams(("parallel", "arbitrary")),
        name="dsa",
    )(iq_t, sm_t, aq_t, sm_r, ak_r, av_t)


GLA_ROWS = 256


def _split3(x):
    hi = x.astype(BF16)
    r1 = x - hi.astype(F32)
    mid = r1.astype(BF16)
    lo = (r1 - mid.astype(F32)).astype(BF16)
    return hi, mid, lo


def _gla_body(q_ref, k_ref, v_ref, r_ref, sm_ref, wa_ref, ba_ref, gn_ref, o_ref, st_ref):
    rows, c = GLA_ROWS, GLA_CHUNK

    @pl.when(pl.program_id(2) == 0)
    def _():
        st_ref[...] = jnp.zeros_like(st_ref)

    ga = sm_ref[:, SM_GA:SM_GA + GLA_GATE_RANK]
    z = _dot(ga, wa_ref[...]) + ba_ref[...]
    g = -(jnp.maximum(-z, 0.0) + jnp.log1p(jnp.exp(-jnp.abs(z)))) * (1.0 / GLA_GATE_NORM)
    ri = lax.broadcasted_iota(jnp.int32, (rows, rows), 0)
    ci = lax.broadcasted_iota(jnp.int32, (rows, rows), 1)
    tri = jnp.where((ri // c == ci // c) & (ci <= ri), 1.0, 0.0).astype(BF16)
    hi, mid, lo = _split3(g)
    b = _dot(tri, hi) + _dot(tri, mid) + _dot(tri, lo)
    low = lax.broadcasted_iota(jnp.int32, (c, c), 1) <= lax.broadcasted_iota(jnp.int32, (c, c), 0)
    gn = gn_ref[...]
    for ch in range(rows // c):
        sl = slice(ch * c, (ch + 1) * c)
        bc = b[sl]
        bl = bc[c - 1:c, :]
        q = q_ref[sl, :].astype(F32) * (GLA_DK ** -0.5)
        k = k_ref[sl, :].astype(F32)
        v = v_ref[sl, :]
        qe = (q * jnp.exp(bc - bl)).astype(BF16)
        kd = (k * jnp.exp(bl - bc)).astype(BF16)
        qb = (q * jnp.exp(bc)).astype(BF16)
        a = jnp.where(low, _dot_nt(qe, kd), 0.0).astype(BF16)
        st = st_ref[...]
        o = _dot(a, v) + _dot_nt(qb, st.astype(BF16))
        vt = v.astype(F32).T.astype(BF16)
        st_ref[...] = st * jnp.exp(bl) + _dot(vt, kd)
        o = _rms(o, gn)
        gr = r_ref[sl, :].astype(F32)
        o_ref[sl, :] = (o * (gr * jax.nn.sigmoid(gr))).astype(BF16)


def _gla(proj, wa2, ba, gn, batch, seq):
    n = proj.shape[0]
    rows = GLA_ROWS
    ng = seq // rows
    rmap = lambda off, w: (lambda b, h, i: (b * ng + i, off // w + h))
    return pl.pallas_call(
        _gla_body,
        grid=(batch, GLA_HEADS, ng),
        in_specs=[
            pl.BlockSpec((rows, GLA_DK), rmap(COL_GQ, GLA_DK)),
            pl.BlockSpec((rows, GLA_DK), rmap(COL_GK, GLA_DK)),
            pl.BlockSpec((rows, GLA_DV), rmap(COL_GV, GLA_DV)),
            pl.BlockSpec((rows, GLA_DV), rmap(COL_GR, GLA_DV)),
            pl.BlockSpec((rows, LANES), lambda b, h, i: (b * ng + i, COL_SM // LANES)),
            pl.BlockSpec((GLA_GATE_RANK, GLA_DK), lambda b, h, i: (0, h)),
            pl.BlockSpec((1, GLA_DK), lambda b, h, i: (0, h)),
            pl.BlockSpec((1, GLA_DV), lambda b, h, i: (0, 0)),
        ],
        out_specs=pl.BlockSpec((rows, GLA_DV), lambda b, h, i: (b * ng + i, h)),
        out_shape=jax.ShapeDtypeStruct((n, GLA_V), BF16),
        scratch_shapes=[pltpu.VMEM((GLA_DV, GLA_DK), F32)],
        compiler_params=_cparams(("parallel", "parallel", "arbitrary")),
        name="gla",
    )(proj, proj, proj, proj, proj, wa2, ba, gn)


def _merge_body(x_ref, oa_ref, og_ref, ma_ref, mg_ref, wua_ref, wug_ref, wo_ref, o_ref):
    ya = jax.nn.sigmoid(ma_ref[...].astype(F32)) * _dot(oa_ref[...], wua_ref[...])
    yg = jax.nn.sigmoid(mg_ref[...].astype(F32)) * _dot(og_ref[...], wug_ref[...])
    y = (ya + yg).astype(BF16)
    o_ref[...] = x_ref[...] + _dot(y, wo_ref[...])


def _merge(x, o_attn, o_gla, proj, wua, wug, wo, *, tm=256):
    n, d = x.shape
    const = lambda i: (0, 0)
    return pl.pallas_call(
        _merge_body,
        grid=(n // tm,),
        in_specs=[
            pl.BlockSpec((tm, d), lambda i: (i, 0)),
            pl.BlockSpec((tm, ATTN_Q), lambda i: (i, 0)),
            pl.BlockSpec((tm, GLA_V), lambda i: (i, 0)),
            pl.BlockSpec((tm, d), lambda i: (i, COL_MA // D_MODEL)),
            pl.BlockSpec((tm, d), lambda i: (i, COL_MG // D_MODEL)),
            pl.BlockSpec((ATTN_Q, d), const),
            pl.BlockSpec((GLA_V, d), const),
            pl.BlockSpec((d, d), const),
        ],
        out_specs=pl.BlockSpec((tm, d), lambda i: (i, 0)),
        out_shape=jax.ShapeDtypeStruct((n, d), F32),
        compiler_params=_cparams(("parallel",)),
        name="merge",
    )(x, o_attn, o_gla, proj, proj, wua, wug, wo)


def _reorder_w_in(w):
    (aq, ak, av, iq, ik, iw, gq, gk, gv, ga, gr, m_a, m_g) = jnp.split(w, IN_OFFSETS, axis=-1)
    d = w.shape[0]
    small = jnp.concatenate(
        [ik, iw, ga, jnp.zeros((d, LANES - IDX_HEAD_DIM - IDX_HEADS - GLA_GATE_RANK), w.dtype)], axis=-1)
    cols = jnp.concatenate([aq, iq, gv, gr, m_a, m_g, gq, gk, ak, av, small], axis=-1)
    cols = jnp.pad(cols, ((0, 0), (0, PROJ_COLS - cols.shape[1])))
    return cols


def kernel(x, ffn1_norm, ffn1_w_gate, ffn1_w_up, ffn1_w_down, mix_norm, w_in, gla_w_a2, gla_b_a, gla_norm, w_up_attn, w_up_gla, w_out, ffn2_norm, ffn2_w_gate, ffn2_w_up, ffn2_w_down, final_norm):
    batch, seq, d = x.shape
    depth = w_in.shape[0]
    tabs_a = _rope_tables(seq, ATTN_HEAD_DIM, ATTN_ROT)
    tabs_i = _rope_tables(seq, IDX_HEAD_DIM, IDX_ROT)
    fg = final_norm.reshape(1, d)
    h = x.reshape(batch * seq, d)
    for l in range(depth):
        h = _ffn(h, ffn1_norm[l].reshape(1, d), ffn1_w_gate[l].astype(BF16), ffn1_w_up[l].astype(BF16),
                 ffn1_w_down[l].astype(BF16), fg, final_norm=False)
        proj = _proj(h, mix_norm[l].reshape(1, d), _reorder_w_in(w_in[l].astype(BF16)))
        aq_t, iq_t, sm_t, ak_r, sm_r, av_t = _rope(proj, seq, tabs_a, tabs_i)
        o_attn = _dsa(aq_t, iq_t, sm_t, ak_r, sm_r, av_t, batch, seq)
        o_gla = _gla(proj, gla_w_a2[l].astype(BF16), gla_b_a[l].reshape(1, GLA_K),
                     gla_norm[l].reshape(1, GLA_DV), batch, seq)
        h = _merge(h, o_attn, o_gla, proj, w_up_attn[l].astype(BF16), w_up_gla[l].astype(BF16),
                   w_out[l].astype(BF16))
        h = _ffn(h, ffn2_norm[l].reshape(1, d), ffn2_w_gate[l].astype(BF16), ffn2_w_up[l].astype(BF16),
                 ffn2_w_down[l].astype(BF16), fg, final_norm=(l == depth - 1))
    return h.reshape(batch, seq, d)
```

```python
import functools

import jax
import jax.numpy as jnp
import numpy as np
from jax import lax
from jax.experimental import pallas as pl
from jax.experimental.pallas import tpu as pltpu

F32 = jnp.float32
BF16 = jnp.bfloat16

D_MODEL = 2048
D_FF = 5632
FFN_RES = 0.5
ATTN_HEADS = 8
ATTN_KV_HEADS = 2
ATTN_HEAD_DIM = 128
ATTN_REP = ATTN_HEADS // ATTN_KV_HEADS
IDX_HEADS = 16
IDX_HEAD_DIM = 64
TOPK_MAX = 256
GLA_HEADS = 4
GLA_DK = 128
GLA_DV = 256
GLA_GATE_RANK = 16
GLA_GATE_NORM = 16.0
GLA_CHUNK = 64
ROPE_THETA = 500000.0
ATTN_ROT = ATTN_HEAD_DIM // 4
IDX_ROT = IDX_HEAD_DIM // 4
EPS = 1e-6
NEG_INF = -1e30
INT_MIN = -(2 ** 31)
HALF_MIN = -(2 ** 15)
PACK16 = 16

ATTN_Q = ATTN_HEADS * ATTN_HEAD_DIM
ATTN_KV = ATTN_KV_HEADS * ATTN_HEAD_DIM
IDX_Q = IDX_HEADS * IDX_HEAD_DIM
GLA_K = GLA_HEADS * GLA_DK
GLA_V = GLA_HEADS * GLA_DV
IN_SPLITS = (ATTN_Q, ATTN_KV, ATTN_KV, IDX_Q, IDX_HEAD_DIM, IDX_HEADS,
             GLA_K, GLA_K, GLA_V, GLA_GATE_RANK, GLA_V, D_MODEL, D_MODEL)
IN_OFFSETS = tuple(int(v) for v in np.cumsum(IN_SPLITS)[:-1])

LANES = 128
COL_AQ = 0
COL_IQ = COL_AQ + ATTN_Q
COL_GV = COL_IQ + IDX_Q
COL_GR = COL_GV + GLA_V
COL_MA = COL_GR + GLA_V
COL_MG = COL_MA + D_MODEL
COL_GQ = COL_MG + D_MODEL
COL_GK = COL_GQ + GLA_K
COL_AK = COL_GK + GLA_K
COL_AV = COL_AK + ATTN_KV
COL_SM = COL_AV + ATTN_KV
SM_IK = 0
SM_IW = SM_IK + IDX_HEAD_DIM
SM_GA = SM_IW + IDX_HEADS
PROJ_TN = 1024
PROJ_COLS = -(-(COL_SM + LANES) // PROJ_TN) * PROJ_TN

DSA_TQ = 128
DSA_TK = 256
VT_ROWS = ATTN_HEAD_DIM + PACK16
LOG2E = 1.4426950408889634

VMEM_LIMIT = 56 * 1024 * 1024


def _cparams(sem):
    return pltpu.CompilerParams(dimension_semantics=sem, vmem_limit_bytes=VMEM_LIMIT)


def _dot(a, b):
    return jnp.dot(a, b, preferred_element_type=F32)


def _dot_nt(a, b):
    return lax.dot_general(a, b, (((1,), (1,)), ((), ())), preferred_element_type=F32)


def _rms(x, g):
    return x * lax.rsqrt(jnp.mean(x * x, axis=-1, keepdims=True) + EPS) * g


def _ffn_body(x_ref, g_ref, wg_ref, wu_ref, wd_ref, fg_ref, o_ref, h_ref, acc_ref, *, final_norm):
    f = pl.program_id(1)

    @pl.when(f == 0)
    def _():
        h_ref[...] = _rms(x_ref[...], g_ref[...]).astype(BF16)
        acc_ref[...] = jnp.zeros_like(acc_ref)

    h = h_ref[...]
    gate = _dot(h, wg_ref[...])
    up = _dot(h, wu_ref[...])
    act = (gate * jax.nn.sigmoid(gate) * up).astype(BF16)
    acc_ref[...] += _dot(act, wd_ref[...])

    @pl.when(f == pl.num_programs(1) - 1)
    def _():
        y = x_ref[...] + FFN_RES * acc_ref[...]
        if final_norm:
            y = _rms(y, fg_ref[...])
        o_ref[...] = y


def _ffn(x, g, wg, wu, wd, fg, *, final_norm, tm=512, tf=512):
    n, d = x.shape
    f = wg.shape[1]
    return pl.pallas_call(
        functools.partial(_ffn_body, final_norm=final_norm),
        grid=(n // tm, f // tf),
        in_specs=[
            pl.BlockSpec((tm, d), lambda i, j: (i, 0)),
            pl.BlockSpec((1, d), lambda i, j: (0, 0)),
            pl.BlockSpec((d, tf), lambda i, j: (0, j)),
            pl.BlockSpec((d, tf), lambda i, j: (0, j)),
            pl.BlockSpec((tf, d), lambda i, j: (j, 0)),
            pl.BlockSpec((1, d), lambda i, j: (0, 0)),
        ],
        out_specs=pl.BlockSpec((tm, d), lambda i, j: (i, 0)),
        out_shape=jax.ShapeDtypeStruct((n, d), F32),
        scratch_shapes=[pltpu.VMEM((tm, d), BF16), pltpu.VMEM((tm, d), F32)],
        compiler_params=_cparams(("parallel", "arbitrary")),
        name="ffn",
    )(x, g, wg, wu, wd, fg)


def _proj_body(x_ref, g_ref, w_ref, o_ref, h_ref):
    @pl.when(pl.program_id(1) == 0)
    def _():
        h_ref[...] = _rms(x_ref[...], g_ref[...]).astype(BF16)

    o_ref[...] = _dot(h_ref[...], w_ref[...]).astype(o_ref.dtype)


def _proj(x, g, w, *, tm=512, tn=PROJ_TN):
    n, d = x.shape
    c = w.shape[1]
    return pl.pallas_call(
        _proj_body,
        grid=(n // tm, c // tn),
        in_specs=[
            pl.BlockSpec((tm, d), lambda i, j: (i, 0)),
            pl.BlockSpec((1, d), lambda i, j: (0, 0)),
            pl.BlockSpec((d, tn), lambda i, j: (0, j)),
        ],
        out_specs=pl.BlockSpec((tm, tn), lambda i, j: (i, j)),
        out_shape=jax.ShapeDtypeStruct((n, c), BF16),
        scratch_shapes=[pltpu.VMEM((tm, d), BF16)],
        compiler_params=_cparams(("parallel", "arbitrary")),
        name="proj",
    )(x, g, w)


def _rope_tables(length, head_dim, rot):
    half = rot // 2
    inv = ROPE_THETA ** (-jnp.arange(0, rot, 2, dtype=F32) / rot)
    ang = jnp.arange(length, dtype=F32)[:, None] * inv[None, :]
    cos, sin = jnp.cos(ang), jnp.sin(ang)
    zeros = jnp.zeros((length, head_dim - rot), F32)
    zh = jnp.zeros((length, half), F32)
    c = jnp.concatenate([cos, cos, zeros + 1.0], axis=1)
    s1 = jnp.concatenate([-sin, zh, zeros], axis=1)
    s2 = jnp.concatenate([zh, sin, zeros], axis=1)
    reps = LANES // head_dim
    return tuple(jnp.tile(t, (1, reps)) for t in (c, s1, s2))


def _rope_lanes(x, c, s1, s2, half):
    return x * c + pltpu.roll(x, LANES - half, 1) * s1 + pltpu.roll(x, half, 1) * s2


def _store_t(dst_ref, rows, val):
    vt = val.T.astype(dst_ref.dtype)
    for c in range(vt.shape[1] // DSA_TQ):
        dst_ref[c, rows, :] = vt[:, c * DSA_TQ:(c + 1) * DSA_TQ]


def _rope_body(aq_ref, iq_ref, ak_ref, av_ref, sm_ref, ca_ref, sa1_ref, sa2_ref, ci_ref, si1_ref, si2_ref,
               aqt_ref, iqt_ref, smt_ref, ako_ref, smo_ref, avt_ref):
    ca, sa1, sa2 = ca_ref[...], sa1_ref[...], sa2_ref[...]
    ci, si1, si2 = ci_ref[...], si1_ref[...], si2_ref[...]
    ha, hi = ATTN_ROT // 2, IDX_ROT // 2
    tk = DSA_TK
    for j in range(ATTN_Q // LANES):
        sl = slice(j * LANES, (j + 1) * LANES)
        r = _rope_lanes(aq_ref[:, sl].astype(F32), ca, sa1, sa2, ha)
        _store_t(aqt_ref, sl, r * (ATTN_HEAD_DIM ** -0.5 * LOG2E))
    for j in range(IDX_Q // LANES):
        sl = slice(j * LANES, (j + 1) * LANES)
        r = _rope_lanes(iq_ref[:, sl].astype(F32), ci, si1, si2, hi)
        _store_t(iqt_ref, sl, r * (IDX_HEAD_DIM ** -0.5))
    for j in range(ATTN_KV // LANES):
        sl = slice(j * LANES, (j + 1) * LANES)
        ako_ref[:, sl] = _rope_lanes(ak_ref[:, sl].astype(F32), ca, sa1, sa2, ha).astype(BF16)
        vt = av_ref[:, sl].astype(F32).T
        for c in range(vt.shape[1] // tk):
            avt_ref[c, j * VT_ROWS:j * VT_ROWS + LANES, :] = vt[:, c * tk:(c + 1) * tk].astype(BF16)
            avt_ref[c, j * VT_ROWS + LANES:(j + 1) * VT_ROWS, :] = jnp.ones((VT_ROWS - LANES, tk), BF16)
    sm = sm_ref[...].astype(F32)
    lane = lax.broadcasted_iota(jnp.int32, sm.shape, 1)
    sm = jnp.where(lane < IDX_HEAD_DIM, _rope_lanes(sm, ci, si1, si2, hi), sm)
    smo_ref[...] = sm.astype(BF16)
    _store_t(smt_ref, slice(0, LANES), sm)


def _rope(proj, seq, tabs_a, tabs_i, *, tr=512):
    n = proj.shape[0]
    nl = seq // tr
    tk = DSA_TK
    tab = pl.BlockSpec((tr, LANES), lambda i: (i % nl, 0))
    return pl.pallas_call(
        _rope_body,
        grid=(n // tr,),
        in_specs=[
            pl.BlockSpec((tr, ATTN_Q), lambda i: (i, COL_AQ // ATTN_Q)),
            pl.BlockSpec((tr, IDX_Q), lambda i: (i, COL_IQ // IDX_Q)),
            pl.BlockSpec((tr, ATTN_KV), lambda i: (i, COL_AK // ATTN_KV)),
            pl.BlockSpec((tr, ATTN_KV), lambda i: (i, COL_AV // ATTN_KV)),
            pl.BlockSpec((tr, LANES), lambda i: (i, COL_SM // LANES)),
            tab, tab, tab, tab, tab, tab,
        ],
        out_specs=[
            pl.BlockSpec((tr // DSA_TQ, ATTN_Q, DSA_TQ), lambda i: (i, 0, 0)),
            pl.BlockSpec((tr // DSA_TQ, IDX_Q, DSA_TQ), lambda i: (i, 0, 0)),
            pl.BlockSpec((tr // DSA_TQ, LANES, DSA_TQ), lambda i: (i, 0, 0)),
            pl.BlockSpec((tr, ATTN_KV), lambda i: (i, 0)),
            pl.BlockSpec((tr, LANES), lambda i: (i, 0)),
            pl.BlockSpec((tr // tk, ATTN_KV_HEADS * VT_ROWS, tk), lambda i: (i, 0, 0)),
        ],
        out_shape=[
            jax.ShapeDtypeStruct((n // DSA_TQ, ATTN_Q, DSA_TQ), BF16),
            jax.ShapeDtypeStruct((n // DSA_TQ, IDX_Q, DSA_TQ), BF16),
            jax.ShapeDtypeStruct((n // DSA_TQ, LANES, DSA_TQ), BF16),
            jax.ShapeDtypeStruct((n, ATTN_KV), BF16),
            jax.ShapeDtypeStruct((n, LANES), BF16),
            jax.ShapeDtypeStruct((n // tk, ATTN_KV_HEADS * VT_ROWS, tk), BF16),
        ],
        compiler_params=_cparams(("parallel",)),
        name="rope",
    )(proj, proj, proj, proj, proj, *tabs_a, *tabs_i)


def _sortable(x):
    i = pltpu.bitcast(x + 0.0, jnp.int32)
    return jnp.where(i < 0, i ^ jnp.int32(0x7FFFFFFF), i)


def _tree_sum(xs):
    while len(xs) > 1:
        xs = [xs[i] + xs[i + 1] for i in range(0, len(xs) - 1, 2)] + ([xs[-1]] if len(xs) % 2 else [])
    return xs[0]


def _colsum(x):
    r, c = x.shape
    return jnp.sum(jnp.sum(x.reshape(r // 8, 8, c), axis=0), axis=0, keepdims=True)


def _dsa_body(iqt_ref, smt_ref, aqt_ref, smk_ref, ak_ref, avt_ref, o_ref,
              key_ref, hi_ref, lo_ref, m_ref, al_ref, acc_ref, s0_ref, s1_ref, p0_ref, p1_ref, *, topk, seq):
    tq, tk = DSA_TQ, DSA_TK
    qi = pl.program_id(1)
    t0 = qi * tq
    nkb = (t0 + tq + tk - 1) // tk
    iwt = smt_ref[SM_IW:SM_IW + IDX_HEADS, :].astype(F32) * (IDX_HEADS ** -0.5)
    qpos = lax.broadcasted_iota(jnp.int32, (tk, tq), 1) + t0
    kiota = lax.broadcasted_iota(jnp.int32, (tk, tq), 0)

    def idx_block(kb, carry):
        k0 = pl.multiple_of(kb * tk, tk)
        ik = smk_ref[pl.ds(k0, tk), :][:, SM_IK:SM_IK + IDX_HEAD_DIM]
        score = jnp.zeros((tk, tq), F32)
        for h in range(0, IDX_HEADS, 2):
            qq = jnp.concatenate([iqt_ref[(h + u) * IDX_HEAD_DIM:(h + u + 1) * IDX_HEAD_DIM, :]
                                  for u in range(2)], axis=1)
            rel = jnp.maximum(_dot(ik, qq), 0.0)
            score = score + rel[:, :tq] * iwt[h:h + 1, :] + rel[:, tq:] * iwt[h + 1:h + 2, :]
        key = jnp.where(kiota + k0 <= qpos, _sortable(score), jnp.int32(INT_MIN))
        key_ref[kb] = key
        hi_ref[kb] = (key >> 16).astype(jnp.int16)
        lo_ref[kb] = ((key & 0xFFFF) + HALF_MIN).astype(jnp.int16)
        return carry

    lax.fori_loop(0, nkb, idx_block, 0)

    @pl.when(nkb % 2 == 1)
    def _():
        key_ref[nkb] = jnp.full((tk, tq), INT_MIN, jnp.int32)
        hi_ref[nkb] = jnp.full((tk, tq), HALF_MIN, jnp.int16)
        lo_ref[nkb] = jnp.full((tk, tq), HALF_MIN, jnp.int16)

    npair = (nkb + 1) // 2

    def count(pred):
        def pair(i, c):
            for u in range(2):
                kb = 2 * i + u
                hit = jnp.where(pred(key_ref[kb], kb), 1.0, 0.0)
                c = c + jnp.sum(hit.reshape(tk // 8, 8, tq), axis=0)
            return c
        c = lax.fori_loop(0, npair, pair, jnp.zeros((8, tq), F32))
        return jnp.sum(c, axis=0, keepdims=True)

    def pack16(v):
        return jnp.broadcast_to(v, (PACK16, tq)).astype(jnp.int16)

    def count16(ref, pred):
        one, zero = jnp.int16(1), jnp.int16(0)

        def pair(i, c):
            for u in range(2):
                blk = ref[2 * i + u]
                c = c + _tree_sum([jnp.where(pred(blk[j:j + PACK16]), one, zero)
                                   for j in range(0, tk, PACK16)])
            return c
        c = lax.fori_loop(0, npair, pair, jnp.zeros((PACK16, tq), jnp.int16))
        return jnp.sum(c.astype(F32), axis=0, keepdims=True)

    def bisect16(ref, want):
        def step(it, t):
            cand = t + lax.shift_left(jnp.int32(1), jnp.int32(15) - it)
            c16 = pack16(cand)
            return jnp.where(count16(ref, lambda b: b >= c16) >= want, cand, t)
        return lax.fori_loop(0, 16, step, jnp.full((1, tq), HALF_MIN, jnp.int32))

    kf = float(topk)
    t_hi = bisect16(hi_ref, kf)
    t_hi16 = pack16(t_hi)
    n_hi_gt = count16(hi_ref, lambda b: b > t_hi16)

    def mask_low(i, carry):
        for u in range(2):
            kb = 2 * i + u
            for j in range(0, tk, PACK16):
                rows = slice(j, j + PACK16)
                lo_ref[kb, rows, :] = jnp.where(hi_ref[kb, rows, :] == t_hi16, lo_ref[kb, rows, :],
                                                jnp.int16(HALF_MIN))
        return carry

    lax.fori_loop(0, npair, mask_low, 0)
    t_lo = bisect16(lo_ref, kf - n_hi_gt)
    t_lo16 = pack16(t_lo)
    thr = t_hi * 65536 + (t_lo - HALF_MIN)

    n_gt = n_hi_gt + count16(lo_ref, lambda b: b > t_lo16)
    n_ge = n_hi_gt + count16(lo_ref, lambda b: b >= t_lo16)
    need = kf - n_gt
    tied = (n_ge > kf) & (thr > jnp.int32(INT_MIN))

    def tie_break():
        def step(it, x):
            cand = x + lax.shift_left(jnp.int32(1), jnp.int32(seq.bit_length() - 1) - it)
            below = count(lambda k, kb: (k == thr) & (kiota + kb * tk < cand))
            return jnp.where(below < need, cand, x)
        return lax.fori_loop(0, seq.bit_length(), step, jnp.zeros((1, tq), jnp.int32))

    jtie = lax.cond(jnp.max(jnp.where(tied, 1.0, 0.0)) > 0.0, tie_break,
                    lambda: jnp.zeros((1, tq), jnp.int32))
    jmax = jnp.where(tied, jtie, jnp.where(thr > jnp.int32(INT_MIN), jnp.int32(seq), jnp.int32(-1)))

    last = nkb + nkb % 2 - 1

    def scores(kb, s_ref):
        kb = jnp.minimum(kb, last)
        kc = jnp.minimum(kb, seq // tk - 1)
        k0 = pl.multiple_of(kc * tk, tk)
        key = key_ref[kb]
        sel = (key > thr) | ((key == thr) & (kiota + kb * tk <= jmax))
        bias = jnp.where(sel, 0.0, NEG_INF)
        bias = jnp.concatenate([bias] * ATTN_REP, axis=1)
        for g in range(ATTN_KV_HEADS):
            kg = ak_ref[pl.ds(k0, tk), g * ATTN_HEAD_DIM:(g + 1) * ATTN_HEAD_DIM]
            qt = jnp.concatenate(
                [aqt_ref[(g * ATTN_REP + r) * ATTN_HEAD_DIM:(g * ATTN_REP + r + 1) * ATTN_HEAD_DIM, :]
                 for r in range(ATTN_REP)], axis=1)
            s_ref[g] = _dot(kg, qt) + bias

    def softmax(s_ref, p_ref):
        for g in range(ATTN_KV_HEADS):
            for r in range(ATTN_REP):
                cols = slice(r * tq, (r + 1) * tq)
                s = s_ref[g, :, cols]
                m_old = m_ref[g, :, cols]
                m_new = jnp.maximum(m_old, jnp.max(s, axis=0, keepdims=True))
                p_ref[g, :, cols] = jnp.exp2(s - m_new).astype(BF16)
                al_ref[g, :, cols] = jnp.exp2(m_old - m_new)
                m_ref[g, :, cols] = m_new

    def values(kb, p_ref):
        kc = jnp.clip(kb, 0, seq // tk - 1)
        for g in range(ATTN_KV_HEADS):
            vt = avt_ref[kc, g * VT_ROWS:(g + 1) * VT_ROWS, :]
            acc_ref[g] = al_ref[g] * acc_ref[g] + _dot(vt, p_ref[g])

    m_ref[...] = jnp.full_like(m_ref, NEG_INF)
    acc_ref[...] = jnp.zeros_like(acc_ref)
    al_ref[...] = jnp.ones_like(al_ref)
    p1_ref[...] = jnp.zeros_like(p1_ref)
    scores(0, s0_ref)

    def att_pair(i, carry):
        j = 2 * i
        values(j - 1, p1_ref)
        softmax(s0_ref, p0_ref)
        scores(j + 1, s1_ref)
        values(j, p0_ref)
        softmax(s1_ref, p1_ref)
        scores(j + 2, s0_ref)
        return carry

    lax.fori_loop(0, npair, att_pair, 0)
    values(last, p1_ref)
    for g in range(ATTN_KV_HEADS):
        acc = acc_ref[g]
        ot = acc[:ATTN_HEAD_DIM] / acc[ATTN_HEAD_DIM:ATTN_HEAD_DIM + 1]
        for r in range(ATTN_REP):
            hh = g * ATTN_REP + r
            o_ref[:, hh * ATTN_HEAD_DIM:(hh + 1) * ATTN_HEAD_DIM] = ot[:, r * tq:(r + 1) * tq].T.astype(BF16)


def _dsa(aq_t, iq_t, sm_t, ak_r, sm_r, av_t, batch, seq):
    n = ak_r.shape[0]
    tq, tk = DSA_TQ, DSA_TK
    nq = seq // tq
    nkb = seq // tk
    topk = min(TOPK_MAX, seq // 4)
    qmap = lambda b, i: (b * nq + i, 0, 0)
    return pl.pallas_call(
        functools.partial(_dsa_body, topk=topk, seq=seq),
        grid=(batch, nq),
        in_specs=[
            pl.BlockSpec((None, IDX_Q, tq), qmap),
            pl.BlockSpec((None, LANES, tq), qmap),
            pl.BlockSpec((None, ATTN_Q, tq), qmap),
            pl.BlockSpec((seq, LANES), lambda b, i: (b, 0)),
            pl.BlockSpec((seq, ATTN_KV), lambda b, i: (b, 0)),
            pl.BlockSpec((nkb, ATTN_KV_HEADS * VT_ROWS, tk), lambda b, i: (b, 0, 0)),
        ],
        out_specs=pl.BlockSpec((tq, ATTN_Q), lambda b, i: (b * nq + i, 0)),
        out_shape=jax.ShapeDtypeStruct((n, ATTN_Q), BF16),
        scratch_shapes=[
            pltpu.VMEM((nkb + nkb % 2, tk, tq), jnp.int32),
            pltpu.VMEM((nkb + nkb % 2, tk, tq), jnp.int16),
            pltpu.VMEM((nkb + nkb % 2, tk, tq), jnp.int16),
            pltpu.VMEM((ATTN_KV_HEADS, 1, ATTN_REP * tq), F32),
            pltpu.VMEM((ATTN_KV_HEADS, 1, ATTN_REP * tq), F32),
            pltpu.VMEM((ATTN_KV_HEADS, VT_ROWS, ATTN_REP * tq), F32),
            pltpu.VMEM((ATTN_KV_HEADS, tk, ATTN_REP * tq), F32),
            pltpu.VMEM((ATTN_KV_HEADS, tk, ATTN_REP * tq), F32),
            pltpu.VMEM((ATTN_KV_HEADS, tk, ATTN_REP * tq), BF16),
            pltpu.VMEM((ATTN_KV_HEADS, tk, ATTN_REP * tq), BF16),
        ],
        compiler_params=_cparams(("parallel", "arbitrary")),
        name="dsa",
    )(iq_t, sm_t, aq_t, sm_r, ak_r, av_t)


GLA_ROWS = 256


def _split3(x):
    hi = x.astype(BF16)
    r1 = x - hi.astype(F32)
    mid = r1.astype(BF16)
    lo = (r1 - mid.astype(F32)).astype(BF16)
    return hi, mid, lo


def _gla_body(q_ref, k_ref, v_ref, r_ref, sm_ref, wa_ref, ba_ref, gn_ref, o_ref, st_ref):
    rows, c = GLA_ROWS, GLA_CHUNK

    @pl.when(pl.program_id(2) == 0)
    def _():
        st_ref[...] = jnp.zeros_like(st_ref)

    ga = sm_ref[:, SM_GA:SM_GA + GLA_GATE_RANK]
    z = _dot(ga, wa_ref[...]) + ba_ref[...]
    g = -(jnp.maximum(-z, 0.0) + jnp.log1p(jnp.exp(-jnp.abs(z)))) * (1.0 / GLA_GATE_NORM)
    ri = lax.broadcasted_iota(jnp.int32, (rows, rows), 0)
    ci = lax.broadcasted_iota(jnp.int32, (rows, rows), 1)
    tri = jnp.where((ri // c == ci // c) & (ci <= ri), 1.0, 0.0).astype(BF16)
    hi, mid, lo = _split3(g)
    b = _dot(tri, hi) + _dot(tri, mid) + _dot(tri, lo)
    low = lax.broadcasted_iota(jnp.int32, (c, c), 1) <= lax.broadcasted_iota(jnp.int32, (c, c), 0)
    gn = gn_ref[...]
    for ch in range(rows // c):
        sl = slice(ch * c, (ch + 1) * c)
        bc = b[sl]
        bl = bc[c - 1:c, :]
        q = q_ref[sl, :].astype(F32) * (GLA_DK ** -0.5)
        k = k_ref[sl, :].astype(F32)
        v = v_ref[sl, :]
        qe = (q * jnp.exp(bc - bl)).astype(BF16)
        kd = (k * jnp.exp(bl - bc)).astype(BF16)
        qb = (q * jnp.exp(bc)).astype(BF16)
        a = jnp.where(low, _dot_nt(qe, kd), 0.0).astype(BF16)
        st = st_ref[...]
        o = _dot(a, v) + _dot_nt(qb, st.astype(BF16))
        vt = v.astype(F32).T.astype(BF16)
        st_ref[...] = st * jnp.exp(bl) + _dot(vt, kd)
        o = _rms(o, gn)
        gr = r_ref[sl, :].astype(F32)
        o_ref[sl, :] = (o * (gr * jax.nn.sigmoid(gr))).astype(BF16)


def _gla(proj, wa2, ba, gn, batch, seq):
    n = proj.shape[0]
    rows = GLA_ROWS
    ng = seq // rows
    rmap = lambda off, w: (lambda b, h, i: (b * ng + i, off // w + h))
    return pl.pallas_call(
        _gla_body,
        grid=(batch, GLA_HEADS, ng),
        in_specs=[
            pl.BlockSpec((rows, GLA_DK), rmap(COL_GQ, GLA_DK)),
            pl.BlockSpec((rows, GLA_DK), rmap(COL_GK, GLA_DK)),
            pl.BlockSpec((rows, GLA_DV), rmap(COL_GV, GLA_DV)),
            pl.BlockSpec((rows, GLA_DV), rmap(COL_GR, GLA_DV)),
            pl.BlockSpec((rows, LANES), lambda b, h, i: (b * ng + i, COL_SM // LANES)),
            pl.BlockSpec((GLA_GATE_RANK, GLA_DK), lambda b, h, i: (0, h)),
            pl.BlockSpec((1, GLA_DK), lambda b, h, i: (0, h)),
            pl.BlockSpec((1, GLA_DV), lambda b, h, i: (0, 0)),
        ],
        out_specs=pl.BlockSpec((rows, GLA_DV), lambda b, h, i: (b * ng + i, h)),
        out_shape=jax.ShapeDtypeStruct((n, GLA_V), BF16),
        scratch_shapes=[pltpu.VMEM((GLA_DV, GLA_DK), F32)],
        compiler_params=_cparams(("parallel", "parallel", "arbitrary")),
        name="gla",
    )(proj, proj, proj, proj, proj, wa2, ba, gn)


def _merge_body(x_ref, oa_ref, og_ref, ma_ref, mg_ref, wua_ref, wug_ref, wo_ref, o_ref):
    ya = jax.nn.sigmoid(ma_ref[...].astype(F32)) * _dot(oa_ref[...], wua_ref[...])
    yg = jax.nn.sigmoid(mg_ref[...].astype(F32)) * _dot(og_ref[...], wug_ref[...])
    y = (ya + yg).astype(BF16)
    o_ref[...] = x_ref[...] + _dot(y, wo_ref[...])


def _merge(x, o_attn, o_gla, proj, wua, wug, wo, *, tm=256):
    n, d = x.shape
    const = lambda i: (0, 0)
    return pl.pallas_call(
        _merge_body,
        grid=(n // tm,),
        in_specs=[
            pl.BlockSpec((tm, d), lambda i: (i, 0)),
            pl.BlockSpec((tm, ATTN_Q), lambda i: (i, 0)),
            pl.BlockSpec((tm, GLA_V), lambda i: (i, 0)),
            pl.BlockSpec((tm, d), lambda i: (i, COL_MA // D_MODEL)),
            pl.BlockSpec((tm, d), lambda i: (i, COL_MG // D_MODEL)),
            pl.BlockSpec((ATTN_Q, d), const),
            pl.BlockSpec((GLA_V, d), const),
            pl.BlockSpec((d, d), const),
        ],
        out_specs=pl.BlockSpec((tm, d), lambda i: (i, 0)),
        out_shape=jax.ShapeDtypeStruct((n, d), F32),
        compiler_params=_cparams(("parallel",)),
        name="merge",
    )(x, o_attn, o_gla, proj, proj, wua, wug, wo)


def _reorder_w_in(w):
    (aq, ak, av, iq, ik, iw, gq, gk, gv, ga, gr, m_a, m_g) = jnp.split(w, IN_OFFSETS, axis=-1)
    d = w.shape[0]
    small = jnp.concatenate(
        [ik, iw, ga, jnp.zeros((d, LANES - IDX_HEAD_DIM - IDX_HEADS - GLA_GATE_RANK), w.dtype)], axis=-1)
    cols = jnp.concatenate([aq, iq, gv, gr, m_a, m_g, gq, gk, ak, av, small], axis=-1)
    cols = jnp.pad(cols, ((0, 0), (0, PROJ_COLS - cols.shape[1])))
    return cols


def kernel(x, ffn1_norm, ffn1_w_gate, ffn1_w_up, ffn1_w_down, mix_norm, w_in, gla_w_a2, gla_b_a, gla_norm, w_up_attn, w_up_gla, w_out, ffn2_norm, ffn2_w_gate, ffn2_w_up, ffn2_w_down, final_norm):
    batch, seq, d = x.shape
    depth = w_in.shape[0]
    tabs_a = _rope_tables(seq, ATTN_HEAD_DIM, ATTN_ROT)
    tabs_i = _rope_tables(seq, IDX_HEAD_DIM, IDX_ROT)
    fg = final_norm.reshape(1, d)
    h = x.reshape(batch * seq, d)
    for l in range(depth):
        h = _ffn(h, ffn1_norm[l].reshape(1, d), ffn1_w_gate[l].astype(BF16), ffn1_w_up[l].astype(BF16),
                 ffn1_w_down[l].astype(BF16), fg, final_norm=False)
        proj = _proj(h, mix_norm[l].reshape(1, d), _reorder_w_in(w_in[l].astype(BF16)))
        aq_t, iq_t, sm_t, ak_r, sm_r, av_t = _rope(proj, seq, tabs_a, tabs_i)
        o_attn = _dsa(aq_t, iq_t, sm_t, ak_r, sm_r, av_t, batch, seq)
        o_gla = _gla(proj, gla_w_a2[l].astype(BF16), gla_b_a[l].reshape(1, GLA_K),
                     gla_norm[l].reshape(1, GLA_DV), batch, seq)
        h = _merge(h, o_attn, o_gla, proj, w_up_attn[l].astype(BF16), w_up_gla[l].astype(BF16),
                   w_out[l].astype(BF16))
        h = _ffn(h, ffn2_norm[l].reshape(1, d), ffn2_w_gate[l].astype(BF16), ffn2_w_up[l].astype(BF16),
                 ffn2_w_down[l].astype(BF16), fg, final_norm=(l == depth - 1))
    return h.reshape(batch, seq, d)
```

```python
import functools

import jax
import jax.numpy as jnp
import numpy as np
from jax import lax
from jax.experimental import pallas as pl
from jax.experimental.pallas import tpu as pltpu

F32 = jnp.float32
BF16 = jnp.bfloat16

D_MODEL = 2048
D_FF = 5632
FFN_RES = 0.5
ATTN_HEADS = 8
ATTN_KV_HEADS = 2
ATTN_HEAD_DIM = 128
ATTN_REP = ATTN_HEADS // ATTN_KV_HEADS
IDX_HEADS = 16
IDX_HEAD_DIM = 64
TOPK_MAX = 256
GLA_HEADS = 4
GLA_DK = 128
GLA_DV = 256
GLA_GATE_RANK = 16
GLA_GATE_NORM = 16.0
GLA_CHUNK = 64
ROPE_THETA = 500000.0
ATTN_ROT = ATTN_HEAD_DIM // 4
IDX_ROT = IDX_HEAD_DIM // 4
EPS = 1e-6
NEG_INF = -1e30
INT_MIN = -(2 ** 31)
HALF_MIN = -(2 ** 15)
PACK16 = 16

ATTN_Q = ATTN_HEADS * ATTN_HEAD_DIM
ATTN_KV = ATTN_KV_HEADS * ATTN_HEAD_DIM
IDX_Q = IDX_HEADS * IDX_HEAD_DIM
GLA_K = GLA_HEADS * GLA_DK
GLA_V = GLA_HEADS * GLA_DV
IN_SPLITS = (ATTN_Q, ATTN_KV, ATTN_KV, IDX_Q, IDX_HEAD_DIM, IDX_HEADS,
             GLA_K, GLA_K, GLA_V, GLA_GATE_RANK, GLA_V, D_MODEL, D_MODEL)
IN_OFFSETS = tuple(int(v) for v in np.cumsum(IN_SPLITS)[:-1])
IN_NAMES = ("aq", "ak", "av", "iq", "ik", "iw", "gq", "gk", "gv", "ga", "gr", "m_a", "m_g")

LANES = 128
COL_AQ = 0
COL_IQ = COL_AQ + ATTN_Q
COL_GV = COL_IQ + IDX_Q
COL_GR = COL_GV + GLA_V
COL_MA = COL_GR + GLA_V
COL_MG = COL_MA + D_MODEL
COL_GQ = COL_MG + D_MODEL
COL_GK = COL_GQ + GLA_K
COL_AK = COL_GK + GLA_K
COL_AV = COL_AK + ATTN_KV
COL_SM = COL_AV + ATTN_KV
SM_IK = 0
SM_IW = SM_IK + IDX_HEAD_DIM
SM_GA = SM_IW + IDX_HEADS
PROJ_TN = 1024
PROJ_COLS = -(-(COL_SM + LANES) // PROJ_TN) * PROJ_TN

DSA_TQ = 128
DSA_TK = 256
VT_ROWS = ATTN_HEAD_DIM + PACK16
LOG2E = 1.4426950408889634

FFN_SLAB = 512

VMEM_LIMIT = 62 * 1024 * 1024


def _cparams(sem):
    return pltpu.CompilerParams(dimension_semantics=sem, vmem_limit_bytes=VMEM_LIMIT)


def _dot(a, b):
    return jnp.dot(a, b, preferred_element_type=F32)


def _dot_nt(a, b):
    return lax.dot_general(a, b, (((1,), (1,)), ((), ())), preferred_element_type=F32)


def _rms(x, g):
    return x * lax.rsqrt(jnp.mean(x * x, axis=-1, keepdims=True) + EPS) * g


def _ffn_body(x_ref, g_ref, wg_ref, wu_ref, wd_ref, fg_ref, o_ref, h_ref, *, final_norm):
    f = pl.program_id(1)

    slabs = [slice(r, r + FFN_SLAB) for r in range(0, x_ref.shape[0], FFN_SLAB)]

    @pl.when(f == 0)
    def _():
        for rows in slabs:
            h_ref[rows, :] = _rms(x_ref[rows, :], g_ref[...]).astype(BF16)
        o_ref[...] = jnp.zeros_like(o_ref)

    for rows in slabs:
        h = h_ref[rows, :]
        gate = _dot(h, wg_ref[...])
        up = _dot(h, wu_ref[...])
        act = (gate * jax.nn.sigmoid(gate) * up).astype(BF16)
        o_ref[rows, :] += _dot(act, wd_ref[...])

    @pl.when(f == pl.num_programs(1) - 1)
    def _():
        for rows in slabs:
            y = x_ref[rows, :] + FFN_RES * o_ref[rows, :]
            if final_norm:
                y = _rms(y, fg_ref[...])
            o_ref[rows, :] = y


def _ffn(x, g, wg, wu, wd, fg, layer, *, final_norm, tm=1024, tf=512):
    n, d = x.shape
    f = wg.shape[2]
    return pl.pallas_call(
        functools.partial(_ffn_body, final_norm=final_norm),
        grid=(n // tm, f // tf),
        in_specs=[
            pl.BlockSpec((tm, d), lambda i, j: (i, 0)),
            pl.BlockSpec((None, 1, d), lambda i, j: (layer, 0, 0)),
            pl.BlockSpec((None, d, tf), lambda i, j: (layer, 0, j)),
            pl.BlockSpec((None, d, tf), lambda i, j: (layer, 0, j)),
            pl.BlockSpec((None, tf, d), lambda i, j: (layer, j, 0)),
            pl.BlockSpec((1, d), lambda i, j: (0, 0)),
        ],
        out_specs=pl.BlockSpec((tm, d), lambda i, j: (i, 0)),
        out_shape=jax.ShapeDtypeStruct((n, d), F32),
        scratch_shapes=[pltpu.VMEM((tm, d), BF16)],
        compiler_params=_cparams(("parallel", "arbitrary")),
        name="ffn",
    )(x, g, wg, wu, wd, fg)


def _cast_body(w_ref, o_ref):
    o_ref[...] = w_ref[...].astype(o_ref.dtype)


def _to_bf16(w, tr):
    depth, r, c = w.shape
    spec = pl.BlockSpec((None, tr, c), lambda l, i: (l, i, 0))
    return pl.pallas_call(
        _cast_body,
        grid=(depth, r // tr),
        in_specs=[spec],
        out_specs=spec,
        out_shape=jax.ShapeDtypeStruct(w.shape, BF16),
        compiler_params=_cparams(("parallel", "parallel")),
        name="cast",
    )(w)


def _w_in_body(w_ref, o_ref):
    w = w_ref[...]
    seg = {name: w[:, off:off + width] for name, off, width in
           zip(IN_NAMES, (0,) + IN_OFFSETS, IN_SPLITS)}
    tr = w.shape[0]
    small_pad = jnp.zeros((tr, LANES - IDX_HEAD_DIM - IDX_HEADS - GLA_GATE_RANK), w.dtype)
    tail = jnp.zeros((tr, PROJ_COLS - COL_SM - LANES), w.dtype)
    order = ("aq", "iq", "gv", "gr", "m_a", "m_g", "gq", "gk", "ak", "av", "ik", "iw", "ga")
    o_ref[...] = jnp.concatenate([seg[k] for k in order] + [small_pad, tail], axis=1).astype(o_ref.dtype)


def _prep_w_in(w, tr=256):
    depth, d, c = w.shape
    return pl.pallas_call(
        _w_in_body,
        grid=(depth, d // tr),
        in_specs=[pl.BlockSpec((None, tr, c), lambda l, i: (l, i, 0))],
        out_specs=pl.BlockSpec((None, tr, PROJ_COLS), lambda l, i: (l, i, 0)),
        out_shape=jax.ShapeDtypeStruct((depth, d, PROJ_COLS), BF16),
        compiler_params=_cparams(("parallel", "parallel")),
        name="w_in_prep",
    )(w)


def _proj_body(x_ref, g_ref, w_ref, o_ref, h_ref):
    @pl.when(pl.program_id(1) == 0)
    def _():
        h_ref[...] = _rms(x_ref[...], g_ref[...]).astype(BF16)

    o_ref[...] = _dot(h_ref[...], w_ref[...]).astype(o_ref.dtype)


def _proj(x, g, w, layer, *, tm=1024, tn=PROJ_TN):
    n, d = x.shape
    c = w.shape[2]
    return pl.pallas_call(
        _proj_body,
        grid=(n // tm, c // tn),
        in_specs=[
            pl.BlockSpec((tm, d), lambda i, j: (i, 0)),
            pl.BlockSpec((None, 1, d), lambda i, j: (layer, 0, 0)),
            pl.BlockSpec((None, d, tn), lambda i, j: (layer, 0, j)),
        ],
        out_specs=pl.BlockSpec((tm, tn), lambda i, j: (i, j)),
        out_shape=jax.ShapeDtypeStruct((n, c), BF16),
        scratch_shapes=[pltpu.VMEM((tm, d), BF16)],
        compiler_params=_cparams(("parallel", "arbitrary")),
        name="proj",
    )(x, g, w)


def _rope_tables(length, head_dim, rot):
    half = rot // 2
    inv = ROPE_THETA ** (-jnp.arange(0, rot, 2, dtype=F32) / rot)
    ang = jnp.arange(length, dtype=F32)[:, None] * inv[None, :]
    cos, sin = jnp.cos(ang), jnp.sin(ang)
    zeros = jnp.zeros((length, head_dim - rot), F32)
    zh = jnp.zeros((length, half), F32)
    c = jnp.concatenate([cos, cos, zeros + 1.0], axis=1)
    s1 = jnp.concatenate([-sin, zh, zeros], axis=1)
    s2 = jnp.concatenate([zh, sin, zeros], axis=1)
    reps = LANES // head_dim
    return tuple(jnp.tile(t, (1, reps)) for t in (c, s1, s2))


def _rope_lanes(x, c, s1, s2, half):
    return x * c + pltpu.roll(x, LANES - half, 1) * s1 + pltpu.roll(x, half, 1) * s2


def _store_t(dst_ref, rows, val):
    vt = val.T.astype(dst_ref.dtype)
    for c in range(vt.shape[1] // DSA_TQ):
        dst_ref[c, rows, :] = vt[:, c * DSA_TQ:(c + 1) * DSA_TQ]


def _rope_body(aq_ref, iq_ref, ak_ref, av_ref, sm_ref, ca_ref, sa1_ref, sa2_ref, ci_ref, si1_ref, si2_ref,
               aqt_ref, iqt_ref, smt_ref, ako_ref, smo_ref, avt_ref):
    ca, sa1, sa2 = ca_ref[...], sa1_ref[...], sa2_ref[...]
    ci, si1, si2 = ci_ref[...], si1_ref[...], si2_ref[...]
    ha, hi = ATTN_ROT // 2, IDX_ROT // 2
    tk = DSA_TK
    for j in range(ATTN_Q // LANES):
        sl = slice(j * LANES, (j + 1) * LANES)
        r = _rope_lanes(aq_ref[:, sl].astype(F32), ca, sa1, sa2, ha)
        _store_t(aqt_ref, sl, r * (ATTN_HEAD_DIM ** -0.5 * LOG2E))
    for j in range(IDX_Q // LANES):
        sl = slice(j * LANES, (j + 1) * LANES)
        r = _rope_lanes(iq_ref[:, sl].astype(F32), ci, si1, si2, hi)
        _store_t(iqt_ref, sl, r * (IDX_HEAD_DIM ** -0.5))
    for j in range(ATTN_KV // LANES):
        sl = slice(j * LANES, (j + 1) * LANES)
        ako_ref[:, sl] = _rope_lanes(ak_ref[:, sl].astype(F32), ca, sa1, sa2, ha).astype(BF16)
        vt = av_ref[:, sl].astype(F32).T
        for c in range(vt.shape[1] // tk):
            avt_ref[c, j * VT_ROWS:j * VT_ROWS + LANES, :] = vt[:, c * tk:(c + 1) * tk].astype(BF16)
            avt_ref[c, j * VT_ROWS + LANES:(j + 1) * VT_ROWS, :] = jnp.ones((VT_ROWS - LANES, tk), BF16)
    sm = sm_ref[...].astype(F32)
    lane = lax.broadcasted_iota(jnp.int32, sm.shape, 1)
    sm = jnp.where(lane < IDX_HEAD_DIM, _rope_lanes(sm, ci, si1, si2, hi), sm)
    smo_ref[...] = sm.astype(BF16)
    _store_t(smt_ref, slice(0, LANES), sm)


def _rope(proj, seq, tabs_a, tabs_i, *, tr=512):
    n = proj.shape[0]
    nl = seq // tr
    tk = DSA_TK
    tab = pl.BlockSpec((tr, LANES), lambda i: (i % nl, 0))
    return pl.pallas_call(
        _rope_body,
        grid=(n // tr,),
        in_specs=[
            pl.BlockSpec((tr, ATTN_Q), lambda i: (i, COL_AQ // ATTN_Q)),
            pl.BlockSpec((tr, IDX_Q), lambda i: (i, COL_IQ // IDX_Q)),
            pl.BlockSpec((tr, ATTN_KV), lambda i: (i, COL_AK // ATTN_KV)),
            pl.BlockSpec((tr, ATTN_KV), lambda i: (i, COL_AV // ATTN_KV)),
            pl.BlockSpec((tr, LANES), lambda i: (i, COL_SM // LANES)),
            tab, tab, tab, tab, tab, tab,
        ],
        out_specs=[
            pl.BlockSpec((tr // DSA_TQ, ATTN_Q, DSA_TQ), lambda i: (i, 0, 0)),
            pl.BlockSpec((tr // DSA_TQ, IDX_Q, DSA_TQ), lambda i: (i, 0, 0)),
            pl.BlockSpec((tr // DSA_TQ, LANES, DSA_TQ), lambda i: (i, 0, 0)),
            pl.BlockSpec((tr, ATTN_KV), lambda i: (i, 0)),
            pl.BlockSpec((tr, LANES), lambda i: (i, 0)),
            pl.BlockSpec((tr // tk, ATTN_KV_HEADS * VT_ROWS, tk), lambda i: (i, 0, 0)),
        ],
        out_shape=[
            jax.ShapeDtypeStruct((n // DSA_TQ, ATTN_Q, DSA_TQ), BF16),
            jax.ShapeDtypeStruct((n // DSA_TQ, IDX_Q, DSA_TQ), BF16),
            jax.ShapeDtypeStruct((n // DSA_TQ, LANES, DSA_TQ), BF16),
            jax.ShapeDtypeStruct((n, ATTN_KV), BF16),
            jax.ShapeDtypeStruct((n, LANES), BF16),
            jax.ShapeDtypeStruct((n // tk, ATTN_KV_HEADS * VT_ROWS, tk), BF16),
        ],
        compiler_params=_cparams(("parallel",)),
        name="rope",
    )(proj, proj, proj, proj, proj, *tabs_a, *tabs_i)


def _sortable(x):
    i = pltpu.bitcast(x + 0.0, jnp.int32)
    return jnp.where(i < 0, i ^ jnp.int32(0x7FFFFFFF), i)


def _tree_sum(xs):
    while len(xs) > 1:
        xs = [xs[i] + xs[i + 1] for i in range(0, len(xs) - 1, 2)] + ([xs[-1]] if len(xs) % 2 else [])
    return xs[0]


def _colsum(x):
    r, c = x.shape
    return jnp.sum(jnp.sum(x.reshape(r // 8, 8, c), axis=0), axis=0, keepdims=True)


def _dsa_body(iqt_ref, smt_ref, aqt_ref, smk_ref, ak_ref, avt_ref, o_ref,
              key_ref, hi_ref, lo_ref, m_ref, al_ref, acc_ref, s0_ref, s1_ref, p0_ref, p1_ref, *, topk, seq):
    tq, tk = DSA_TQ, DSA_TK
    qi = pl.program_id(1)
    t0 = qi * tq
    nkb = (t0 + tq + tk - 1) // tk
    iwt = smt_ref[SM_IW:SM_IW + IDX_HEADS, :].astype(F32) * (IDX_HEADS ** -0.5)
    qpos = lax.broadcasted_iota(jnp.int32, (tk, tq), 1) + t0
    kiota = lax.broadcasted_iota(jnp.int32, (tk, tq), 0)

    def idx_block(kb, carry):
        k0 = pl.multiple_of(kb * tk, tk)
        ik = smk_ref[pl.ds(k0, tk), :][:, SM_IK:SM_IK + IDX_HEAD_DIM]
        score = jnp.zeros((tk, tq), F32)
        for h in range(0, IDX_HEADS, 2):
            qq = jnp.concatenate([iqt_ref[(h + u) * IDX_HEAD_DIM:(h + u + 1) * IDX_HEAD_DIM, :]
                                  for u in range(2)], axis=1)
            rel = jnp.maximum(_dot(ik, qq), 0.0)
            score = score + rel[:, :tq] * iwt[h:h + 1, :] + rel[:, tq:] * iwt[h + 1:h + 2, :]
        key = jnp.where(kiota + k0 <= qpos, _sortable(score), jnp.int32(INT_MIN))
        key_ref[kb] = key
        hi_ref[kb] = (key >> 16).astype(jnp.int16)
        lo_ref[kb] = ((key & 0xFFFF) + HALF_MIN).astype(jnp.int16)
        return carry

    lax.fori_loop(0, nkb, idx_block, 0)

    @pl.when(nkb % 2 == 1)
    def _():
        key_ref[nkb] = jnp.full((tk, tq), INT_MIN, jnp.int32)
        hi_ref[nkb] = jnp.full((tk, tq), HALF_MIN, jnp.int16)
        lo_ref[nkb] = jnp.full((tk, tq), HALF_MIN, jnp.int16)

    npair = (nkb + 1) // 2

    def count(pred):
        def pair(i, c):
            for u in range(2):
                kb = 2 * i + u
                hit = jnp.where(pred(key_ref[kb], kb), 1.0, 0.0)
                c = c + jnp.sum(hit.reshape(tk // 8, 8, tq), axis=0)
            return c
        c = lax.fori_loop(0, npair, pair, jnp.zeros((8, tq), F32))
        return jnp.sum(c, axis=0, keepdims=True)

    def pack16(v):
        return jnp.broadcast_to(v, (PACK16, tq)).astype(jnp.int16)

    def count16(ref, pred):
        one, zero = jnp.int16(1), jnp.int16(0)

        def pair(i, c):
            for u in range(2):
                blk = ref[2 * i + u]
                c = c + _tree_sum([jnp.where(pred(blk[j:j + PACK16]), one, zero)
                                   for j in range(0, tk, PACK16)])
            return c
        c = lax.fori_loop(0, npair, pair, jnp.zeros((PACK16, tq), jnp.int16))
        return jnp.sum(c.astype(F32), axis=0, keepdims=True)

    def bisect16(ref, want):
        def step(it, t):
            cand = t + lax.shift_left(jnp.int32(1), jnp.int32(15) - it)
            c16 = pack16(cand)
            return jnp.where(count16(ref, lambda b: b >= c16) >= want, cand, t)
        return lax.fori_loop(0, 16, step, jnp.full((1, tq), HALF_MIN, jnp.int32))

    kf = float(topk)
    t_hi = bisect16(hi_ref, kf)
    t_hi16 = pack16(t_hi)
    n_hi_gt = count16(hi_ref, lambda b: b > t_hi16)

    def mask_low(i, carry):
        for u in range(2):
            kb = 2 * i + u
            for j in range(0, tk, PACK16):
                rows = slice(j, j + PACK16)
                lo_ref[kb, rows, :] = jnp.where(hi_ref[kb, rows, :] == t_hi16, lo_ref[kb, rows, :],
                                                jnp.int16(HALF_MIN))
        return carry

    lax.fori_loop(0, npair, mask_low, 0)
    t_lo = bisect16(lo_ref, kf - n_hi_gt)
    t_lo16 = pack16(t_lo)
    thr = t_hi * 65536 + (t_lo - HALF_MIN)

    n_gt = n_hi_gt + count16(lo_ref, lambda b: b > t_lo16)
    n_ge = n_hi_gt + count16(lo_ref, lambda b: b >= t_lo16)
    need = kf - n_gt
    tied = (n_ge > kf) & (thr > jnp.int32(INT_MIN))

    def tie_break():
        def step(it, x):
            cand = x + lax.shift_left(jnp.int32(1), jnp.int32(seq.bit_length() - 1) - it)
            below = count(lambda k, kb: (k == thr) & (kiota + kb * tk < cand))
            return jnp.where(below < need, cand, x)
        return lax.fori_loop(0, seq.bit_length(), step, jnp.zeros((1, tq), jnp.int32))

    jtie = lax.cond(jnp.max(jnp.where(tied, 1.0, 0.0)) > 0.0, tie_break,
                    lambda: jnp.zeros((1, tq), jnp.int32))
    jmax = jnp.where(tied, jtie, jnp.where(thr > jnp.int32(INT_MIN), jnp.int32(seq), jnp.int32(-1)))

    last = nkb + nkb % 2 - 1

    def scores(kb, s_ref):
        kb = jnp.minimum(kb, last)
        kc = jnp.minimum(kb, seq // tk - 1)
        k0 = pl.multiple_of(kc * tk, tk)
        key = key_ref[kb]
        sel = (key > thr) | ((key == thr) & (kiota + kb * tk <= jmax))
        bias = jnp.where(sel, 0.0, NEG_INF)
        bias = jnp.concatenate([bias] * ATTN_REP, axis=1)
        for g in range(ATTN_KV_HEADS):
            kg = ak_ref[pl.ds(k0, tk), g * ATTN_HEAD_DIM:(g + 1) * ATTN_HEAD_DIM]
            qt = jnp.concatenate(
                [aqt_ref[(g * ATTN_REP + r) * ATTN_HEAD_DIM:(g * ATTN_REP + r + 1) * ATTN_HEAD_DIM, :]
                 for r in range(ATTN_REP)], axis=1)
            s_ref[g] = _dot(kg, qt) + bias

    def softmax(s_ref, p_ref):
        for g in range(ATTN_KV_HEADS):
            for r in range(ATTN_REP):
                cols = slice(r * tq, (r + 1) * tq)
                s = s_ref[g, :, cols]
                m_old = m_ref[g, :, cols]
                m_new = jnp.maximum(m_old, jnp.max(s, axis=0, keepdims=True))
                p_ref[g, :, cols] = jnp.exp2(s - m_new).astype(BF16)
                al_ref[g, :, cols] = jnp.exp2(m_old - m_new)
                m_ref[g, :, cols] = m_new

    def values(kb, p_ref):
        kc = jnp.clip(kb, 0, seq // tk - 1)
        for g in range(ATTN_KV_HEADS):
            vt = avt_ref[kc, g * VT_ROWS:(g + 1) * VT_ROWS, :]
            acc_ref[g] = al_ref[g] * acc_ref[g] + _dot(vt, p_ref[g])

    m_ref[...] = jnp.full_like(m_ref, NEG_INF)
    acc_ref[...] = jnp.zeros_like(acc_ref)
    al_ref[...] = jnp.ones_like(al_ref)
    p1_ref[...] = jnp.zeros_like(p1_ref)
    scores(0, s0_ref)

    def att_pair(i, carry):
        j = 2 * i
        values(j - 1, p1_ref)
        softmax(s0_ref, p0_ref)
        scores(j + 1, s1_ref)
        values(j, p0_ref)
        softmax(s1_ref, p1_ref)
        scores(j + 2, s0_ref)
        return carry

    lax.fori_loop(0, npair, att_pair, 0)
    values(last, p1_ref)
    for g in range(ATTN_KV_HEADS):
        acc = acc_ref[g]
        ot = acc[:ATTN_HEAD_DIM] / acc[ATTN_HEAD_DIM:ATTN_HEAD_DIM + 1]
        for r in range(ATTN_REP):
            hh = g * ATTN_REP + r
            o_ref[:, hh * ATTN_HEAD_DIM:(hh + 1) * ATTN_HEAD_DIM] = ot[:, r * tq:(r + 1) * tq].T.astype(BF16)


def _dsa(aq_t, iq_t, sm_t, ak_r, sm_r, av_t, batch, seq):
    n = ak_r.shape[0]
    tq, tk = DSA_TQ, DSA_TK
    nq = seq // tq
    nkb = seq // tk
    topk = min(TOPK_MAX, seq // 4)
    qmap = lambda b, i: (b * nq + i, 0, 0)
    return pl.pallas_call(
        functools.partial(_dsa_body, topk=topk, seq=seq),
        grid=(batch, nq),
        in_specs=[
            pl.BlockSpec((None, IDX_Q, tq), qmap),
            pl.BlockSpec((None, LANES, tq), qmap),
            pl.BlockSpec((None, ATTN_Q, tq), qmap),
            pl.BlockSpec((seq, LANES), lambda b, i: (b, 0)),
            pl.BlockSpec((seq, ATTN_KV), lambda b, i: (b, 0)),
            pl.BlockSpec((nkb, ATTN_KV_HEADS * VT_ROWS, tk), lambda b, i: (b, 0, 0)),
        ],
        out_specs=pl.BlockSpec((tq, ATTN_Q), lambda b, i: (b * nq + i, 0)),
        out_shape=jax.ShapeDtypeStruct((n, ATTN_Q), BF16),
        scratch_shapes=[
            pltpu.VMEM((nkb + nkb % 2, tk, tq), jnp.int32),
            pltpu.VMEM((nkb + nkb % 2, tk, tq), jnp.int16),
            pltpu.VMEM((nkb + nkb % 2, tk, tq), jnp.int16),
            pltpu.VMEM((ATTN_KV_HEADS, 1, ATTN_REP * tq), F32),
            pltpu.VMEM((ATTN_KV_HEADS, 1, ATTN_REP * tq), F32),
            pltpu.VMEM((ATTN_KV_HEADS, VT_ROWS, ATTN_REP * tq), F32),
            pltpu.VMEM((ATTN_KV_HEADS, tk, ATTN_REP * tq), F32),
            pltpu.VMEM((ATTN_KV_HEADS, tk, ATTN_REP * tq), F32),
            pltpu.VMEM((ATTN_KV_HEADS, tk, ATTN_REP * tq), BF16),
            pltpu.VMEM((ATTN_KV_HEADS, tk, ATTN_REP * tq), BF16),
        ],
        compiler_params=_cparams(("parallel", "arbitrary")),
        name="dsa",
    )(iq_t, sm_t, aq_t, sm_r, ak_r, av_t)


GLA_ROWS = 256


def _split3(x):
    hi = x.astype(BF16)
    r1 = x - hi.astype(F32)
    mid = r1.astype(BF16)
    lo = (r1 - mid.astype(F32)).astype(BF16)
    return hi, mid, lo


def _gla_body(q_ref, k_ref, v_ref, r_ref, sm_ref, wa_ref, ba_ref, gn_ref, o_ref, st_ref):
    rows, c = GLA_ROWS, GLA_CHUNK

    @pl.when(pl.program_id(2) == 0)
    def _():
        st_ref[...] = jnp.zeros_like(st_ref)

    ga = sm_ref[:, SM_GA:SM_GA + GLA_GATE_RANK]
    z = _dot(ga, wa_ref[...]) + ba_ref[...]
    g = -(jnp.maximum(-z, 0.0) + jnp.log1p(jnp.exp(-jnp.abs(z)))) * (1.0 / GLA_GATE_NORM)
    ri = lax.broadcasted_iota(jnp.int32, (rows, rows), 0)
    ci = lax.broadcasted_iota(jnp.int32, (rows, rows), 1)
    tri = jnp.where((ri // c == ci // c) & (ci <= ri), 1.0, 0.0).astype(BF16)
    hi, mid, lo = _split3(g)
    b = _dot(tri, hi) + _dot(tri, mid) + _dot(tri, lo)
    low = lax.broadcasted_iota(jnp.int32, (c, c), 1) <= lax.broadcasted_iota(jnp.int32, (c, c), 0)
    gn = gn_ref[...]
    for ch in range(rows // c):
        sl = slice(ch * c, (ch + 1) * c)
        bc = b[sl]
        bl = bc[c - 1:c, :]
        q = q_ref[sl, :].astype(F32) * (GLA_DK ** -0.5)
        k = k_ref[sl, :].astype(F32)
        v = v_ref[sl, :]
        qe = (q * jnp.exp(bc - bl)).astype(BF16)
        kd = (k * jnp.exp(bl - bc)).astype(BF16)
        qb = (q * jnp.exp(bc)).astype(BF16)
        a = jnp.where(low, _dot_nt(qe, kd), 0.0).astype(BF16)
        st = st_ref[...]
        o = _dot(a, v) + _dot_nt(qb, st.astype(BF16))
        vt = v.astype(F32).T.astype(BF16)
        st_ref[...] = st * jnp.exp(bl) + _dot(vt, kd)
        o = _rms(o, gn)
        gr = r_ref[sl, :].astype(F32)
        o_ref[sl, :] = (o * (gr * jax.nn.sigmoid(gr))).astype(BF16)


def _gla(proj, wa2, ba, gn, batch, seq):
    n = proj.shape[0]
    rows = GLA_ROWS
    ng = seq // rows
    rmap = lambda off, w: (lambda b, h, i: (b * ng + i, off // w + h))
    return pl.pallas_call(
        _gla_body,
        grid=(batch, GLA_HEADS, ng),
        in_specs=[
            pl.BlockSpec((rows, GLA_DK), rmap(COL_GQ, GLA_DK)),
            pl.BlockSpec((rows, GLA_DK), rmap(COL_GK, GLA_DK)),
            pl.BlockSpec((rows, GLA_DV), rmap(COL_GV, GLA_DV)),
            pl.BlockSpec((rows, GLA_DV), rmap(COL_GR, GLA_DV)),
            pl.BlockSpec((rows, LANES), lambda b, h, i: (b * ng + i, COL_SM // LANES)),
            pl.BlockSpec((GLA_GATE_RANK, GLA_DK), lambda b, h, i: (0, h)),
            pl.BlockSpec((1, GLA_DK), lambda b, h, i: (0, h)),
            pl.BlockSpec((1, GLA_DV), lambda b, h, i: (0, 0)),
        ],
        out_specs=pl.BlockSpec((rows, GLA_DV), lambda b, h, i: (b * ng + i, h)),
        out_shape=jax.ShapeDtypeStruct((n, GLA_V), BF16),
        scratch_shapes=[pltpu.VMEM((GLA_DV, GLA_DK), F32)],
        compiler_params=_cparams(("parallel", "parallel", "arbitrary")),
        name="gla",
    )(proj, proj, proj, proj, proj, wa2, ba, gn)


def _merge_body(x_ref, oa_ref, og_ref, ma_ref, mg_ref, wua_ref, wug_ref, wo_ref, o_ref):
    ya = jax.nn.sigmoid(ma_ref[...].astype(F32)) * _dot(oa_ref[...], wua_ref[...])
    yg = jax.nn.sigmoid(mg_ref[...].astype(F32)) * _dot(og_ref[...], wug_ref[...])
    y = (ya + yg).astype(BF16)
    o_ref[...] = x_ref[...] + _dot(y, wo_ref[...])


def _merge(x, o_attn, o_gla, proj, wua, wug, wo, layer, *, tm=256):
    n, d = x.shape
    const = lambda i: (layer, 0, 0)
    return pl.pallas_call(
        _merge_body,
        grid=(n // tm,),
        in_specs=[
            pl.BlockSpec((tm, d), lambda i: (i, 0)),
            pl.BlockSpec((tm, ATTN_Q), lambda i: (i, 0)),
            pl.BlockSpec((tm, GLA_V), lambda i: (i, 0)),
            pl.BlockSpec((tm, d), lambda i: (i, COL_MA // D_MODEL)),
            pl.BlockSpec((tm, d), lambda i: (i, COL_MG // D_MODEL)),
            pl.BlockSpec((None, ATTN_Q, d), const),
            pl.BlockSpec((None, GLA_V, d), const),
            pl.BlockSpec((None, d, d), const),
        ],
        out_specs=pl.BlockSpec((tm, d), lambda i: (i, 0)),
        out_shape=jax.ShapeDtypeStruct((n, d), F32),
        compiler_params=_cparams(("parallel",)),
        name="merge",
    )(x, o_attn, o_gla, proj, proj, wua, wug, wo)


def kernel(x, ffn1_norm, ffn1_w_gate, ffn1_w_up, ffn1_w_down, mix_norm, w_in, gla_w_a2, gla_b_a, gla_norm, w_up_attn, w_up_gla, w_out, ffn2_norm, ffn2_w_gate, ffn2_w_up, ffn2_w_down, final_norm):
    batch, seq, d = x.shape
    depth = w_in.shape[0]
    tabs_a = _rope_tables(seq, ATTN_HEAD_DIM, ATTN_ROT)
    tabs_i = _rope_tables(seq, IDX_HEAD_DIM, IDX_ROT)
    fg = final_norm.reshape(1, d)
    g1, gm, g2 = (a.reshape(depth, 1, d) for a in (ffn1_norm, mix_norm, ffn2_norm))
    ffn1 = (_to_bf16(ffn1_w_gate, 256), _to_bf16(ffn1_w_up, 256), _to_bf16(ffn1_w_down, 512))
    ffn2 = (_to_bf16(ffn2_w_gate, 256), _to_bf16(ffn2_w_up, 256), _to_bf16(ffn2_w_down, 512))
    w_in_b = _prep_w_in(w_in)
    wua, wug, wo = _to_bf16(w_up_attn, 512), _to_bf16(w_up_gla, 512), _to_bf16(w_out, 512)
    h = x.reshape(batch * seq, d)
    for l in range(depth):
        h = _ffn(h, g1, *ffn1, fg, l, final_norm=False)
        proj = _proj(h, gm, w_in_b, l)
        aq_t, iq_t, sm_t, ak_r, sm_r, av_t = _rope(proj, seq, tabs_a, tabs_i)
        o_attn = _dsa(aq_t, iq_t, sm_t, ak_r, sm_r, av_t, batch, seq)
        o_gla = _gla(proj, gla_w_a2[l].astype(BF16), gla_b_a[l].reshape(1, GLA_K),
                     gla_norm[l].reshape(1, GLA_DV), batch, seq)
        h = _merge(h, o_attn, o_gla, proj, wua, wug, wo, l)
        h = _ffn(h, g2, *ffn2, fg, l, final_norm=(l == depth - 1))
    return h.reshape(batch, seq, d)
```

```python
import functools

import jax
import jax.numpy as jnp
import numpy as np
from jax import lax
from jax.experimental import pallas as pl
from jax.experimental.pallas import tpu as pltpu

F32 = jnp.float32
BF16 = jnp.bfloat16

D_MODEL = 2048
D_FF = 5632
FFN_RES = 0.5
ATTN_HEADS = 8
ATTN_KV_HEADS = 2
ATTN_HEAD_DIM = 128
ATTN_REP = ATTN_HEADS // ATTN_KV_HEADS
IDX_HEADS = 16
IDX_HEAD_DIM = 64
TOPK_MAX = 256
GLA_HEADS = 4
GLA_DK = 128
GLA_DV = 256
GLA_GATE_RANK = 16
GLA_GATE_NORM = 16.0
GLA_CHUNK = 64
ROPE_THETA = 500000.0
ATTN_ROT = ATTN_HEAD_DIM // 4
IDX_ROT = IDX_HEAD_DIM // 4
EPS = 1e-6
NEG_INF = -1e30
INT_MIN = -(2 ** 31)
HALF_MIN = -(2 ** 15)
PACK16 = 16

ATTN_Q = ATTN_HEADS * ATTN_HEAD_DIM
ATTN_KV = ATTN_KV_HEADS * ATTN_HEAD_DIM
IDX_Q = IDX_HEADS * IDX_HEAD_DIM
GLA_K = GLA_HEADS * GLA_DK
GLA_V = GLA_HEADS * GLA_DV
IN_SPLITS = (ATTN_Q, ATTN_KV, ATTN_KV, IDX_Q, IDX_HEAD_DIM, IDX_HEADS,
             GLA_K, GLA_K, GLA_V, GLA_GATE_RANK, GLA_V, D_MODEL, D_MODEL)
IN_OFFSETS = tuple(int(v) for v in np.cumsum(IN_SPLITS)[:-1])
IN_NAMES = ("aq", "ak", "av", "iq", "ik", "iw", "gq", "gk", "gv", "ga", "gr", "m_a", "m_g")

LANES = 128
COL_AQ = 0
COL_IQ = COL_AQ + ATTN_Q
COL_GV = COL_IQ + IDX_Q
COL_GR = COL_GV + GLA_V
COL_MA = COL_GR + GLA_V
COL_MG = COL_MA + D_MODEL
COL_GQ = COL_MG + D_MODEL
COL_GK = COL_GQ + GLA_K
COL_AK = COL_GK + GLA_K
COL_AV = COL_AK + ATTN_KV
COL_SM = COL_AV + ATTN_KV
SM_IK = 0
SM_IW = SM_IK + IDX_HEAD_DIM
SM_GA = SM_IW + IDX_HEADS
PROJ_TN = 1024
PROJ_COLS = -(-(COL_SM + LANES) // PROJ_TN) * PROJ_TN

DSA_TQ = 128
DSA_TK = 256
VT_ROWS = ATTN_HEAD_DIM + PACK16
LOG2E = 1.4426950408889634

FFN_SLAB = 512

VMEM_LIMIT = 62 * 1024 * 1024


def _cparams(sem):
    return pltpu.CompilerParams(dimension_semantics=sem, vmem_limit_bytes=VMEM_LIMIT)


def _dot(a, b):
    return jnp.dot(a, b, preferred_element_type=F32)


def _dot_nt(a, b):
    return lax.dot_general(a, b, (((1,), (1,)), ((), ())), preferred_element_type=F32)


def _rms(x, g):
    return x * lax.rsqrt(jnp.mean(x * x, axis=-1, keepdims=True) + EPS) * g


def _ffn_body(x_ref, g_ref, wg_ref, wu_ref, wd_ref, fg_ref, o_ref, h_ref, *, final_norm):
    f = pl.program_id(1)

    slabs = [slice(r, r + FFN_SLAB) for r in range(0, x_ref.shape[0], FFN_SLAB)]

    @pl.when(f == 0)
    def _():
        for rows in slabs:
            h_ref[rows, :] = _rms(x_ref[rows, :], g_ref[...]).astype(BF16)
        o_ref[...] = jnp.zeros_like(o_ref)

    for rows in slabs:
        h = h_ref[rows, :]
        gate = _dot(h, wg_ref[...])
        up = _dot(h, wu_ref[...])
        act = (gate * jax.nn.sigmoid(gate) * up).astype(BF16)
        o_ref[rows, :] += _dot(act, wd_ref[...])

    @pl.when(f == pl.num_programs(1) - 1)
    def _():
        for rows in slabs:
            y = x_ref[rows, :] + FFN_RES * o_ref[rows, :]
            if final_norm:
                y = _rms(y, fg_ref[...])
            o_ref[rows, :] = y


def _ffn(x, g, wg, wu, wd, fg, layer, *, final_norm, tm=1024, tf=512):
    n, d = x.shape
    f = wg.shape[2]
    return pl.pallas_call(
        functools.partial(_ffn_body, final_norm=final_norm),
        grid=(n // tm, f // tf),
        in_specs=[
            pl.BlockSpec((tm, d), lambda i, j: (i, 0)),
            pl.BlockSpec((None, 1, d), lambda i, j: (layer, 0, 0)),
            pl.BlockSpec((None, d, tf), lambda i, j: (layer, 0, j)),
            pl.BlockSpec((None, d, tf), lambda i, j: (layer, 0, j)),
            pl.BlockSpec((None, tf, d), lambda i, j: (layer, j, 0)),
            pl.BlockSpec((1, d), lambda i, j: (0, 0)),
        ],
        out_specs=pl.BlockSpec((tm, d), lambda i, j: (i, 0)),
        out_shape=jax.ShapeDtypeStruct((n, d), F32),
        scratch_shapes=[pltpu.VMEM((tm, d), BF16)],
        compiler_params=_cparams(("parallel", "arbitrary")),
        name="ffn",
    )(x, g, wg, wu, wd, fg)


def _cast_body(w_ref, o_ref):
    o_ref[...] = w_ref[...].astype(o_ref.dtype)


def _to_bf16(w, tr):
    depth, r, c = w.shape
    spec = pl.BlockSpec((None, tr, c), lambda l, i: (l, i, 0))
    return pl.pallas_call(
        _cast_body,
        grid=(depth, r // tr),
        in_specs=[spec],
        out_specs=spec,
        out_shape=jax.ShapeDtypeStruct(w.shape, BF16),
        compiler_params=_cparams(("parallel", "parallel")),
        name="cast",
    )(w)


def _w_in_body(w_ref, o_ref):
    w = w_ref[...]
    seg = {name: w[:, off:off + width] for name, off, width in
           zip(IN_NAMES, (0,) + IN_OFFSETS, IN_SPLITS)}
    tr = w.shape[0]
    small_pad = jnp.zeros((tr, LANES - IDX_HEAD_DIM - IDX_HEADS - GLA_GATE_RANK), w.dtype)
    tail = jnp.zeros((tr, PROJ_COLS - COL_SM - LANES), w.dtype)
    order = ("aq", "iq", "gv", "gr", "m_a", "m_g", "gq", "gk", "ak", "av", "ik", "iw", "ga")
    o_ref[...] = jnp.concatenate([seg[k] for k in order] + [small_pad, tail], axis=1).astype(o_ref.dtype)


def _prep_w_in(w, tr=256):
    depth, d, c = w.shape
    return pl.pallas_call(
        _w_in_body,
        grid=(depth, d // tr),
        in_specs=[pl.BlockSpec((None, tr, c), lambda l, i: (l, i, 0))],
        out_specs=pl.BlockSpec((None, tr, PROJ_COLS), lambda l, i: (l, i, 0)),
        out_shape=jax.ShapeDtypeStruct((depth, d, PROJ_COLS), BF16),
        compiler_params=_cparams(("parallel", "parallel")),
        name="w_in_prep",
    )(w)


def _proj_body(x_ref, g_ref, w_ref, o_ref, h_ref):
    @pl.when(pl.program_id(1) == 0)
    def _():
        h_ref[...] = _rms(x_ref[...], g_ref[...]).astype(BF16)

    o_ref[...] = _dot(h_ref[...], w_ref[...]).astype(o_ref.dtype)


def _proj(x, g, w, layer, *, tm=1024, tn=PROJ_TN):
    n, d = x.shape
    c = w.shape[2]
    return pl.pallas_call(
        _proj_body,
        grid=(n // tm, c // tn),
        in_specs=[
            pl.BlockSpec((tm, d), lambda i, j: (i, 0)),
            pl.BlockSpec((None, 1, d), lambda i, j: (layer, 0, 0)),
            pl.BlockSpec((None, d, tn), lambda i, j: (layer, 0, j)),
        ],
        out_specs=pl.BlockSpec((tm, tn), lambda i, j: (i, j)),
        out_shape=jax.ShapeDtypeStruct((n, c), BF16),
        scratch_shapes=[pltpu.VMEM((tm, d), BF16)],
        compiler_params=_cparams(("parallel", "arbitrary")),
        name="proj",
    )(x, g, w)


def _rope_tables(length, head_dim, rot):
    half = rot // 2
    inv = ROPE_THETA ** (-jnp.arange(0, rot, 2, dtype=F32) / rot)
    ang = jnp.arange(length, dtype=F32)[:, None] * inv[None, :]
    cos, sin = jnp.cos(ang), jnp.sin(ang)
    zeros = jnp.zeros((length, head_dim - rot), F32)
    zh = jnp.zeros((length, half), F32)
    c = jnp.concatenate([cos, cos, zeros + 1.0], axis=1)
    s1 = jnp.concatenate([-sin, zh, zeros], axis=1)
    s2 = jnp.concatenate([zh, sin, zeros], axis=1)
    reps = LANES // head_dim
    return tuple(jnp.tile(t, (1, reps)) for t in (c, s1, s2))


def _rope_lanes(x, c, s1, s2, half):
    return x * c + pltpu.roll(x, LANES - half, 1) * s1 + pltpu.roll(x, half, 1) * s2


def _store_t(dst_ref, rows, val):
    vt = val.T.astype(dst_ref.dtype)
    for c in range(vt.shape[1] // DSA_TQ):
        dst_ref[c, rows, :] = vt[:, c * DSA_TQ:(c + 1) * DSA_TQ]


def _rope_body(aq_ref, iq_ref, ak_ref, av_ref, sm_ref, ca_ref, sa1_ref, sa2_ref, ci_ref, si1_ref, si2_ref,
               aqt_ref, iqt_ref, smt_ref, ako_ref, smo_ref, avt_ref):
    ca, sa1, sa2 = ca_ref[...], sa1_ref[...], sa2_ref[...]
    ci, si1, si2 = ci_ref[...], si1_ref[...], si2_ref[...]
    ha, hi = ATTN_ROT // 2, IDX_ROT // 2
    tk = DSA_TK
    for j in range(ATTN_Q // LANES):
        sl = slice(j * LANES, (j + 1) * LANES)
        r = _rope_lanes(aq_ref[:, sl].astype(F32), ca, sa1, sa2, ha)
        _store_t(aqt_ref, sl, r * (ATTN_HEAD_DIM ** -0.5 * LOG2E))
    for j in range(IDX_Q // LANES):
        sl = slice(j * LANES, (j + 1) * LANES)
        r = _rope_lanes(iq_ref[:, sl].astype(F32), ci, si1, si2, hi)
        _store_t(iqt_ref, sl, r * (IDX_HEAD_DIM ** -0.5))
    for j in range(ATTN_KV // LANES):
        sl = slice(j * LANES, (j + 1) * LANES)
        ako_ref[:, sl] = _rope_lanes(ak_ref[:, sl].astype(F32), ca, sa1, sa2, ha).astype(BF16)
        vt = av_ref[:, sl].astype(F32).T
        for c in range(vt.shape[1] // tk):
            avt_ref[c, j * VT_ROWS:j * VT_ROWS + LANES, :] = vt[:, c * tk:(c + 1) * tk].astype(BF16)
            avt_ref[c, j * VT_ROWS + LANES:(j + 1) * VT_ROWS, :] = jnp.ones((VT_ROWS - LANES, tk), BF16)
    sm = sm_ref[...].astype(F32)
    lane = lax.broadcasted_iota(jnp.int32, sm.shape, 1)
    sm = jnp.where(lane < IDX_HEAD_DIM, _rope_lanes(sm, ci, si1, si2, hi), sm)
    smo_ref[...] = sm.astype(BF16)
    _store_t(smt_ref, slice(0, LANES), sm)


def _rope(proj, seq, tabs_a, tabs_i, *, tr=512):
    n = proj.shape[0]
    nl = seq // tr
    tk = DSA_TK
    tab = pl.BlockSpec((tr, LANES), lambda i: (i % nl, 0))
    return pl.pallas_call(
        _rope_body,
        grid=(n // tr,),
        in_specs=[
            pl.BlockSpec((tr, ATTN_Q), lambda i: (i, COL_AQ // ATTN_Q)),
            pl.BlockSpec((tr, IDX_Q), lambda i: (i, COL_IQ // IDX_Q)),
            pl.BlockSpec((tr, ATTN_KV), lambda i: (i, COL_AK // ATTN_KV)),
            pl.BlockSpec((tr, ATTN_KV), lambda i: (i, COL_AV // ATTN_KV)),
            pl.BlockSpec((tr, LANES), lambda i: (i, COL_SM // LANES)),
            tab, tab, tab, tab, tab, tab,
        ],
        out_specs=[
            pl.BlockSpec((tr // DSA_TQ, ATTN_Q, DSA_TQ), lambda i: (i, 0, 0)),
            pl.BlockSpec((tr // DSA_TQ, IDX_Q, DSA_TQ), lambda i: (i, 0, 0)),
            pl.BlockSpec((tr // DSA_TQ, LANES, DSA_TQ), lambda i: (i, 0, 0)),
            pl.BlockSpec((tr, ATTN_KV), lambda i: (i, 0)),
            pl.BlockSpec((tr, LANES), lambda i: (i, 0)),
            pl.BlockSpec((tr // tk, ATTN_KV_HEADS * VT_ROWS, tk), lambda i: (i, 0, 0)),
        ],
        out_shape=[
            jax.ShapeDtypeStruct((n // DSA_TQ, ATTN_Q, DSA_TQ), BF16),
            jax.ShapeDtypeStruct((n // DSA_TQ, IDX_Q, DSA_TQ), BF16),
            jax.ShapeDtypeStruct((n // DSA_TQ, LANES, DSA_TQ), BF16),
            jax.ShapeDtypeStruct((n, ATTN_KV), BF16),
            jax.ShapeDtypeStruct((n, LANES), BF16),
            jax.ShapeDtypeStruct((n // tk, ATTN_KV_HEADS * VT_ROWS, tk), BF16),
        ],
        compiler_params=_cparams(("parallel",)),
        name="rope",
    )(proj, proj, proj, proj, proj, *tabs_a, *tabs_i)


def _sortable(x):
    i = pltpu.bitcast(x + 0.0, jnp.int32)
    return jnp.where(i < 0, i ^ jnp.int32(0x7FFFFFFF), i)


def _tree_sum(xs):
    while len(xs) > 1:
        xs = [xs[i] + xs[i + 1] for i in range(0, len(xs) - 1, 2)] + ([xs[-1]] if len(xs) % 2 else [])
    return xs[0]


def _colsum(x):
    r, c = x.shape
    return jnp.sum(jnp.sum(x.reshape(r // 8, 8, c), axis=0), axis=0, keepdims=True)


def _dsa_body(iqt_ref, smt_ref, aqt_ref, smk_ref, ak_ref, avt_ref, o_ref,
              key_ref, hi_ref, lo_ref, m_ref, al_ref, acc_ref, s0_ref, s1_ref, p0_ref, p1_ref, *, topk, seq):
    tq, tk = DSA_TQ, DSA_TK
    qi = pl.program_id(1)
    t0 = qi * tq
    nkb = (t0 + tq + tk - 1) // tk
    iwt = smt_ref[SM_IW:SM_IW + IDX_HEADS, :].astype(F32) * (IDX_HEADS ** -0.5)
    qpos = lax.broadcasted_iota(jnp.int32, (tk, tq), 1) + t0
    kiota = lax.broadcasted_iota(jnp.int32, (tk, tq), 0)

    def idx_block(kb, carry):
        k0 = pl.multiple_of(kb * tk, tk)
        ik = smk_ref[pl.ds(k0, tk), :][:, SM_IK:SM_IK + IDX_HEAD_DIM]
        score = jnp.zeros((tk, tq), F32)
        for h in range(0, IDX_HEADS, 2):
            qq = jnp.concatenate([iqt_ref[(h + u) * IDX_HEAD_DIM:(h + u + 1) * IDX_HEAD_DIM, :]
                                  for u in range(2)], axis=1)
            rel = jnp.maximum(_dot(ik, qq), 0.0)
            score = score + rel[:, :tq] * iwt[h:h + 1, :] + rel[:, tq:] * iwt[h + 1:h + 2, :]
        key = jnp.where(kiota + k0 <= qpos, _sortable(score), jnp.int32(INT_MIN))
        key_ref[kb] = key
        hi_ref[kb] = (key >> 16).astype(jnp.int16)
        lo_ref[kb] = ((key & 0xFFFF) + HALF_MIN).astype(jnp.int16)
        return carry

    lax.fori_loop(0, nkb, idx_block, 0)

    @pl.when(nkb % 2 == 1)
    def _():
        key_ref[nkb] = jnp.full((tk, tq), INT_MIN, jnp.int32)
        hi_ref[nkb] = jnp.full((tk, tq), HALF_MIN, jnp.int16)
        lo_ref[nkb] = jnp.full((tk, tq), HALF_MIN, jnp.int16)

    npair = (nkb + 1) // 2

    def count(pred):
        def pair(i, c):
            for u in range(2):
                kb = 2 * i + u
                hit = jnp.where(pred(key_ref[kb], kb), 1.0, 0.0)
                c = c + jnp.sum(hit.reshape(tk // 8, 8, tq), axis=0)
            return c
        c = lax.fori_loop(0, npair, pair, jnp.zeros((8, tq), F32))
        return jnp.sum(c, axis=0, keepdims=True)

    def pack16(v):
        return jnp.broadcast_to(v, (PACK16, tq)).astype(jnp.int16)

    kf = float(topk)

    def threshold_stage(pairs):
        blocks = range(2 * pairs)

        def count16(ref, pred):
            one, zero = jnp.int16(1), jnp.int16(0)
            c = _tree_sum([jnp.where(pred(ref[kb, j:j + PACK16, :]), one, zero)
                           for kb in blocks for j in range(0, tk, PACK16)])
            return jnp.sum(c.astype(F32), axis=0, keepdims=True)

        def bisect16(ref, want):
            def step(it, t):
                cand = t + lax.shift_left(jnp.int32(1), jnp.int32(15) - it)
                c16 = pack16(cand)
                return jnp.where(count16(ref, lambda b: b >= c16) >= want, cand, t)
            return lax.fori_loop(0, 16, step, jnp.full((1, tq), HALF_MIN, jnp.int32))

        t_hi = bisect16(hi_ref, kf)
        t_hi16 = pack16(t_hi)
        n_hi_gt = count16(hi_ref, lambda b: b > t_hi16)
        for kb in blocks:
            for j in range(0, tk, PACK16):
                rows = slice(j, j + PACK16)
                lo_ref[kb, rows, :] = jnp.where(hi_ref[kb, rows, :] == t_hi16, lo_ref[kb, rows, :],
                                                jnp.int16(HALF_MIN))
        t_lo = bisect16(lo_ref, kf - n_hi_gt)
        t_lo16 = pack16(t_lo)
        return (t_hi * 65536 + (t_lo - HALF_MIN),
                n_hi_gt + count16(lo_ref, lambda b: b > t_lo16),
                n_hi_gt + count16(lo_ref, lambda b: b >= t_lo16))

    max_pairs = (seq // tk + 1) // 2
    thr, n_gt, n_ge = lax.switch(npair - 1, [functools.partial(threshold_stage, p)
                                             for p in range(1, max_pairs + 1)])

    need = kf - n_gt
    tied = (n_ge > kf) & (thr > jnp.int32(INT_MIN))

    def tie_break():
        def step(it, x):
            cand = x + lax.shift_left(jnp.int32(1), jnp.int32(seq.bit_length() - 1) - it)
            below = count(lambda k, kb: (k == thr) & (kiota + kb * tk < cand))
            return jnp.where(below < need, cand, x)
        return lax.fori_loop(0, seq.bit_length(), step, jnp.zeros((1, tq), jnp.int32))

    jtie = lax.cond(jnp.max(jnp.where(tied, 1.0, 0.0)) > 0.0, tie_break,
                    lambda: jnp.zeros((1, tq), jnp.int32))
    jmax = jnp.where(tied, jtie, jnp.where(thr > jnp.int32(INT_MIN), jnp.int32(seq), jnp.int32(-1)))

    last = nkb + nkb % 2 - 1

    def scores(kb, s_ref):
        kb = jnp.minimum(kb, last)
        kc = jnp.minimum(kb, seq // tk - 1)
        k0 = pl.multiple_of(kc * tk, tk)
        key = key_ref[kb]
        sel = (key > thr) | ((key == thr) & (kiota + kb * tk <= jmax))
        bias = jnp.where(sel, 0.0, NEG_INF)
        bias = jnp.concatenate([bias] * ATTN_REP, axis=1)
        for g in range(ATTN_KV_HEADS):
            kg = ak_ref[pl.ds(k0, tk), g * ATTN_HEAD_DIM:(g + 1) * ATTN_HEAD_DIM]
            qt = jnp.concatenate(
                [aqt_ref[(g * ATTN_REP + r) * ATTN_HEAD_DIM:(g * ATTN_REP + r + 1) * ATTN_HEAD_DIM, :]
                 for r in range(ATTN_REP)], axis=1)
            s_ref[g] = _dot(kg, qt) + bias

    def softmax(s_ref, p_ref):
        for g in range(ATTN_KV_HEADS):
            for r in range(ATTN_REP):
                cols = slice(r * tq, (r + 1) * tq)
                s = s_ref[g, :, cols]
                m_old = m_ref[g, :, cols]
                m_new = jnp.maximum(m_old, jnp.max(s, axis=0, keepdims=True))
                p_ref[g, :, cols] = jnp.exp2(s - m_new).astype(BF16)
                al_ref[g, :, cols] = jnp.exp2(m_old - m_new)
                m_ref[g, :, cols] = m_new

    def values(kb, p_ref):
        kc = jnp.clip(kb, 0, seq // tk - 1)
        for g in range(ATTN_KV_HEADS):
            vt = avt_ref[kc, g * VT_ROWS:(g + 1) * VT_ROWS, :]
            acc_ref[g] = al_ref[g] * acc_ref[g] + _dot(vt, p_ref[g])

    m_ref[...] = jnp.full_like(m_ref, NEG_INF)
    acc_ref[...] = jnp.zeros_like(acc_ref)
    al_ref[...] = jnp.ones_like(al_ref)
    p1_ref[...] = jnp.zeros_like(p1_ref)
    scores(0, s0_ref)

    def att_pair(i, carry):
        j = 2 * i
        values(j - 1, p1_ref)
        softmax(s0_ref, p0_ref)
        scores(j + 1, s1_ref)
        values(j, p0_ref)
        softmax(s1_ref, p1_ref)
        scores(j + 2, s0_ref)
        return carry

    lax.fori_loop(0, npair, att_pair, 0)
    values(last, p1_ref)
    for g in range(ATTN_KV_HEADS):
        acc = acc_ref[g]
        ot = acc[:ATTN_HEAD_DIM] / acc[ATTN_HEAD_DIM:ATTN_HEAD_DIM + 1]
        for r in range(ATTN_REP):
            hh = g * ATTN_REP + r
            o_ref[:, hh * ATTN_HEAD_DIM:(hh + 1) * ATTN_HEAD_DIM] = ot[:, r * tq:(r + 1) * tq].T.astype(BF16)


def _dsa(aq_t, iq_t, sm_t, ak_r, sm_r, av_t, batch, seq):
    n = ak_r.shape[0]
    tq, tk = DSA_TQ, DSA_TK
    nq = seq // tq
    nkb = seq // tk
    topk = min(TOPK_MAX, seq // 4)
    qmap = lambda b, i: (b * nq + i, 0, 0)
    return pl.pallas_call(
        functools.partial(_dsa_body, topk=topk, seq=seq),
        grid=(batch, nq),
        in_specs=[
            pl.BlockSpec((None, IDX_Q, tq), qmap),
            pl.BlockSpec((None, LANES, tq), qmap),
            pl.BlockSpec((None, ATTN_Q, tq), qmap),
            pl.BlockSpec((seq, LANES), lambda b, i: (b, 0)),
            pl.BlockSpec((seq, ATTN_KV), lambda b, i: (b, 0)),
            pl.BlockSpec((nkb, ATTN_KV_HEADS * VT_ROWS, tk), lambda b, i: (b, 0, 0)),
        ],
        out_specs=pl.BlockSpec((tq, ATTN_Q), lambda b, i: (b * nq + i, 0)),
        out_shape=jax.ShapeDtypeStruct((n, ATTN_Q), BF16),
        scratch_shapes=[
            pltpu.VMEM((nkb + nkb % 2, tk, tq), jnp.int32),
            pltpu.VMEM((nkb + nkb % 2, tk, tq), jnp.int16),
            pltpu.VMEM((nkb + nkb % 2, tk, tq), jnp.int16),
            pltpu.VMEM((ATTN_KV_HEADS, 1, ATTN_REP * tq), F32),
            pltpu.VMEM((ATTN_KV_HEADS, 1, ATTN_REP * tq), F32),
            pltpu.VMEM((ATTN_KV_HEADS, VT_ROWS, ATTN_REP * tq), F32),
            pltpu.VMEM((ATTN_KV_HEADS, tk, ATTN_REP * tq), F32),
            pltpu.VMEM((ATTN_KV_HEADS, tk, ATTN_REP * tq), F32),
            pltpu.VMEM((ATTN_KV_HEADS, tk, ATTN_REP * tq), BF16),
            pltpu.VMEM((ATTN_KV_HEADS, tk, ATTN_REP * tq), BF16),
        ],
        compiler_params=_cparams(("parallel", "arbitrary")),
        name="dsa",
    )(iq_t, sm_t, aq_t, sm_r, ak_r, av_t)


GLA_ROWS = 256


def _split3(x):
    hi = x.astype(BF16)
    r1 = x - hi.astype(F32)
    mid = r1.astype(BF16)
    lo = (r1 - mid.astype(F32)).astype(BF16)
    return hi, mid, lo


def _gla_body(q_ref, k_ref, v_ref, r_ref, sm_ref, wa_ref, ba_ref, gn_ref, o_ref, st_ref):
    rows, c = GLA_ROWS, GLA_CHUNK

    @pl.when(pl.program_id(2) == 0)
    def _():
        st_ref[...] = jnp.zeros_like(st_ref)

    ga = sm_ref[:, SM_GA:SM_GA + GLA_GATE_RANK]
    z = _dot(ga, wa_ref[...]) + ba_ref[...]
    g = -(jnp.maximum(-z, 0.0) + jnp.log1p(jnp.exp(-jnp.abs(z)))) * (1.0 / GLA_GATE_NORM)
    ri = lax.broadcasted_iota(jnp.int32, (rows, rows), 0)
    ci = lax.broadcasted_iota(jnp.int32, (rows, rows), 1)
    tri = jnp.where((ri // c == ci // c) & (ci <= ri), 1.0, 0.0).astype(BF16)
    hi, mid, lo = _split3(g)
    b = _dot(tri, hi) + _dot(tri, mid) + _dot(tri, lo)
    low = lax.broadcasted_iota(jnp.int32, (c, c), 1) <= lax.broadcasted_iota(jnp.int32, (c, c), 0)
    gn = gn_ref[...]
    for ch in range(rows // c):
        sl = slice(ch * c, (ch + 1) * c)
        bc = b[sl]
        bl = bc[c - 1:c, :]
        q = q_ref[sl, :].astype(F32) * (GLA_DK ** -0.5)
        k = k_ref[sl, :].astype(F32)
        v = v_ref[sl, :]
        qe = (q * jnp.exp(bc - bl)).astype(BF16)
        kd = (k * jnp.exp(bl - bc)).astype(BF16)
        qb = (q * jnp.exp(bc)).astype(BF16)
        a = jnp.where(low, _dot_nt(qe, kd), 0.0).astype(BF16)
        st = st_ref[...]
        o = _dot(a, v) + _dot_nt(qb, st.astype(BF16))
        vt = v.astype(F32).T.astype(BF16)
        st_ref[...] = st * jnp.exp(bl) + _dot(vt, kd)
        o = _rms(o, gn)
        gr = r_ref[sl, :].astype(F32)
        o_ref[sl, :] = (o * (gr * jax.nn.sigmoid(gr))).astype(BF16)


def _gla(proj, wa2, ba, gn, batch, seq):
    n = proj.shape[0]
    rows = GLA_ROWS
    ng = seq // rows
    rmap = lambda off, w: (lambda b, h, i: (b * ng + i, off // w + h))
    return pl.pallas_call(
        _gla_body,
        grid=(batch, GLA_HEADS, ng),
        in_specs=[
            pl.BlockSpec((rows, GLA_DK), rmap(COL_GQ, GLA_DK)),
            pl.BlockSpec((rows, GLA_DK), rmap(COL_GK, GLA_DK)),
            pl.BlockSpec((rows, GLA_DV), rmap(COL_GV, GLA_DV)),
            pl.BlockSpec((rows, GLA_DV), rmap(COL_GR, GLA_DV)),
            pl.BlockSpec((rows, LANES), lambda b, h, i: (b * ng + i, COL_SM // LANES)),
            pl.BlockSpec((GLA_GATE_RANK, GLA_DK), lambda b, h, i: (0, h)),
            pl.BlockSpec((1, GLA_DK), lambda b, h, i: (0, h)),
            pl.BlockSpec((1, GLA_DV), lambda b, h, i: (0, 0)),
        ],
        out_specs=pl.BlockSpec((rows, GLA_DV), lambda b, h, i: (b * ng + i, h)),
        out_shape=jax.ShapeDtypeStruct((n, GLA_V), BF16),
        scratch_shapes=[pltpu.VMEM((GLA_DV, GLA_DK), F32)],
        compiler_params=_cparams(("parallel", "parallel", "arbitrary")),
        name="gla",
    )(proj, proj, proj, proj, proj, wa2, ba, gn)


def _merge_body(x_ref, oa_ref, og_ref, ma_ref, mg_ref, wua_ref, wug_ref, wo_ref, o_ref):
    ya = jax.nn.sigmoid(ma_ref[...].astype(F32)) * _dot(oa_ref[...], wua_ref[...])
    yg = jax.nn.sigmoid(mg_ref[...].astype(F32)) * _dot(og_ref[...], wug_ref[...])
    y = (ya + yg).astype(BF16)
    o_ref[...] = x_ref[...] + _dot(y, wo_ref[...])


def _merge(x, o_attn, o_gla, proj, wua, wug, wo, layer, *, tm=256):
    n, d = x.shape
    const = lambda i: (layer, 0, 0)
    return pl.pallas_call(
        _merge_body,
        grid=(n // tm,),
        in_specs=[
            pl.BlockSpec((tm, d), lambda i: (i, 0)),
            pl.BlockSpec((tm, ATTN_Q), lambda i: (i, 0)),
            pl.BlockSpec((tm, GLA_V), lambda i: (i, 0)),
            pl.BlockSpec((tm, d), lambda i: (i, COL_MA // D_MODEL)),
            pl.BlockSpec((tm, d), lambda i: (i, COL_MG // D_MODEL)),
            pl.BlockSpec((None, ATTN_Q, d), const),
            pl.BlockSpec((None, GLA_V, d), const),
            pl.BlockSpec((None, d, d), const),
        ],
        out_specs=pl.BlockSpec((tm, d), lambda i: (i, 0)),
        out_shape=jax.ShapeDtypeStruct((n, d), F32),
        compiler_params=_cparams(("parallel",)),
        name="merge",
    )(x, o_attn, o_gla, proj, proj, wua, wug, wo)


def kernel(x, ffn1_norm, ffn1_w_gate, ffn1_w_up, ffn1_w_down, mix_norm, w_in, gla_w_a2, gla_b_a, gla_norm, w_up_attn, w_up_gla, w_out, ffn2_norm, ffn2_w_gate, ffn2_w_up, ffn2_w_down, final_norm):
    batch, seq, d = x.shape
    depth = w_in.shape[0]
    tabs_a = _rope_tables(seq, ATTN_HEAD_DIM, ATTN_ROT)
    tabs_i = _rope_tables(seq, IDX_HEAD_DIM, IDX_ROT)
    fg = final_norm.reshape(1, d)
    g1, gm, g2 = (a.reshape(depth, 1, d) for a in (ffn1_norm, mix_norm, ffn2_norm))
    ffn1 = (_to_bf16(ffn1_w_gate, 256), _to_bf16(ffn1_w_up, 256), _to_bf16(ffn1_w_down, 512))
    ffn2 = (_to_bf16(ffn2_w_gate, 256), _to_bf16(ffn2_w_up, 256), _to_bf16(ffn2_w_down, 512))
    w_in_b = _prep_w_in(w_in)
    wua, wug, wo = _to_bf16(w_up_attn, 512), _to_bf16(w_up_gla, 512), _to_bf16(w_out, 512)
    h = x.reshape(batch * seq, d)
    for l in range(depth):
        h = _ffn(h, g1, *ffn1, fg, l, final_norm=False)
        proj = _proj(h, gm, w_in_b, l)
        aq_t, iq_t, sm_t, ak_r, sm_r, av_t = _rope(proj, seq, tabs_a, tabs_i)
        o_attn = _dsa(aq_t, iq_t, sm_t, ak_r, sm_r, av_t, batch, seq)
        o_gla = _gla(proj, gla_w_a2[l].astype(BF16), gla_b_a[l].reshape(1, GLA_K),
                     gla_norm[l].reshape(1, GLA_DV), batch, seq)
        h = _merge(h, o_attn, o_gla, proj, wua, wug, wo, l)
        h = _ffn(h, g2, *ffn2, fg, l, final_norm=(l == depth - 1))
    return h.reshape(batch, seq, d)
```

```python
import functools

import jax
import jax.numpy as jnp
import numpy as np
from jax import lax
from jax.experimental import pallas as pl
from jax.experimental.pallas import tpu as pltpu

F32 = jnp.float32
BF16 = jnp.bfloat16

D_MODEL = 2048
D_FF = 5632
FFN_RES = 0.5
ATTN_HEADS = 8
ATTN_KV_HEADS = 2
ATTN_HEAD_DIM = 128
ATTN_REP = ATTN_HEADS // ATTN_KV_HEADS
IDX_HEADS = 16
IDX_HEAD_DIM = 64
TOPK_MAX = 256
GLA_HEADS = 4
GLA_DK = 128
GLA_DV = 256
GLA_GATE_RANK = 16
GLA_GATE_NORM = 16.0
GLA_CHUNK = 64
ROPE_THETA = 500000.0
ATTN_ROT = ATTN_HEAD_DIM // 4
IDX_ROT = IDX_HEAD_DIM // 4
EPS = 1e-6
NEG_INF = -1e30
INT_MIN = -(2 ** 31)
HALF_MIN = -(2 ** 15)
PACK16 = 16

ATTN_Q = ATTN_HEADS * ATTN_HEAD_DIM
ATTN_KV = ATTN_KV_HEADS * ATTN_HEAD_DIM
IDX_Q = IDX_HEADS * IDX_HEAD_DIM
GLA_K = GLA_HEADS * GLA_DK
GLA_V = GLA_HEADS * GLA_DV
IN_SPLITS = (ATTN_Q, ATTN_KV, ATTN_KV, IDX_Q, IDX_HEAD_DIM, IDX_HEADS,
             GLA_K, GLA_K, GLA_V, GLA_GATE_RANK, GLA_V, D_MODEL, D_MODEL)
IN_OFFSETS = tuple(int(v) for v in np.cumsum(IN_SPLITS)[:-1])
IN_NAMES = ("aq", "ak", "av", "iq", "ik", "iw", "gq", "gk", "gv", "ga", "gr", "m_a", "m_g")

LANES = 128
COL_AQ = 0
COL_IQ = COL_AQ + ATTN_Q
COL_GV = COL_IQ + IDX_Q
COL_GR = COL_GV + GLA_V
COL_MA = COL_GR + GLA_V
COL_MG = COL_MA + D_MODEL
COL_GQ = COL_MG + D_MODEL
COL_GK = COL_GQ + GLA_K
COL_AK = COL_GK + GLA_K
COL_AV = COL_AK + ATTN_KV
COL_SM = COL_AV + ATTN_KV
SM_IK = 0
SM_IW = SM_IK + IDX_HEAD_DIM
SM_GA = SM_IW + IDX_HEADS
PROJ_TN = 1024
PROJ_COLS = -(-(COL_SM + LANES) // PROJ_TN) * PROJ_TN

DSA_TQ = 128
DSA_TK = 256
VT_ROWS = ATTN_HEAD_DIM + PACK16
LOG2E = 1.4426950408889634

FFN_SLAB = 512

VMEM_LIMIT = 62 * 1024 * 1024


def _cparams(sem):
    return pltpu.CompilerParams(dimension_semantics=sem, vmem_limit_bytes=VMEM_LIMIT)


def _dot(a, b):
    return jnp.dot(a, b, preferred_element_type=F32)


def _dot_nt(a, b):
    return lax.dot_general(a, b, (((1,), (1,)), ((), ())), preferred_element_type=F32)


def _rms(x, g):
    return x * lax.rsqrt(jnp.mean(x * x, axis=-1, keepdims=True) + EPS) * g


def _ffn_body(x_ref, g_ref, wg_ref, wu_ref, wd_ref, fg_ref, o_ref, h_ref, *, final_norm):
    f = pl.program_id(1)

    slabs = [slice(r, r + FFN_SLAB) for r in range(0, x_ref.shape[0], FFN_SLAB)]

    @pl.when(f == 0)
    def _():
        for rows in slabs:
            h_ref[rows, :] = _rms(x_ref[rows, :], g_ref[...]).astype(BF16)
        o_ref[...] = jnp.zeros_like(o_ref)

    for rows in slabs:
        h = h_ref[rows, :]
        gate = _dot(h, wg_ref[...])
        up = _dot(h, wu_ref[...])
        act = (gate * jax.nn.sigmoid(gate) * up).astype(BF16)
        o_ref[rows, :] += _dot(act, wd_ref[...])

    @pl.when(f == pl.num_programs(1) - 1)
    def _():
        for rows in slabs:
            y = x_ref[rows, :] + FFN_RES * o_ref[rows, :]
            if final_norm:
                y = _rms(y, fg_ref[...])
            o_ref[rows, :] = y


def _ffn(x, g, wg, wu, wd, fg, layer, *, final_norm, tm=1024, tf=512):
    n, d = x.shape
    f = wg.shape[2]
    return pl.pallas_call(
        functools.partial(_ffn_body, final_norm=final_norm),
        grid=(n // tm, f // tf),
        in_specs=[
            pl.BlockSpec((tm, d), lambda i, j: (i, 0)),
            pl.BlockSpec((None, 1, d), lambda i, j: (layer, 0, 0)),
            pl.BlockSpec((None, d, tf), lambda i, j: (layer, 0, j)),
            pl.BlockSpec((None, d, tf), lambda i, j: (layer, 0, j)),
            pl.BlockSpec((None, tf, d), lambda i, j: (layer, j, 0)),
            pl.BlockSpec((1, d), lambda i, j: (0, 0)),
        ],
        out_specs=pl.BlockSpec((tm, d), lambda i, j: (i, 0)),
        out_shape=jax.ShapeDtypeStruct((n, d), F32),
        scratch_shapes=[pltpu.VMEM((tm, d), BF16)],
        compiler_params=_cparams(("parallel", "arbitrary")),
        name="ffn",
    )(x, g, wg, wu, wd, fg)


def _cast_body(w_ref, o_ref):
    o_ref[...] = w_ref[...].astype(o_ref.dtype)


def _to_bf16(w, tr):
    depth, r, c = w.shape
    spec = pl.BlockSpec((None, tr, c), lambda l, i: (l, i, 0))
    return pl.pallas_call(
        _cast_body,
        grid=(depth, r // tr),
        in_specs=[spec],
        out_specs=spec,
        out_shape=jax.ShapeDtypeStruct(w.shape, BF16),
        compiler_params=_cparams(("parallel", "parallel")),
        name="cast",
    )(w)


def _w_in_body(w_ref, o_ref):
    w = w_ref[...]
    seg = {name: w[:, off:off + width] for name, off, width in
           zip(IN_NAMES, (0,) + IN_OFFSETS, IN_SPLITS)}
    tr = w.shape[0]
    small_pad = jnp.zeros((tr, LANES - IDX_HEAD_DIM - IDX_HEADS - GLA_GATE_RANK), w.dtype)
    tail = jnp.zeros((tr, PROJ_COLS - COL_SM - LANES), w.dtype)
    order = ("aq", "iq", "gv", "gr", "m_a", "m_g", "gq", "gk", "ak", "av", "ik", "iw", "ga")
    o_ref[...] = jnp.concatenate([seg[k] for k in order] + [small_pad, tail], axis=1).astype(o_ref.dtype)


def _prep_w_in(w, tr=256):
    depth, d, c = w.shape
    return pl.pallas_call(
        _w_in_body,
        grid=(depth, d // tr),
        in_specs=[pl.BlockSpec((None, tr, c), lambda l, i: (l, i, 0))],
        out_specs=pl.BlockSpec((None, tr, PROJ_COLS), lambda l, i: (l, i, 0)),
        out_shape=jax.ShapeDtypeStruct((depth, d, PROJ_COLS), BF16),
        compiler_params=_cparams(("parallel", "parallel")),
        name="w_in_prep",
    )(w)


def _proj_body(x_ref, g_ref, w_ref, o_ref, h_ref):
    @pl.when(pl.program_id(1) == 0)
    def _():
        h_ref[...] = _rms(x_ref[...], g_ref[...]).astype(BF16)

    o_ref[...] = _dot(h_ref[...], w_ref[...]).astype(o_ref.dtype)


def _proj(x, g, w, layer, *, tm=1024, tn=PROJ_TN):
    n, d = x.shape
    c = w.shape[2]
    return pl.pallas_call(
        _proj_body,
        grid=(n // tm, c // tn),
        in_specs=[
            pl.BlockSpec((tm, d), lambda i, j: (i, 0)),
            pl.BlockSpec((None, 1, d), lambda i, j: (layer, 0, 0)),
            pl.BlockSpec((None, d, tn), lambda i, j: (layer, 0, j)),
        ],
        out_specs=pl.BlockSpec((tm, tn), lambda i, j: (i, j)),
        out_shape=jax.ShapeDtypeStruct((n, c), BF16),
        scratch_shapes=[pltpu.VMEM((tm, d), BF16)],
        compiler_params=_cparams(("parallel", "arbitrary")),
        name="proj",
    )(x, g, w)


def _rope_tables(length, head_dim, rot):
    half = rot // 2
    inv = ROPE_THETA ** (-jnp.arange(0, rot, 2, dtype=F32) / rot)
    ang = jnp.arange(length, dtype=F32)[:, None] * inv[None, :]
    cos, sin = jnp.cos(ang), jnp.sin(ang)
    zeros = jnp.zeros((length, head_dim - rot), F32)
    zh = jnp.zeros((length, half), F32)
    c = jnp.concatenate([cos, cos, zeros + 1.0], axis=1)
    s1 = jnp.concatenate([-sin, zh, zeros], axis=1)
    s2 = jnp.concatenate([zh, sin, zeros], axis=1)
    reps = LANES // head_dim
    return tuple(jnp.tile(t, (1, reps)) for t in (c, s1, s2))


def _rope_lanes(x, c, s1, s2, half):
    return x * c + pltpu.roll(x, LANES - half, 1) * s1 + pltpu.roll(x, half, 1) * s2


def _store_t(dst_ref, rows, val):
    vt = val.T.astype(dst_ref.dtype)
    for c in range(vt.shape[1] // DSA_TQ):
        dst_ref[c, rows, :] = vt[:, c * DSA_TQ:(c + 1) * DSA_TQ]


def _rope_body(aq_ref, iq_ref, ak_ref, av_ref, sm_ref, ca_ref, sa1_ref, sa2_ref, ci_ref, si1_ref, si2_ref,
               aqt_ref, iqt_ref, smt_ref, ako_ref, smo_ref, avt_ref):
    ca, sa1, sa2 = ca_ref[...], sa1_ref[...], sa2_ref[...]
    ci, si1, si2 = ci_ref[...], si1_ref[...], si2_ref[...]
    ha, hi = ATTN_ROT // 2, IDX_ROT // 2
    tk = DSA_TK
    for j in range(ATTN_Q // LANES):
        sl = slice(j * LANES, (j + 1) * LANES)
        r = _rope_lanes(aq_ref[:, sl].astype(F32), ca, sa1, sa2, ha)
        _store_t(aqt_ref, sl, r * (ATTN_HEAD_DIM ** -0.5 * LOG2E))
    for j in range(IDX_Q // LANES):
        sl = slice(j * LANES, (j + 1) * LANES)
        r = _rope_lanes(iq_ref[:, sl].astype(F32), ci, si1, si2, hi)
        _store_t(iqt_ref, sl, r * (IDX_HEAD_DIM ** -0.5))
    for j in range(ATTN_KV // LANES):
        sl = slice(j * LANES, (j + 1) * LANES)
        ako_ref[:, sl] = _rope_lanes(ak_ref[:, sl].astype(F32), ca, sa1, sa2, ha).astype(BF16)
        vt = av_ref[:, sl].astype(F32).T
        for c in range(vt.shape[1] // tk):
            avt_ref[c, j * VT_ROWS:j * VT_ROWS + LANES, :] = vt[:, c * tk:(c + 1) * tk].astype(BF16)
            avt_ref[c, j * VT_ROWS + LANES:(j + 1) * VT_ROWS, :] = jnp.ones((VT_ROWS - LANES, tk), BF16)
    sm = sm_ref[...].astype(F32)
    lane = lax.broadcasted_iota(jnp.int32, sm.shape, 1)
    sm = jnp.where(lane < IDX_HEAD_DIM, _rope_lanes(sm, ci, si1, si2, hi), sm)
    smo_ref[...] = sm.astype(BF16)
    _store_t(smt_ref, slice(0, LANES), sm)


def _rope(proj, seq, tabs_a, tabs_i, *, tr=512):
    n = proj.shape[0]
    nl = seq // tr
    tk = DSA_TK
    tab = pl.BlockSpec((tr, LANES), lambda i: (i % nl, 0))
    return pl.pallas_call(
        _rope_body,
        grid=(n // tr,),
        in_specs=[
            pl.BlockSpec((tr, ATTN_Q), lambda i: (i, COL_AQ // ATTN_Q)),
            pl.BlockSpec((tr, IDX_Q), lambda i: (i, COL_IQ // IDX_Q)),
            pl.BlockSpec((tr, ATTN_KV), lambda i: (i, COL_AK // ATTN_KV)),
            pl.BlockSpec((tr, ATTN_KV), lambda i: (i, COL_AV // ATTN_KV)),
            pl.BlockSpec((tr, LANES), lambda i: (i, COL_SM // LANES)),
            tab, tab, tab, tab, tab, tab,
        ],
        out_specs=[
            pl.BlockSpec((tr // DSA_TQ, ATTN_Q, DSA_TQ), lambda i: (i, 0, 0)),
            pl.BlockSpec((tr // DSA_TQ, IDX_Q, DSA_TQ), lambda i: (i, 0, 0)),
            pl.BlockSpec((tr // DSA_TQ, LANES, DSA_TQ), lambda i: (i, 0, 0)),
            pl.BlockSpec((tr, ATTN_KV), lambda i: (i, 0)),
            pl.BlockSpec((tr, LANES), lambda i: (i, 0)),
            pl.BlockSpec((tr // tk, ATTN_KV_HEADS * VT_ROWS, tk), lambda i: (i, 0, 0)),
        ],
        out_shape=[
            jax.ShapeDtypeStruct((n // DSA_TQ, ATTN_Q, DSA_TQ), BF16),
            jax.ShapeDtypeStruct((n // DSA_TQ, IDX_Q, DSA_TQ), BF16),
            jax.ShapeDtypeStruct((n // DSA_TQ, LANES, DSA_TQ), BF16),
            jax.ShapeDtypeStruct((n, ATTN_KV), BF16),
            jax.ShapeDtypeStruct((n, LANES), BF16),
            jax.ShapeDtypeStruct((n // tk, ATTN_KV_HEADS * VT_ROWS, tk), BF16),
        ],
        compiler_params=_cparams(("parallel",)),
        name="rope",
    )(proj, proj, proj, proj, proj, *tabs_a, *tabs_i)


def _sortable(x):
    i = pltpu.bitcast(x + 0.0, jnp.int32)
    return jnp.where(i < 0, i ^ jnp.int32(0x7FFFFFFF), i)


def _tree_sum(xs):
    while len(xs) > 1:
        xs = [xs[i] + xs[i + 1] for i in range(0, len(xs) - 1, 2)] + ([xs[-1]] if len(xs) % 2 else [])
    return xs[0]


def _colsum(x):
    r, c = x.shape
    return jnp.sum(jnp.sum(x.reshape(r // 8, 8, c), axis=0), axis=0, keepdims=True)


def _dsa_body(iqt_ref, smt_ref, aqt_ref, smk_ref, ak_ref, avt_ref, o_ref,
              key_ref, hi_ref, lo_ref, m_ref, al_ref, acc_ref, s0_ref, s1_ref, p0_ref, p1_ref, *, topk, seq):
    tq, tk = DSA_TQ, DSA_TK
    qi = pl.program_id(1)
    t0 = qi * tq
    nkb = (t0 + tq + tk - 1) // tk
    iwt = smt_ref[SM_IW:SM_IW + IDX_HEADS, :].astype(F32) * (IDX_HEADS ** -0.5)
    qpos = lax.broadcasted_iota(jnp.int32, (tk, tq), 1) + t0
    kiota = lax.broadcasted_iota(jnp.int32, (tk, tq), 0)

    npair = (nkb + 1) // 2

    def idx_block(kb):
        k0 = pl.multiple_of(jnp.minimum(kb, seq // tk - 1) * tk, tk)
        ik = smk_ref[pl.ds(k0, tk), :][:, SM_IK:SM_IK + IDX_HEAD_DIM]
        score = jnp.zeros((tk, tq), F32)
        for h in range(0, IDX_HEADS, 2):
            qq = jnp.concatenate([iqt_ref[(h + u) * IDX_HEAD_DIM:(h + u + 1) * IDX_HEAD_DIM, :]
                                  for u in range(2)], axis=1)
            rel = jnp.maximum(_dot(ik, qq), 0.0)
            score = score + rel[:, :tq] * iwt[h:h + 1, :] + rel[:, tq:] * iwt[h + 1:h + 2, :]
        key = jnp.where(kiota + kb * tk <= qpos, _sortable(score), jnp.int32(INT_MIN))
        key_ref[kb] = key
        hi_ref[kb] = (key >> 16).astype(jnp.int16)
        lo_ref[kb] = ((key & 0xFFFF) + HALF_MIN).astype(jnp.int16)

    def idx_pair(i, carry):
        idx_block(2 * i)
        idx_block(2 * i + 1)
        return carry

    lax.fori_loop(0, npair, idx_pair, 0)

    def count(pred):
        def pair(i, c):
            for u in range(2):
                kb = 2 * i + u
                hit = jnp.where(pred(key_ref[kb], kb), 1.0, 0.0)
                c = c + jnp.sum(hit.reshape(tk // 8, 8, tq), axis=0)
            return c
        c = lax.fori_loop(0, npair, pair, jnp.zeros((8, tq), F32))
        return jnp.sum(c, axis=0, keepdims=True)

    def pack16(v):
        return jnp.broadcast_to(v, (PACK16, tq)).astype(jnp.int16)

    def count16(ref, pred):
        one, zero = jnp.int16(1), jnp.int16(0)

        def pair(i, c):
            for u in range(2):
                blk = ref[2 * i + u]
                c = c + _tree_sum([jnp.where(pred(blk[j:j + PACK16]), one, zero)
                                   for j in range(0, tk, PACK16)])
            return c
        c = lax.fori_loop(0, npair, pair, jnp.zeros((PACK16, tq), jnp.int16))
        return jnp.sum(c.astype(F32), axis=0, keepdims=True)

    def bisect16(ref, want):
        def step(it, t):
            cand = t + lax.shift_left(jnp.int32(1), jnp.int32(15) - it)
            c16 = pack16(cand)
            return jnp.where(count16(ref, lambda b: b >= c16) >= want, cand, t)
        return lax.fori_loop(0, 16, step, jnp.full((1, tq), HALF_MIN, jnp.int32))

    kf = float(topk)
    t_hi = bisect16(hi_ref, kf)
    t_hi16 = pack16(t_hi)
    n_hi_gt = count16(hi_ref, lambda b: b > t_hi16)

    def mask_low(i, carry):
        for u in range(2):
            kb = 2 * i + u
            for j in range(0, tk, PACK16):
                rows = slice(j, j + PACK16)
                lo_ref[kb, rows, :] = jnp.where(hi_ref[kb, rows, :] == t_hi16, lo_ref[kb, rows, :],
                                                jnp.int16(HALF_MIN))
        return carry

    lax.fori_loop(0, npair, mask_low, 0)
    t_lo = bisect16(lo_ref, kf - n_hi_gt)
    t_lo16 = pack16(t_lo)
    thr = t_hi * 65536 + (t_lo - HALF_MIN)

    n_gt = n_hi_gt + count16(lo_ref, lambda b: b > t_lo16)
    n_ge = n_hi_gt + count16(lo_ref, lambda b: b >= t_lo16)
    need = kf - n_gt
    tied = (n_ge > kf) & (thr > jnp.int32(INT_MIN))

    def tie_break():
        def step(it, x):
            cand = x + lax.shift_left(jnp.int32(1), jnp.int32(seq.bit_length() - 1) - it)
            below = count(lambda k, kb: (k == thr) & (kiota + kb * tk < cand))
            return jnp.where(below < need, cand, x)
        return lax.fori_loop(0, seq.bit_length(), step, jnp.zeros((1, tq), jnp.int32))

    jtie = lax.cond(jnp.max(jnp.where(tied, 1.0, 0.0)) > 0.0, tie_break,
                    lambda: jnp.zeros((1, tq), jnp.int32))
    jmax = jnp.where(tied, jtie, jnp.where(thr > jnp.int32(INT_MIN), jnp.int32(seq), jnp.int32(-1)))

    last = nkb + nkb % 2 - 1

    def scores(kb, s_ref):
        kb = jnp.minimum(kb, last)
        kc = jnp.minimum(kb, seq // tk - 1)
        k0 = pl.multiple_of(kc * tk, tk)
        key = key_ref[kb]
        sel = (key > thr) | ((key == thr) & (kiota + kb * tk <= jmax))
        bias = jnp.where(sel, 0.0, NEG_INF)
        bias = jnp.concatenate([bias] * ATTN_REP, axis=1)
        for g in range(ATTN_KV_HEADS):
            kg = ak_ref[pl.ds(k0, tk), g * ATTN_HEAD_DIM:(g + 1) * ATTN_HEAD_DIM]
            qt = jnp.concatenate(
                [aqt_ref[(g * ATTN_REP + r) * ATTN_HEAD_DIM:(g * ATTN_REP + r + 1) * ATTN_HEAD_DIM, :]
                 for r in range(ATTN_REP)], axis=1)
            s_ref[g] = _dot(kg, qt) + bias

    def softmax(s_ref, p_ref):
        for g in range(ATTN_KV_HEADS):
            for r in range(ATTN_REP):
                cols = slice(r * tq, (r + 1) * tq)
                s = s_ref[g, :, cols]
                m_old = m_ref[g, :, cols]
                m_new = jnp.maximum(m_old, jnp.max(s, axis=0, keepdims=True))
                p_ref[g, :, cols] = jnp.exp2(s - m_new).astype(BF16)
                al_ref[g, :, cols] = jnp.exp2(m_old - m_new)
                m_ref[g, :, cols] = m_new

    def values(kb, p_ref):
        kc = jnp.clip(kb, 0, seq // tk - 1)
        for g in range(ATTN_KV_HEADS):
            vt = avt_ref[kc, g * VT_ROWS:(g + 1) * VT_ROWS, :]
            acc_ref[g] = al_ref[g] * acc_ref[g] + _dot(vt, p_ref[g])

    m_ref[...] = jnp.full_like(m_ref, NEG_INF)
    acc_ref[...] = jnp.zeros_like(acc_ref)
    al_ref[...] = jnp.ones_like(al_ref)
    p1_ref[...] = jnp.zeros_like(p1_ref)
    scores(0, s0_ref)

    def att_pair(i, carry):
        j = 2 * i
        values(j - 1, p1_ref)
        softmax(s0_ref, p0_ref)
        scores(j + 1, s1_ref)
        values(j, p0_ref)
        softmax(s1_ref, p1_ref)
        scores(j + 2, s0_ref)
        return carry

    lax.fori_loop(0, npair, att_pair, 0)
    values(last, p1_ref)
    for g in range(ATTN_KV_HEADS):
        acc = acc_ref[g]
        ot = acc[:ATTN_HEAD_DIM] / acc[ATTN_HEAD_DIM:ATTN_HEAD_DIM + 1]
        for r in range(ATTN_REP):
            hh = g * ATTN_REP + r
            o_ref[:, hh * ATTN_HEAD_DIM:(hh + 1) * ATTN_HEAD_DIM] = ot[:, r * tq:(r + 1) * tq].T.astype(BF16)


def _dsa(aq_t, iq_t, sm_t, ak_r, sm_r, av_t, batch, seq):
    n = ak_r.shape[0]
    tq, tk = DSA_TQ, DSA_TK
    nq = seq // tq
    nkb = seq // tk
    topk = min(TOPK_MAX, seq // 4)
    qmap = lambda b, i: (b * nq + i, 0, 0)
    return pl.pallas_call(
        functools.partial(_dsa_body, topk=topk, seq=seq),
        grid=(batch, nq),
        in_specs=[
            pl.BlockSpec((None, IDX_Q, tq), qmap),
            pl.BlockSpec((None, LANES, tq), qmap),
            pl.BlockSpec((None, ATTN_Q, tq), qmap),
            pl.BlockSpec((seq, LANES), lambda b, i: (b, 0)),
            pl.BlockSpec((seq, ATTN_KV), lambda b, i: (b, 0)),
            pl.BlockSpec((nkb, ATTN_KV_HEADS * VT_ROWS, tk), lambda b, i: (b, 0, 0)),
        ],
        out_specs=pl.BlockSpec((tq, ATTN_Q), lambda b, i: (b * nq + i, 0)),
        out_shape=jax.ShapeDtypeStruct((n, ATTN_Q), BF16),
        scratch_shapes=[
            pltpu.VMEM((nkb + nkb % 2, tk, tq), jnp.int32),
            pltpu.VMEM((nkb + nkb % 2, tk, tq), jnp.int16),
            pltpu.VMEM((nkb + nkb % 2, tk, tq), jnp.int16),
            pltpu.VMEM((ATTN_KV_HEADS, 1, ATTN_REP * tq), F32),
            pltpu.VMEM((ATTN_KV_HEADS, 1, ATTN_REP * tq), F32),
            pltpu.VMEM((ATTN_KV_HEADS, VT_ROWS, ATTN_REP * tq), F32),
            pltpu.VMEM((ATTN_KV_HEADS, tk, ATTN_REP * tq), F32),
            pltpu.VMEM((ATTN_KV_HEADS, tk, ATTN_REP * tq), F32),
            pltpu.VMEM((ATTN_KV_HEADS, tk, ATTN_REP * tq), BF16),
            pltpu.VMEM((ATTN_KV_HEADS, tk, ATTN_REP * tq), BF16),
        ],
        compiler_params=_cparams(("parallel", "arbitrary")),
        name="dsa",
    )(iq_t, sm_t, aq_t, sm_r, ak_r, av_t)


GLA_ROWS = 256


def _split3(x):
    hi = x.astype(BF16)
    r1 = x - hi.astype(F32)
    mid = r1.astype(BF16)
    lo = (r1 - mid.astype(F32)).astype(BF16)
    return hi, mid, lo


def _gla_body(q_ref, k_ref, v_ref, r_ref, sm_ref, wa_ref, ba_ref, gn_ref, o_ref, st_ref):
    rows, c = GLA_ROWS, GLA_CHUNK

    @pl.when(pl.program_id(1) == 0)
    def _():
        st_ref[...] = jnp.zeros_like(st_ref)

    ga = sm_ref[:, SM_GA:SM_GA + GLA_GATE_RANK]
    z = _dot(ga, wa_ref[...]) + ba_ref[...]
    g = -(jnp.maximum(-z, 0.0) + jnp.log1p(jnp.exp(-jnp.abs(z)))) * (1.0 / GLA_GATE_NORM)
    ri = lax.broadcasted_iota(jnp.int32, (rows, rows), 0)
    ci = lax.broadcasted_iota(jnp.int32, (rows, rows), 1)
    tri = jnp.where((ri // c == ci // c) & (ci <= ri), 1.0, 0.0).astype(BF16)
    hi, mid, lo = _split3(g)
    b = _dot(tri, hi) + _dot(tri, mid) + _dot(tri, lo)
    low = lax.broadcasted_iota(jnp.int32, (c, c), 1) <= lax.broadcasted_iota(jnp.int32, (c, c), 0)
    gn = gn_ref[...]
    for ch in range(rows // c):
        sl = slice(ch * c, (ch + 1) * c)
        for h in range(GLA_HEADS):
            ks = slice(h * GLA_DK, (h + 1) * GLA_DK)
            vs = slice(h * GLA_DV, (h + 1) * GLA_DV)
            bc = b[sl, ks]
            bl = bc[c - 1:c, :]
            q = q_ref[sl, ks].astype(F32) * (GLA_DK ** -0.5)
            k = k_ref[sl, ks].astype(F32)
            v = v_ref[sl, vs]
            qe = (q * jnp.exp(bc - bl)).astype(BF16)
            kd = (k * jnp.exp(bl - bc)).astype(BF16)
            qb = (q * jnp.exp(bc)).astype(BF16)
            a = jnp.where(low, _dot_nt(qe, kd), 0.0).astype(BF16)
            st = st_ref[h]
            o = _dot(a, v) + _dot_nt(qb, st.astype(BF16))
            vt = v.astype(F32).T.astype(BF16)
            st_ref[h] = st * jnp.exp(bl) + _dot(vt, kd)
            o = _rms(o, gn)
            gr = r_ref[sl, vs].astype(F32)
            o_ref[sl, vs] = (o * (gr * jax.nn.sigmoid(gr))).astype(BF16)


def _gla(proj, wa2, ba, gn, batch, seq):
    n = proj.shape[0]
    rows = GLA_ROWS
    ng = seq // rows
    rmap = lambda off, w: (lambda b, i: (b * ng + i, off // w))
    return pl.pallas_call(
        _gla_body,
        grid=(batch, ng),
        in_specs=[
            pl.BlockSpec((rows, GLA_K), rmap(COL_GQ, GLA_K)),
            pl.BlockSpec((rows, GLA_K), rmap(COL_GK, GLA_K)),
            pl.BlockSpec((rows, GLA_V), rmap(COL_GV, GLA_V)),
            pl.BlockSpec((rows, GLA_V), rmap(COL_GR, GLA_V)),
            pl.BlockSpec((rows, LANES), rmap(COL_SM, LANES)),
            pl.BlockSpec((GLA_GATE_RANK, GLA_K), lambda b, i: (0, 0)),
            pl.BlockSpec((1, GLA_K), lambda b, i: (0, 0)),
            pl.BlockSpec((1, GLA_DV), lambda b, i: (0, 0)),
        ],
        out_specs=pl.BlockSpec((rows, GLA_V), lambda b, i: (b * ng + i, 0)),
        out_shape=jax.ShapeDtypeStruct((n, GLA_V), BF16),
        scratch_shapes=[pltpu.VMEM((GLA_HEADS, GLA_DV, GLA_DK), F32)],
        compiler_params=_cparams(("parallel", "arbitrary")),
        name="gla",
    )(proj, proj, proj, proj, proj, wa2, ba, gn)


def _merge_body(x_ref, oa_ref, og_ref, ma_ref, mg_ref, wua_ref, wug_ref, wo_ref, o_ref):
    ya = jax.nn.sigmoid(ma_ref[...].astype(F32)) * _dot(oa_ref[...], wua_ref[...])
    yg = jax.nn.sigmoid(mg_ref[...].astype(F32)) * _dot(og_ref[...], wug_ref[...])
    y = (ya + yg).astype(BF16)
    o_ref[...] = x_ref[...] + _dot(y, wo_ref[...])


def _merge(x, o_attn, o_gla, proj, wua, wug, wo, layer, *, tm=256):
    n, d = x.shape
    const = lambda i: (layer, 0, 0)
    return pl.pallas_call(
        _merge_body,
        grid=(n // tm,),
        in_specs=[
            pl.BlockSpec((tm, d), lambda i: (i, 0)),
            pl.BlockSpec((tm, ATTN_Q), lambda i: (i, 0)),
            pl.BlockSpec((tm, GLA_V), lambda i: (i, 0)),
            pl.BlockSpec((tm, d), lambda i: (i, COL_MA // D_MODEL)),
            pl.BlockSpec((tm, d), lambda i: (i, COL_MG // D_MODEL)),
            pl.BlockSpec((None, ATTN_Q, d), const),
            pl.BlockSpec((None, GLA_V, d), const),
            pl.BlockSpec((None, d, d), const),
        ],
        out_specs=pl.BlockSpec((tm, d), lambda i: (i, 0)),
        out_shape=jax.ShapeDtypeStruct((n, d), F32),
        compiler_params=_cparams(("parallel",)),
        name="merge",
    )(x, o_attn, o_gla, proj, proj, wua, wug, wo)


def kernel(x, ffn1_norm, ffn1_w_gate, ffn1_w_up, ffn1_w_down, mix_norm, w_in, gla_w_a2, gla_b_a, gla_norm, w_up_attn, w_up_gla, w_out, ffn2_norm, ffn2_w_gate, ffn2_w_up, ffn2_w_down, final_norm):
    batch, seq, d = x.shape
    depth = w_in.shape[0]
    tabs_a = _rope_tables(seq, ATTN_HEAD_DIM, ATTN_ROT)
    tabs_i = _rope_tables(seq, IDX_HEAD_DIM, IDX_ROT)
    fg = final_norm.reshape(1, d)
    g1, gm, g2 = (a.reshape(depth, 1, d) for a in (ffn1_norm, mix_norm, ffn2_norm))
    ffn1 = (_to_bf16(ffn1_w_gate, 256), _to_bf16(ffn1_w_up, 256), _to_bf16(ffn1_w_down, 512))
    ffn2 = (_to_bf16(ffn2_w_gate, 256), _to_bf16(ffn2_w_up, 256), _to_bf16(ffn2_w_down, 512))
    w_in_b = _prep_w_in(w_in)
    wua, wug, wo = _to_bf16(w_up_attn, 512), _to_bf16(w_up_gla, 512), _to_bf16(w_out, 512)
    h = x.reshape(batch * seq, d)
    for l in range(depth):
        h = _ffn(h, g1, *ffn1, fg, l, final_norm=False)
        proj = _proj(h, gm, w_in_b, l)
        aq_t, iq_t, sm_t, ak_r, sm_r, av_t = _rope(proj, seq, tabs_a, tabs_i)
        o_attn = _dsa(aq_t, iq_t, sm_t, ak_r, sm_r, av_t, batch, seq)
        o_gla = _gla(proj, gla_w_a2[l].astype(BF16), gla_b_a[l].reshape(1, GLA_K),
                     gla_norm[l].reshape(1, GLA_DV), batch, seq)
        h = _merge(h, o_attn, o_gla, proj, wua, wug, wo, l)
        h = _ffn(h, g2, *ffn2, fg, l, final_norm=(l == depth - 1))
    return h.reshape(batch, seq, d)
```

```python
import functools

import jax
import jax.numpy as jnp
import numpy as np
from jax import lax
from jax.experimental import pallas as pl
from jax.experimental.pallas import tpu as pltpu

F32 = jnp.float32
BF16 = jnp.bfloat16

D_MODEL = 2048
D_FF = 5632
FFN_RES = 0.5
ATTN_HEADS = 8
ATTN_KV_HEADS = 2
ATTN_HEAD_DIM = 128
ATTN_REP = ATTN_HEADS // ATTN_KV_HEADS
IDX_HEADS = 16
IDX_HEAD_DIM = 64
TOPK_MAX = 256
GLA_HEADS = 4
GLA_DK = 128
GLA_DV = 256
GLA_GATE_RANK = 16
GLA_GATE_NORM = 16.0
GLA_CHUNK = 64
ROPE_THETA = 500000.0
ATTN_ROT = ATTN_HEAD_DIM // 4
IDX_ROT = IDX_HEAD_DIM // 4
EPS = 1e-6
NEG_INF = -1e30
INT_MIN = -(2 ** 31)
PACK16 = 16

ATTN_Q = ATTN_HEADS * ATTN_HEAD_DIM
ATTN_KV = ATTN_KV_HEADS * ATTN_HEAD_DIM
IDX_Q = IDX_HEADS * IDX_HEAD_DIM
GLA_K = GLA_HEADS * GLA_DK
GLA_V = GLA_HEADS * GLA_DV
IN_SPLITS = (ATTN_Q, ATTN_KV, ATTN_KV, IDX_Q, IDX_HEAD_DIM, IDX_HEADS,
             GLA_K, GLA_K, GLA_V, GLA_GATE_RANK, GLA_V, D_MODEL, D_MODEL)
IN_OFFSETS = tuple(int(v) for v in np.cumsum(IN_SPLITS)[:-1])
IN_NAMES = ("aq", "ak", "av", "iq", "ik", "iw", "gq", "gk", "gv", "ga", "gr", "m_a", "m_g")

LANES = 128
COL_AQ = 0
COL_IQ = COL_AQ + ATTN_Q
COL_GV = COL_IQ + IDX_Q
COL_GR = COL_GV + GLA_V
COL_MA = COL_GR + GLA_V
COL_MG = COL_MA + D_MODEL
COL_GQ = COL_MG + D_MODEL
COL_GK = COL_GQ + GLA_K
COL_AK = COL_GK + GLA_K
COL_AV = COL_AK + ATTN_KV
COL_SM = COL_AV + ATTN_KV
SM_IK = 0
SM_IW = SM_IK + IDX_HEAD_DIM
SM_GA = SM_IW + IDX_HEADS
PROJ_TN = 1024
PROJ_COLS = -(-(COL_SM + LANES) // PROJ_TN) * PROJ_TN

DSA_TQ = 128
DSA_TK = 256
VT_ROWS = ATTN_HEAD_DIM + PACK16
LOG2E = 1.4426950408889634

FFN_SLAB = 512

VMEM_LIMIT = 62 * 1024 * 1024


def _cparams(sem):
    return pltpu.CompilerParams(dimension_semantics=sem, vmem_limit_bytes=VMEM_LIMIT)


def _dot(a, b):
    return jnp.dot(a, b, preferred_element_type=F32)


def _dot_nt(a, b):
    return lax.dot_general(a, b, (((1,), (1,)), ((), ())), preferred_element_type=F32)


def _rms(x, g):
    return x * lax.rsqrt(jnp.mean(x * x, axis=-1, keepdims=True) + EPS) * g


def _ffn_body(x_ref, g_ref, wg_ref, wu_ref, wd_ref, fg_ref, o_ref, h_ref, *, final_norm):
    f = pl.program_id(1)

    slabs = [slice(r, r + FFN_SLAB) for r in range(0, x_ref.shape[0], FFN_SLAB)]

    @pl.when(f == 0)
    def _():
        for rows in slabs:
            h_ref[rows, :] = _rms(x_ref[rows, :], g_ref[...]).astype(BF16)
        o_ref[...] = jnp.zeros_like(o_ref)

    for rows in slabs:
        h = h_ref[rows, :]
        gate = _dot(h, wg_ref[...])
        up = _dot(h, wu_ref[...])
        act = (gate * jax.nn.sigmoid(gate) * up).astype(BF16)
        o_ref[rows, :] += _dot(act, wd_ref[...])

    @pl.when(f == pl.num_programs(1) - 1)
    def _():
        for rows in slabs:
            y = x_ref[rows, :] + FFN_RES * o_ref[rows, :]
            if final_norm:
                y = _rms(y, fg_ref[...])
            o_ref[rows, :] = y


def _ffn(x, g, wg, wu, wd, fg, layer, *, final_norm, tm=1024, tf=512):
    n, d = x.shape
    f = wg.shape[2]
    return pl.pallas_call(
        functools.partial(_ffn_body, final_norm=final_norm),
        grid=(n // tm, f // tf),
        in_specs=[
            pl.BlockSpec((tm, d), lambda i, j: (i, 0)),
            pl.BlockSpec((None, 1, d), lambda i, j: (layer, 0, 0)),
            pl.BlockSpec((None, d, tf), lambda i, j: (layer, 0, j)),
            pl.BlockSpec((None, d, tf), lambda i, j: (layer, 0, j)),
            pl.BlockSpec((None, tf, d), lambda i, j: (layer, j, 0)),
            pl.BlockSpec((1, d), lambda i, j: (0, 0)),
        ],
        out_specs=pl.BlockSpec((tm, d), lambda i, j: (i, 0)),
        out_shape=jax.ShapeDtypeStruct((n, d), F32),
        scratch_shapes=[pltpu.VMEM((tm, d), BF16)],
        compiler_params=_cparams(("parallel", "arbitrary")),
        name="ffn",
    )(x, g, wg, wu, wd, fg)


def _cast_body(w_ref, o_ref):
    o_ref[...] = w_ref[...].astype(o_ref.dtype)


def _to_bf16(w, tr):
    depth, r, c = w.shape
    spec = pl.BlockSpec((None, tr, c), lambda l, i: (l, i, 0))
    return pl.pallas_call(
        _cast_body,
        grid=(depth, r // tr),
        in_specs=[spec],
        out_specs=spec,
        out_shape=jax.ShapeDtypeStruct(w.shape, BF16),
        compiler_params=_cparams(("parallel", "parallel")),
        name="cast",
    )(w)


def _w_in_body(w_ref, o_ref):
    w = w_ref[...]
    seg = {name: w[:, off:off + width] for name, off, width in
           zip(IN_NAMES, (0,) + IN_OFFSETS, IN_SPLITS)}
    tr = w.shape[0]
    small_pad = jnp.zeros((tr, LANES - IDX_HEAD_DIM - IDX_HEADS - GLA_GATE_RANK), w.dtype)
    tail = jnp.zeros((tr, PROJ_COLS - COL_SM - LANES), w.dtype)
    order = ("aq", "iq", "gv", "gr", "m_a", "m_g", "gq", "gk", "ak", "av", "ik", "iw", "ga")
    o_ref[...] = jnp.concatenate([seg[k] for k in order] + [small_pad, tail], axis=1).astype(o_ref.dtype)


def _prep_w_in(w, tr=256):
    depth, d, c = w.shape
    return pl.pallas_call(
        _w_in_body,
        grid=(depth, d // tr),
        in_specs=[pl.BlockSpec((None, tr, c), lambda l, i: (l, i, 0))],
        out_specs=pl.BlockSpec((None, tr, PROJ_COLS), lambda l, i: (l, i, 0)),
        out_shape=jax.ShapeDtypeStruct((depth, d, PROJ_COLS), BF16),
        compiler_params=_cparams(("parallel", "parallel")),
        name="w_in_prep",
    )(w)


def _proj_body(x_ref, g_ref, w_ref, o_ref, h_ref):
    @pl.when(pl.program_id(1) == 0)
    def _():
        h_ref[...] = _rms(x_ref[...], g_ref[...]).astype(BF16)

    o_ref[...] = _dot(h_ref[...], w_ref[...]).astype(o_ref.dtype)


def _proj(x, g, w, layer, *, tm=1024, tn=PROJ_TN):
    n, d = x.shape
    c = w.shape[2]
    return pl.pallas_call(
        _proj_body,
        grid=(n // tm, c // tn),
        in_specs=[
            pl.BlockSpec((tm, d), lambda i, j: (i, 0)),
            pl.BlockSpec((None, 1, d), lambda i, j: (layer, 0, 0)),
            pl.BlockSpec((None, d, tn), lambda i, j: (layer, 0, j)),
        ],
        out_specs=pl.BlockSpec((tm, tn), lambda i, j: (i, j)),
        out_shape=jax.ShapeDtypeStruct((n, c), BF16),
        scratch_shapes=[pltpu.VMEM((tm, d), BF16)],
        compiler_params=_cparams(("parallel", "arbitrary")),
        name="proj",
    )(x, g, w)


def _rope_tables(length, head_dim, rot):
    half = rot // 2
    inv = ROPE_THETA ** (-jnp.arange(0, rot, 2, dtype=F32) / rot)
    ang = jnp.arange(length, dtype=F32)[:, None] * inv[None, :]
    cos, sin = jnp.cos(ang), jnp.sin(ang)
    zeros = jnp.zeros((length, head_dim - rot), F32)
    zh = jnp.zeros((length, half), F32)
    c = jnp.concatenate([cos, cos, zeros + 1.0], axis=1)
    s1 = jnp.concatenate([-sin, zh, zeros], axis=1)
    s2 = jnp.concatenate([zh, sin, zeros], axis=1)
    reps = LANES // head_dim
    return tuple(jnp.tile(t, (1, reps)) for t in (c, s1, s2))


def _rope_lanes(x, c, s1, s2, half):
    return x * c + pltpu.roll(x, LANES - half, 1) * s1 + pltpu.roll(x, half, 1) * s2


def _store_t(dst_ref, rows, val):
    vt = val.T.astype(dst_ref.dtype)
    for c in range(vt.shape[1] // DSA_TQ):
        dst_ref[c, rows, :] = vt[:, c * DSA_TQ:(c + 1) * DSA_TQ]


def _rope_body(aq_ref, iq_ref, ak_ref, av_ref, sm_ref, ca_ref, sa1_ref, sa2_ref, ci_ref, si1_ref, si2_ref,
               aqt_ref, iqt_ref, smt_ref, ako_ref, smo_ref, avt_ref):
    ca, sa1, sa2 = ca_ref[...], sa1_ref[...], sa2_ref[...]
    ci, si1, si2 = ci_ref[...], si1_ref[...], si2_ref[...]
    ha, hi = ATTN_ROT // 2, IDX_ROT // 2
    tk = DSA_TK
    for j in range(ATTN_Q // LANES):
        sl = slice(j * LANES, (j + 1) * LANES)
        r = _rope_lanes(aq_ref[:, sl].astype(F32), ca, sa1, sa2, ha)
        _store_t(aqt_ref, sl, r * (ATTN_HEAD_DIM ** -0.5 * LOG2E))
    for j in range(IDX_Q // LANES):
        sl = slice(j * LANES, (j + 1) * LANES)
        r = _rope_lanes(iq_ref[:, sl].astype(F32), ci, si1, si2, hi)
        _store_t(iqt_ref, sl, r * (IDX_HEAD_DIM ** -0.5))
    for j in range(ATTN_KV // LANES):
        sl = slice(j * LANES, (j + 1) * LANES)
        ako_ref[:, sl] = _rope_lanes(ak_ref[:, sl].astype(F32), ca, sa1, sa2, ha).astype(BF16)
        vt = av_ref[:, sl].astype(F32).T
        for c in range(vt.shape[1] // tk):
            avt_ref[c, j * VT_ROWS:j * VT_ROWS + LANES, :] = vt[:, c * tk:(c + 1) * tk].astype(BF16)
            avt_ref[c, j * VT_ROWS + LANES:(j + 1) * VT_ROWS, :] = jnp.ones((VT_ROWS - LANES, tk), BF16)
    sm = sm_ref[...].astype(F32)
    lane = lax.broadcasted_iota(jnp.int32, sm.shape, 1)
    sm = jnp.where(lane < IDX_HEAD_DIM, _rope_lanes(sm, ci, si1, si2, hi), sm)
    smo_ref[...] = sm.astype(BF16)
    _store_t(smt_ref, slice(0, LANES), sm)


def _rope(proj, seq, tabs_a, tabs_i, *, tr=512):
    n = proj.shape[0]
    nl = seq // tr
    tk = DSA_TK
    tab = pl.BlockSpec((tr, LANES), lambda i: (i % nl, 0))
    return pl.pallas_call(
        _rope_body,
        grid=(n // tr,),
        in_specs=[
            pl.BlockSpec((tr, ATTN_Q), lambda i: (i, COL_AQ // ATTN_Q)),
            pl.BlockSpec((tr, IDX_Q), lambda i: (i, COL_IQ // IDX_Q)),
            pl.BlockSpec((tr, ATTN_KV), lambda i: (i, COL_AK // ATTN_KV)),
            pl.BlockSpec((tr, ATTN_KV), lambda i: (i, COL_AV // ATTN_KV)),
            pl.BlockSpec((tr, LANES), lambda i: (i, COL_SM // LANES)),
            tab, tab, tab, tab, tab, tab,
        ],
        out_specs=[
            pl.BlockSpec((tr // DSA_TQ, ATTN_Q, DSA_TQ), lambda i: (i, 0, 0)),
            pl.BlockSpec((tr // DSA_TQ, IDX_Q, DSA_TQ), lambda i: (i, 0, 0)),
            pl.BlockSpec((tr // DSA_TQ, LANES, DSA_TQ), lambda i: (i, 0, 0)),
            pl.BlockSpec((tr, ATTN_KV), lambda i: (i, 0)),
            pl.BlockSpec((tr, LANES), lambda i: (i, 0)),
            pl.BlockSpec((tr // tk, ATTN_KV_HEADS * VT_ROWS, tk), lambda i: (i, 0, 0)),
        ],
        out_shape=[
            jax.ShapeDtypeStruct((n // DSA_TQ, ATTN_Q, DSA_TQ), BF16),
            jax.ShapeDtypeStruct((n // DSA_TQ, IDX_Q, DSA_TQ), BF16),
            jax.ShapeDtypeStruct((n // DSA_TQ, LANES, DSA_TQ), BF16),
            jax.ShapeDtypeStruct((n, ATTN_KV), BF16),
            jax.ShapeDtypeStruct((n, LANES), BF16),
            jax.ShapeDtypeStruct((n // tk, ATTN_KV_HEADS * VT_ROWS, tk), BF16),
        ],
        compiler_params=_cparams(("parallel",)),
        name="rope",
    )(proj, proj, proj, proj, proj, *tabs_a, *tabs_i)


def _sortable(x):
    i = pltpu.bitcast(x + 0.0, jnp.int32)
    return jnp.where(i < 0, i ^ jnp.int32(0x7FFFFFFF), i)


def _tree_sum(xs):
    while len(xs) > 1:
        xs = [xs[i] + xs[i + 1] for i in range(0, len(xs) - 1, 2)] + ([xs[-1]] if len(xs) % 2 else [])
    return xs[0]


def _colsum(x):
    r, c = x.shape
    return jnp.sum(jnp.sum(x.reshape(r // 8, 8, c), axis=0), axis=0, keepdims=True)


def _dsa_body(iqt_ref, smt_ref, aqt_ref, smk_ref, ak_ref, avt_ref, o_ref,
              key_ref, m_ref, al_ref, acc_ref, s0_ref, s1_ref, p0_ref, p1_ref, *, topk, seq):
    tq, tk = DSA_TQ, DSA_TK
    qi = pl.program_id(1)
    t0 = qi * tq
    nkb = (t0 + tq + tk - 1) // tk
    iwt = smt_ref[SM_IW:SM_IW + IDX_HEADS, :].astype(F32) * (IDX_HEADS ** -0.5)
    qpos = lax.broadcasted_iota(jnp.int32, (tk, tq), 1) + t0
    kiota = lax.broadcasted_iota(jnp.int32, (tk, tq), 0)

    npair = (nkb + 1) // 2

    def idx_block(kb):
        k0 = pl.multiple_of(jnp.minimum(kb, seq // tk - 1) * tk, tk)
        ik = smk_ref[pl.ds(k0, tk), :][:, SM_IK:SM_IK + IDX_HEAD_DIM]
        score = jnp.zeros((tk, tq), F32)
        for h in range(0, IDX_HEADS, 2):
            qq = jnp.concatenate([iqt_ref[(h + u) * IDX_HEAD_DIM:(h + u + 1) * IDX_HEAD_DIM, :]
                                  for u in range(2)], axis=1)
            rel = jnp.maximum(_dot(ik, qq), 0.0)
            score = score + rel[:, :tq] * iwt[h:h + 1, :] + rel[:, tq:] * iwt[h + 1:h + 2, :]
        key_ref[kb] = jnp.where(kiota + kb * tk <= qpos, _sortable(score), jnp.int32(INT_MIN))

    def idx_pair(i, carry):
        idx_block(2 * i)
        idx_block(2 * i + 1)
        return carry

    lax.fori_loop(0, npair, idx_pair, 0)

    def count(pred):
        def pair(i, c):
            for u in range(2):
                kb = 2 * i + u
                hit = jnp.where(pred(key_ref[kb], kb), 1, 0)
                c = c + _tree_sum([hit[j:j + 8] for j in range(0, tk, 8)])
            return c
        c = lax.fori_loop(0, npair, pair, jnp.zeros((8, tq), jnp.int32))
        return jnp.sum(c.astype(F32), axis=0, keepdims=True)

    kf = float(topk)

    def bisect(it, t):
        cand = t + lax.shift_left(jnp.int32(1), jnp.int32(31) - it)
        return jnp.where(count(lambda k, kb: k >= cand) >= kf, cand, t)

    thr = lax.fori_loop(0, 32, bisect, jnp.full((1, tq), INT_MIN, jnp.int32))

    n_gt = count(lambda k, kb: k > thr)
    n_ge = count(lambda k, kb: k >= thr)
    need = kf - n_gt
    tied = (n_ge > kf) & (thr > jnp.int32(INT_MIN))

    def tie_break():
        def step(it, x):
            cand = x + lax.shift_left(jnp.int32(1), jnp.int32(seq.bit_length() - 1) - it)
            below = count(lambda k, kb: (k == thr) & (kiota + kb * tk < cand))
            return jnp.where(below < need, cand, x)
        return lax.fori_loop(0, seq.bit_length(), step, jnp.zeros((1, tq), jnp.int32))

    jtie = lax.cond(jnp.max(jnp.where(tied, 1.0, 0.0)) > 0.0, tie_break,
                    lambda: jnp.zeros((1, tq), jnp.int32))
    jmax = jnp.where(tied, jtie, jnp.where(thr > jnp.int32(INT_MIN), jnp.int32(seq), jnp.int32(-1)))

    last = nkb + nkb % 2 - 1

    def scores(kb, s_ref):
        kb = jnp.minimum(kb, last)
        kc = jnp.minimum(kb, seq // tk - 1)
        k0 = pl.multiple_of(kc * tk, tk)
        key = key_ref[kb]
        sel = (key > thr) | ((key == thr) & (kiota + kb * tk <= jmax))
        bias = jnp.where(sel, 0.0, NEG_INF)
        bias = jnp.concatenate([bias] * ATTN_REP, axis=1)
        for g in range(ATTN_KV_HEADS):
            kg = ak_ref[pl.ds(k0, tk), g * ATTN_HEAD_DIM:(g + 1) * ATTN_HEAD_DIM]
            qt = jnp.concatenate(
                [aqt_ref[(g * ATTN_REP + r) * ATTN_HEAD_DIM:(g * ATTN_REP + r + 1) * ATTN_HEAD_DIM, :]
                 for r in range(ATTN_REP)], axis=1)
            s_ref[g] = _dot(kg, qt) + bias

    def softmax(s_ref, p_ref):
        for g in range(ATTN_KV_HEADS):
            for r in range(ATTN_REP):
                cols = slice(r * tq, (r + 1) * tq)
                s = s_ref[g, :, cols]
                m_old = m_ref[g, :, cols]
                m_new = jnp.maximum(m_old, jnp.max(s, axis=0, keepdims=True))
                p_ref[g, :, cols] = jnp.exp2(s - m_new).astype(BF16)
                al_ref[g, :, cols] = jnp.exp2(m_old - m_new)
                m_ref[g, :, cols] = m_new

    def values(kb, p_ref):
        kc = jnp.clip(kb, 0, seq // tk - 1)
        for g in range(ATTN_KV_HEADS):
            vt = avt_ref[kc, g * VT_ROWS:(g + 1) * VT_ROWS, :]
            acc_ref[g] = al_ref[g] * acc_ref[g] + _dot(vt, p_ref[g])

    m_ref[...] = jnp.full_like(m_ref, NEG_INF)
    acc_ref[...] = jnp.zeros_like(acc_ref)
    al_ref[...] = jnp.ones_like(al_ref)
    p1_ref[...] = jnp.zeros_like(p1_ref)
    scores(0, s0_ref)

    def att_pair(i, carry):
        j = 2 * i
        values(j - 1, p1_ref)
        softmax(s0_ref, p0_ref)
        scores(j + 1, s1_ref)
        values(j, p0_ref)
        softmax(s1_ref, p1_ref)
        scores(j + 2, s0_ref)
        return carry

    lax.fori_loop(0, npair, att_pair, 0)
    values(last, p1_ref)
    for g in range(ATTN_KV_HEADS):
        acc = acc_ref[g]
        ot = acc[:ATTN_HEAD_DIM] / acc[ATTN_HEAD_DIM:ATTN_HEAD_DIM + 1]
        for r in range(ATTN_REP):
            hh = g * ATTN_REP + r
            o_ref[:, hh * ATTN_HEAD_DIM:(hh + 1) * ATTN_HEAD_DIM] = ot[:, r * tq:(r + 1) * tq].T.astype(BF16)


def _dsa(aq_t, iq_t, sm_t, ak_r, sm_r, av_t, batch, seq):
    n = ak_r.shape[0]
    tq, tk = DSA_TQ, DSA_TK
    nq = seq // tq
    nkb = seq // tk
    topk = min(TOPK_MAX, seq // 4)
    qmap = lambda b, i: (b * nq + i, 0, 0)
    return pl.pallas_call(
        functools.partial(_dsa_body, topk=topk, seq=seq),
        grid=(batch, nq),
        in_specs=[
            pl.BlockSpec((None, IDX_Q, tq), qmap),
            pl.BlockSpec((None, LANES, tq), qmap),
            pl.BlockSpec((None, ATTN_Q, tq), qmap),
            pl.BlockSpec((seq, LANES), lambda b, i: (b, 0)),
            pl.BlockSpec((seq, ATTN_KV), lambda b, i: (b, 0)),
            pl.BlockSpec((nkb, ATTN_KV_HEADS * VT_ROWS, tk), lambda b, i: (b, 0, 0)),
        ],
        out_specs=pl.BlockSpec((tq, ATTN_Q), lambda b, i: (b * nq + i, 0)),
        out_shape=jax.ShapeDtypeStruct((n, ATTN_Q), BF16),
        scratch_shapes=[
            pltpu.VMEM((nkb + nkb % 2, tk, tq), jnp.int32),
            pltpu.VMEM((ATTN_KV_HEADS, 1, ATTN_REP * tq), F32),
            pltpu.VMEM((ATTN_KV_HEADS, 1, ATTN_REP * tq), F32),
            pltpu.VMEM((ATTN_KV_HEADS, VT_ROWS, ATTN_REP * tq), F32),
            pltpu.VMEM((ATTN_KV_HEADS, tk, ATTN_REP * tq), F32),
            pltpu.VMEM((ATTN_KV_HEADS, tk, ATTN_REP * tq), F32),
            pltpu.VMEM((ATTN_KV_HEADS, tk, ATTN_REP * tq), BF16),
            pltpu.VMEM((ATTN_KV_HEADS, tk, ATTN_REP * tq), BF16),
        ],
        compiler_params=_cparams(("parallel", "arbitrary")),
        name="dsa",
    )(iq_t, sm_t, aq_t, sm_r, ak_r, av_t)


GLA_ROWS = 256


def _split3(x):
    hi = x.astype(BF16)
    r1 = x - hi.astype(F32)
    mid = r1.astype(BF16)
    lo = (r1 - mid.astype(F32)).astype(BF16)
    return hi, mid, lo


def _gla_body(q_ref, k_ref, v_ref, r_ref, sm_ref, wa_ref, ba_ref, gn_ref, o_ref, st_ref):
    rows, c = GLA_ROWS, GLA_CHUNK

    @pl.when(pl.program_id(1) == 0)
    def _():
        st_ref[...] = jnp.zeros_like(st_ref)

    ga = sm_ref[:, SM_GA:SM_GA + GLA_GATE_RANK]
    z = _dot(ga, wa_ref[...]) + ba_ref[...]
    g = -(jnp.maximum(-z, 0.0) + jnp.log1p(jnp.exp(-jnp.abs(z)))) * (1.0 / GLA_GATE_NORM)
    ri = lax.broadcasted_iota(jnp.int32, (rows, rows), 0)
    ci = lax.broadcasted_iota(jnp.int32, (rows, rows), 1)
    tri = jnp.where((ri // c == ci // c) & (ci <= ri), 1.0, 0.0).astype(BF16)
    hi, mid, lo = _split3(g)
    b = _dot(tri, hi) + _dot(tri, mid) + _dot(tri, lo)
    low = lax.broadcasted_iota(jnp.int32, (c, c), 1) <= lax.broadcasted_iota(jnp.int32, (c, c), 0)
    gn = gn_ref[...]
    for ch in range(rows // c):
        sl = slice(ch * c, (ch + 1) * c)
        for h in range(GLA_HEADS):
            ks = slice(h * GLA_DK, (h + 1) * GLA_DK)
            vs = slice(h * GLA_DV, (h + 1) * GLA_DV)
            bc = b[sl, ks]
            bl = bc[c - 1:c, :]
            q = q_ref[sl, ks].astype(F32) * (GLA_DK ** -0.5)
            k = k_ref[sl, ks].astype(F32)
            v = v_ref[sl, vs]
            qe = (q * jnp.exp(bc - bl)).astype(BF16)
            kd = (k * jnp.exp(bl - bc)).astype(BF16)
            qb = (q * jnp.exp(bc)).astype(BF16)
            a = jnp.where(low, _dot_nt(qe, kd), 0.0).astype(BF16)
            st = st_ref[h]
            o = _dot(a, v) + _dot_nt(qb, st.astype(BF16))
            vt = v.astype(F32).T.astype(BF16)
            st_ref[h] = st * jnp.exp(bl) + _dot(vt, kd)
            o = _rms(o, gn)
            gr = r_ref[sl, vs].astype(F32)
            o_ref[sl, vs] = (o * (gr * jax.nn.sigmoid(gr))).astype(BF16)


def _gla(proj, wa2, ba, gn, batch, seq):
    n = proj.shape[0]
    rows = GLA_ROWS
    ng = seq // rows
    rmap = lambda off, w: (lambda b, i: (b * ng + i, off // w))
    return pl.pallas_call(
        _gla_body,
        grid=(batch, ng),
        in_specs=[
            pl.BlockSpec((rows, GLA_K), rmap(COL_GQ, GLA_K)),
            pl.BlockSpec((rows, GLA_K), rmap(COL_GK, GLA_K)),
            pl.BlockSpec((rows, GLA_V), rmap(COL_GV, GLA_V)),
            pl.BlockSpec((rows, GLA_V), rmap(COL_GR, GLA_V)),
            pl.BlockSpec((rows, LANES), rmap(COL_SM, LANES)),
            pl.BlockSpec((GLA_GATE_RANK, GLA_K), lambda b, i: (0, 0)),
            pl.BlockSpec((1, GLA_K), lambda b, i: (0, 0)),
            pl.BlockSpec((1, GLA_DV), lambda b, i: (0, 0)),
        ],
        out_specs=pl.BlockSpec((rows, GLA_V), lambda b, i: (b * ng + i, 0)),
        out_shape=jax.ShapeDtypeStruct((n, GLA_V), BF16),
        scratch_shapes=[pltpu.VMEM((GLA_HEADS, GLA_DV, GLA_DK), F32)],
        compiler_params=_cparams(("parallel", "arbitrary")),
        name="gla",
    )(proj, proj, proj, proj, proj, wa2, ba, gn)


def _merge_body(x_ref, oa_ref, og_ref, ma_ref, mg_ref, wua_ref, wug_ref, wo_ref, o_ref):
    ya = jax.nn.sigmoid(ma_ref[...].astype(F32)) * _dot(oa_ref[...], wua_ref[...])
    yg = jax.nn.sigmoid(mg_ref[...].astype(F32)) * _dot(og_ref[...], wug_ref[...])
    y = (ya + yg).astype(BF16)
    o_ref[...] = x_ref[...] + _dot(y, wo_ref[...])


def _merge(x, o_attn, o_gla, proj, wua, wug, wo, layer, *, tm=256):
    n, d = x.shape
    const = lambda i: (layer, 0, 0)
    return pl.pallas_call(
        _merge_body,
        grid=(n // tm,),
        in_specs=[
            pl.BlockSpec((tm, d), lambda i: (i, 0)),
            pl.BlockSpec((tm, ATTN_Q), lambda i: (i, 0)),
            pl.BlockSpec((tm, GLA_V), lambda i: (i, 0)),
            pl.BlockSpec((tm, d), lambda i: (i, COL_MA // D_MODEL)),
            pl.BlockSpec((tm, d), lambda i: (i, COL_MG // D_MODEL)),
            pl.BlockSpec((None, ATTN_Q, d), const),
            pl.BlockSpec((None, GLA_V, d), const),
            pl.BlockSpec((None, d, d), const),
        ],
        out_specs=pl.BlockSpec((tm, d), lambda i: (i, 0)),
        out_shape=jax.ShapeDtypeStruct((n, d), F32),
        compiler_params=_cparams(("parallel",)),
        name="merge",
    )(x, o_attn, o_gla, proj, proj, wua, wug, wo)


def kernel(x, ffn1_norm, ffn1_w_gate, ffn1_w_up, ffn1_w_down, mix_norm, w_in, gla_w_a2, gla_b_a, gla_norm, w_up_attn, w_up_gla, w_out, ffn2_norm, ffn2_w_gate, ffn2_w_up, ffn2_w_down, final_norm):
    batch, seq, d = x.shape
    depth = w_in.shape[0]
    tabs_a = _rope_tables(seq, ATTN_HEAD_DIM, ATTN_ROT)
    tabs_i = _rope_tables(seq, IDX_HEAD_DIM, IDX_ROT)
    fg = final_norm.reshape(1, d)
    g1, gm, g2 = (a.reshape(depth, 1, d) for a in (ffn1_norm, mix_norm, ffn2_norm))
    ffn1 = (_to_bf16(ffn1_w_gate, 256), _to_bf16(ffn1_w_up, 256), _to_bf16(ffn1_w_down, 512))
    ffn2 = (_to_bf16(ffn2_w_gate, 256), _to_bf16(ffn2_w_up, 256), _to_bf16(ffn2_w_down, 512))
    w_in_b = _prep_w_in(w_in)
    wua, wug, wo = _to_bf16(w_up_attn, 512), _to_bf16(w_up_gla, 512), _to_bf16(w_out, 512)
    h = x.reshape(batch * seq, d)
    for l in range(depth):
        h = _ffn(h, g1, *ffn1, fg, l, final_norm=False)
        proj = _proj(h, gm, w_in_b, l)
        aq_t, iq_t, sm_t, ak_r, sm_r, av_t = _rope(proj, seq, tabs_a, tabs_i)
        o_attn = _dsa(aq_t, iq_t, sm_t, ak_r, sm_r, av_t, batch, seq)
        o_gla = _gla(proj, gla_w_a2[l].astype(BF16), gla_b_a[l].reshape(1, GLA_K),
                     gla_norm[l].reshape(1, GLA_DV), batch, seq)
        h = _merge(h, o_attn, o_gla, proj, wua, wug, wo, l)
        h = _ffn(h, g2, *ffn2, fg, l, final_norm=(l == depth - 1))
    return h.reshape(batch, seq, d)
```

```python
import functools

import jax
import jax.numpy as jnp
import numpy as np
from jax import lax
from jax.experimental import pallas as pl
from jax.experimental.pallas import tpu as pltpu

F32 = jnp.float32
BF16 = jnp.bfloat16

D_MODEL = 2048
D_FF = 5632
FFN_RES = 0.5
ATTN_HEADS = 8
ATTN_KV_HEADS = 2
ATTN_HEAD_DIM = 128
ATTN_REP = ATTN_HEADS // ATTN_KV_HEADS
IDX_HEADS = 16
IDX_HEAD_DIM = 64
TOPK_MAX = 256
GLA_HEADS = 4
GLA_DK = 128
GLA_DV = 256
GLA_GATE_RANK = 16
GLA_GATE_NORM = 16.0
GLA_CHUNK = 64
ROPE_THETA = 500000.0
ATTN_ROT = ATTN_HEAD_DIM // 4
IDX_ROT = IDX_HEAD_DIM // 4
EPS = 1e-6
NEG_INF = -1e30
INT_MIN = -(2 ** 31)
PACK16 = 16

ATTN_Q = ATTN_HEADS * ATTN_HEAD_DIM
ATTN_KV = ATTN_KV_HEADS * ATTN_HEAD_DIM
IDX_Q = IDX_HEADS * IDX_HEAD_DIM
GLA_K = GLA_HEADS * GLA_DK
GLA_V = GLA_HEADS * GLA_DV
IN_SPLITS = (ATTN_Q, ATTN_KV, ATTN_KV, IDX_Q, IDX_HEAD_DIM, IDX_HEADS,
             GLA_K, GLA_K, GLA_V, GLA_GATE_RANK, GLA_V, D_MODEL, D_MODEL)
IN_OFFSETS = tuple(int(v) for v in np.cumsum(IN_SPLITS)[:-1])
IN_NAMES = ("aq", "ak", "av", "iq", "ik", "iw", "gq", "gk", "gv", "ga", "gr", "m_a", "m_g")

LANES = 128
COL_AQ = 0
COL_IQ = COL_AQ + ATTN_Q
COL_GV = COL_IQ + IDX_Q
COL_GR = COL_GV + GLA_V
COL_MA = COL_GR + GLA_V
COL_MG = COL_MA + D_MODEL
COL_GQ = COL_MG + D_MODEL
COL_GK = COL_GQ + GLA_K
COL_AK = COL_GK + GLA_K
COL_AV = COL_AK + ATTN_KV
COL_SM = COL_AV + ATTN_KV
SM_IK = 0
SM_IW = SM_IK + IDX_HEAD_DIM
SM_GA = SM_IW + IDX_HEADS
PROJ_TN = 1024
PROJ_COLS = -(-(COL_SM + LANES) // PROJ_TN) * PROJ_TN

DSA_TQ = 128
DSA_TK = 256
VT_ROWS = ATTN_HEAD_DIM + PACK16
LOG2E = 1.4426950408889634

FFN_SLAB = 512

VMEM_LIMIT = 62 * 1024 * 1024


def _cparams(sem):
    return pltpu.CompilerParams(dimension_semantics=sem, vmem_limit_bytes=VMEM_LIMIT)


def _dot(a, b):
    return jnp.dot(a, b, preferred_element_type=F32)


def _dot_nt(a, b):
    return lax.dot_general(a, b, (((1,), (1,)), ((), ())), preferred_element_type=F32)


def _rms(x, g):
    return x * lax.rsqrt(jnp.mean(x * x, axis=-1, keepdims=True) + EPS) * g


def _ffn_body(x_ref, g_ref, wg_ref, wu_ref, wd_ref, fg_ref, o_ref, h_ref, *, final_norm):
    f = pl.program_id(1)

    slabs = [slice(r, r + FFN_SLAB) for r in range(0, x_ref.shape[0], FFN_SLAB)]

    @pl.when(f == 0)
    def _():
        for rows in slabs:
            h_ref[rows, :] = _rms(x_ref[rows, :], g_ref[...]).astype(BF16)
        o_ref[...] = jnp.zeros_like(o_ref)

    for rows in slabs:
        h = h_ref[rows, :]
        gate = _dot(h, wg_ref[...])
        up = _dot(h, wu_ref[...])
        act = (gate * jax.nn.sigmoid(gate) * up).astype(BF16)
        o_ref[rows, :] += _dot(act, wd_ref[...])

    @pl.when(f == pl.num_programs(1) - 1)
    def _():
        for rows in slabs:
            y = x_ref[rows, :] + FFN_RES * o_ref[rows, :]
            if final_norm:
                y = _rms(y, fg_ref[...])
            o_ref[rows, :] = y


def _ffn(x, g, wg, wu, wd, fg, layer, *, final_norm, tm=1024, tf=512):
    n, d = x.shape
    f = wg.shape[2]
    return pl.pallas_call(
        functools.partial(_ffn_body, final_norm=final_norm),
        grid=(n // tm, f // tf),
        in_specs=[
            pl.BlockSpec((tm, d), lambda i, j: (i, 0)),
            pl.BlockSpec((None, 1, d), lambda i, j: (layer, 0, 0)),
            pl.BlockSpec((None, d, tf), lambda i, j: (layer, 0, j)),
            pl.BlockSpec((None, d, tf), lambda i, j: (layer, 0, j)),
            pl.BlockSpec((None, tf, d), lambda i, j: (layer, j, 0)),
            pl.BlockSpec((1, d), lambda i, j: (0, 0)),
        ],
        out_specs=pl.BlockSpec((tm, d), lambda i, j: (i, 0)),
        out_shape=jax.ShapeDtypeStruct((n, d), F32),
        scratch_shapes=[pltpu.VMEM((tm, d), BF16)],
        compiler_params=_cparams(("parallel", "arbitrary")),
        name="ffn",
    )(x, g, wg, wu, wd, fg)


def _cast_body(w_ref, o_ref):
    o_ref[...] = w_ref[...].astype(o_ref.dtype)


def _to_bf16(w, tr):
    depth, r, c = w.shape
    spec = pl.BlockSpec((None, tr, c), lambda l, i: (l, i, 0))
    return pl.pallas_call(
        _cast_body,
        grid=(depth, r // tr),
        in_specs=[spec],
        out_specs=spec,
        out_shape=jax.ShapeDtypeStruct(w.shape, BF16),
        compiler_params=_cparams(("parallel", "parallel")),
        name="cast",
    )(w)


def _w_in_body(w_ref, o_ref):
    w = w_ref[...]
    seg = {name: w[:, off:off + width] for name, off, width in
           zip(IN_NAMES, (0,) + IN_OFFSETS, IN_SPLITS)}
    tr = w.shape[0]
    small_pad = jnp.zeros((tr, LANES - IDX_HEAD_DIM - IDX_HEADS - GLA_GATE_RANK), w.dtype)
    tail = jnp.zeros((tr, PROJ_COLS - COL_SM - LANES), w.dtype)
    order = ("aq", "iq", "gv", "gr", "m_a", "m_g", "gq", "gk", "ak", "av", "ik", "iw", "ga")
    o_ref[...] = jnp.concatenate([seg[k] for k in order] + [small_pad, tail], axis=1).astype(o_ref.dtype)


def _prep_w_in(w, tr=256):
    depth, d, c = w.shape
    return pl.pallas_call(
        _w_in_body,
        grid=(depth, d // tr),
        in_specs=[pl.BlockSpec((None, tr, c), lambda l, i: (l, i, 0))],
        out_specs=pl.BlockSpec((None, tr, PROJ_COLS), lambda l, i: (l, i, 0)),
        out_shape=jax.ShapeDtypeStruct((depth, d, PROJ_COLS), BF16),
        compiler_params=_cparams(("parallel", "parallel")),
        name="w_in_prep",
    )(w)


def _proj_body(x_ref, g_ref, w_ref, o_ref, h_ref):
    @pl.when(pl.program_id(1) == 0)
    def _():
        h_ref[...] = _rms(x_ref[...], g_ref[...]).astype(BF16)

    o_ref[...] = _dot(h_ref[...], w_ref[...]).astype(o_ref.dtype)


def _proj(x, g, w, layer, *, tm=1024, tn=2 * PROJ_TN):
    n, d = x.shape
    c = w.shape[2]
    return pl.pallas_call(
        _proj_body,
        grid=(n // tm, c // tn),
        in_specs=[
            pl.BlockSpec((tm, d), lambda i, j: (i, 0)),
            pl.BlockSpec((None, 1, d), lambda i, j: (layer, 0, 0)),
            pl.BlockSpec((None, d, tn), lambda i, j: (layer, 0, j)),
        ],
        out_specs=pl.BlockSpec((tm, tn), lambda i, j: (i, j)),
        out_shape=jax.ShapeDtypeStruct((n, c), BF16),
        scratch_shapes=[pltpu.VMEM((tm, d), BF16)],
        compiler_params=_cparams(("parallel", "arbitrary")),
        name="proj",
    )(x, g, w)


def _rope_tables(length, head_dim, rot):
    half = rot // 2
    inv = ROPE_THETA ** (-jnp.arange(0, rot, 2, dtype=F32) / rot)
    ang = jnp.arange(length, dtype=F32)[:, None] * inv[None, :]
    cos, sin = jnp.cos(ang), jnp.sin(ang)
    zeros = jnp.zeros((length, head_dim - rot), F32)
    zh = jnp.zeros((length, half), F32)
    c = jnp.concatenate([cos, cos, zeros + 1.0], axis=1)
    s1 = jnp.concatenate([-sin, zh, zeros], axis=1)
    s2 = jnp.concatenate([zh, sin, zeros], axis=1)
    reps = LANES // head_dim
    return tuple(jnp.tile(t, (1, reps)) for t in (c, s1, s2))


def _rope_lanes(x, c, s1, s2, half):
    return x * c + pltpu.roll(x, LANES - half, 1) * s1 + pltpu.roll(x, half, 1) * s2


def _store_t(dst_ref, rows, val):
    vt = val.T.astype(dst_ref.dtype)
    for c in range(vt.shape[1] // DSA_TQ):
        dst_ref[c, rows, :] = vt[:, c * DSA_TQ:(c + 1) * DSA_TQ]


def _rope_body(aq_ref, iq_ref, ak_ref, av_ref, sm_ref, ca_ref, sa1_ref, sa2_ref, ci_ref, si1_ref, si2_ref,
               aqt_ref, iqt_ref, smt_ref, ako_ref, smo_ref, avt_ref):
    ca, sa1, sa2 = ca_ref[...], sa1_ref[...], sa2_ref[...]
    ci, si1, si2 = ci_ref[...], si1_ref[...], si2_ref[...]
    ha, hi = ATTN_ROT // 2, IDX_ROT // 2
    tk = DSA_TK
    for j in range(ATTN_Q // LANES):
        sl = slice(j * LANES, (j + 1) * LANES)
        r = _rope_lanes(aq_ref[:, sl].astype(F32), ca, sa1, sa2, ha)
        _store_t(aqt_ref, sl, r * (ATTN_HEAD_DIM ** -0.5 * LOG2E))
    for j in range(IDX_Q // LANES):
        sl = slice(j * LANES, (j + 1) * LANES)
        r = _rope_lanes(iq_ref[:, sl].astype(F32), ci, si1, si2, hi)
        _store_t(iqt_ref, sl, r * (IDX_HEAD_DIM ** -0.5))
    for j in range(ATTN_KV // LANES):
        sl = slice(j * LANES, (j + 1) * LANES)
        ako_ref[:, sl] = _rope_lanes(ak_ref[:, sl].astype(F32), ca, sa1, sa2, ha).astype(BF16)
        vt = av_ref[:, sl].astype(F32).T
        for c in range(vt.shape[1] // tk):
            avt_ref[c, j * VT_ROWS:j * VT_ROWS + LANES, :] = vt[:, c * tk:(c + 1) * tk].astype(BF16)
            avt_ref[c, j * VT_ROWS + LANES:(j + 1) * VT_ROWS, :] = jnp.ones((VT_ROWS - LANES, tk), BF16)
    sm = sm_ref[...].astype(F32)
    lane = lax.broadcasted_iota(jnp.int32, sm.shape, 1)
    sm = jnp.where(lane < IDX_HEAD_DIM, _rope_lanes(sm, ci, si1, si2, hi), sm)
    smo_ref[...] = sm.astype(BF16)
    _store_t(smt_ref, slice(0, LANES), sm)


def _rope(proj, seq, tabs_a, tabs_i, *, tr=512):
    n = proj.shape[0]
    nl = seq // tr
    tk = DSA_TK
    tab = pl.BlockSpec((tr, LANES), lambda i: (i % nl, 0))
    return pl.pallas_call(
        _rope_body,
        grid=(n // tr,),
        in_specs=[
            pl.BlockSpec((tr, ATTN_Q), lambda i: (i, COL_AQ // ATTN_Q)),
            pl.BlockSpec((tr, IDX_Q), lambda i: (i, COL_IQ // IDX_Q)),
            pl.BlockSpec((tr, ATTN_KV), lambda i: (i, COL_AK // ATTN_KV)),
            pl.BlockSpec((tr, ATTN_KV), lambda i: (i, COL_AV // ATTN_KV)),
            pl.BlockSpec((tr, LANES), lambda i: (i, COL_SM // LANES)),
            tab, tab, tab, tab, tab, tab,
        ],
        out_specs=[
            pl.BlockSpec((tr // DSA_TQ, ATTN_Q, DSA_TQ), lambda i: (i, 0, 0)),
            pl.BlockSpec((tr // DSA_TQ, IDX_Q, DSA_TQ), lambda i: (i, 0, 0)),
            pl.BlockSpec((tr // DSA_TQ, LANES, DSA_TQ), lambda i: (i, 0, 0)),
            pl.BlockSpec((tr, ATTN_KV), lambda i: (i, 0)),
            pl.BlockSpec((tr, LANES), lambda i: (i, 0)),
            pl.BlockSpec((tr // tk, ATTN_KV_HEADS * VT_ROWS, tk), lambda i: (i, 0, 0)),
        ],
        out_shape=[
            jax.ShapeDtypeStruct((n // DSA_TQ, ATTN_Q, DSA_TQ), BF16),
            jax.ShapeDtypeStruct((n // DSA_TQ, IDX_Q, DSA_TQ), BF16),
            jax.ShapeDtypeStruct((n // DSA_TQ, LANES, DSA_TQ), BF16),
            jax.ShapeDtypeStruct((n, ATTN_KV), BF16),
            jax.ShapeDtypeStruct((n, LANES), BF16),
            jax.ShapeDtypeStruct((n // tk, ATTN_KV_HEADS * VT_ROWS, tk), BF16),
        ],
        compiler_params=_cparams(("parallel",)),
        name="rope",
    )(proj, proj, proj, proj, proj, *tabs_a, *tabs_i)


def _sortable(x):
    i = pltpu.bitcast(x + 0.0, jnp.int32)
    return jnp.where(i < 0, i ^ jnp.int32(0x7FFFFFFF), i)


def _tree_sum(xs):
    while len(xs) > 1:
        xs = [xs[i] + xs[i + 1] for i in range(0, len(xs) - 1, 2)] + ([xs[-1]] if len(xs) % 2 else [])
    return xs[0]


def _colsum(x):
    r, c = x.shape
    return jnp.sum(jnp.sum(x.reshape(r // 8, 8, c), axis=0), axis=0, keepdims=True)


def _dsa_body(iqt_ref, smt_ref, aqt_ref, smk_ref, ak_ref, avt_ref, o_ref,
              key_ref, m_ref, al_ref, acc_ref, s0_ref, s1_ref, p0_ref, p1_ref, *, topk, seq):
    tq, tk = DSA_TQ, DSA_TK
    qi = pl.program_id(1)
    t0 = qi * tq
    nkb = (t0 + tq + tk - 1) // tk
    iwt = smt_ref[SM_IW:SM_IW + IDX_HEADS, :].astype(F32) * (IDX_HEADS ** -0.5)
    qpos = lax.broadcasted_iota(jnp.int32, (tk, tq), 1) + t0
    kiota = lax.broadcasted_iota(jnp.int32, (tk, tq), 0)

    npair = (nkb + 1) // 2
    odd = nkb % 2 == 1

    def idx_block(kb):
        k0 = pl.multiple_of(kb * tk, tk)
        ik = smk_ref[pl.ds(k0, tk), :][:, SM_IK:SM_IK + IDX_HEAD_DIM]
        score = jnp.zeros((tk, tq), F32)
        for h in range(0, IDX_HEADS, 2):
            qq = jnp.concatenate([iqt_ref[(h + u) * IDX_HEAD_DIM:(h + u + 1) * IDX_HEAD_DIM, :]
                                  for u in range(2)], axis=1)
            rel = jnp.maximum(_dot(ik, qq), 0.0)
            score = score + rel[:, :tq] * iwt[h:h + 1, :] + rel[:, tq:] * iwt[h + 1:h + 2, :]
        key_ref[kb] = jnp.where(kiota + kb * tk <= qpos, _sortable(score), jnp.int32(INT_MIN))

    def idx_pair(i, carry):
        idx_block(2 * i)
        idx_block(2 * i + 1)
        return carry

    lax.fori_loop(0, nkb // 2, idx_pair, 0)

    @pl.when(odd)
    def _():
        idx_block(nkb - 1)
        key_ref[nkb] = jnp.full((tk, tq), INT_MIN, jnp.int32)

    def counts(*preds):
        def pair(i, cs):
            for u in range(2):
                kb = 2 * i + u
                key = key_ref[kb]
                hits = [jnp.where(p(key, kb), 1, 0) for p in preds]
                cs = tuple(c + _tree_sum([h[j:j + 8] for j in range(0, tk, 8)]) for c, h in zip(cs, hits))
            return cs
        cs = lax.fori_loop(0, npair, pair, tuple(jnp.zeros((8, tq), jnp.int32) for _ in preds))
        return tuple(jnp.sum(c.astype(F32), axis=0, keepdims=True) for c in cs)

    def count(pred):
        return counts(pred)[0]

    kf = float(topk)

    def bisect(it, t):
        cand = t + lax.shift_left(jnp.int32(1), jnp.int32(31) - it)
        return jnp.where(count(lambda k, kb: k >= cand) >= kf, cand, t)

    thr = lax.fori_loop(0, 32, bisect, jnp.full((1, tq), INT_MIN, jnp.int32))

    n_gt, n_ge = counts(lambda k, kb: k > thr, lambda k, kb: k >= thr)
    need = kf - n_gt
    tied = (n_ge > kf) & (thr > jnp.int32(INT_MIN))

    def tie_break():
        def step(it, x):
            cand = x + lax.shift_left(jnp.int32(1), jnp.int32(seq.bit_length() - 1) - it)
            below = count(lambda k, kb: (k == thr) & (kiota + kb * tk < cand))
            return jnp.where(below < need, cand, x)
        return lax.fori_loop(0, seq.bit_length(), step, jnp.zeros((1, tq), jnp.int32))

    jtie = lax.cond(jnp.max(jnp.where(tied, 1.0, 0.0)) > 0.0, tie_break,
                    lambda: jnp.zeros((1, tq), jnp.int32))
    jmax = jnp.where(tied, jtie, jnp.where(thr > jnp.int32(INT_MIN), jnp.int32(seq), jnp.int32(-1)))

    last = nkb - 1

    def scores(kb, s_ref):
        kb = jnp.minimum(kb, last)
        kc = jnp.minimum(kb, seq // tk - 1)
        k0 = pl.multiple_of(kc * tk, tk)
        key = key_ref[kb]
        sel = (key > thr) | ((key == thr) & (kiota + kb * tk <= jmax))
        bias = jnp.where(sel, 0.0, NEG_INF)
        bias = jnp.concatenate([bias] * ATTN_REP, axis=1)
        for g in range(ATTN_KV_HEADS):
            kg = ak_ref[pl.ds(k0, tk), g * ATTN_HEAD_DIM:(g + 1) * ATTN_HEAD_DIM]
            qt = jnp.concatenate(
                [aqt_ref[(g * ATTN_REP + r) * ATTN_HEAD_DIM:(g * ATTN_REP + r + 1) * ATTN_HEAD_DIM, :]
                 for r in range(ATTN_REP)], axis=1)
            s_ref[g] = _dot(kg, qt) + bias

    def softmax(s_ref, p_ref):
        for g in range(ATTN_KV_HEADS):
            for r in range(ATTN_REP):
                cols = slice(r * tq, (r + 1) * tq)
                s = s_ref[g, :, cols]
                m_old = m_ref[g, :, cols]
                m_new = jnp.maximum(m_old, jnp.max(s, axis=0, keepdims=True))
                p_ref[g, :, cols] = jnp.exp2(s - m_new).astype(BF16)
                al_ref[g, :, cols] = jnp.exp2(m_old - m_new)
                m_ref[g, :, cols] = m_new

    def values(kb, p_ref):
        kc = jnp.clip(kb, 0, seq // tk - 1)
        for g in range(ATTN_KV_HEADS):
            vt = avt_ref[kc, g * VT_ROWS:(g + 1) * VT_ROWS, :]
            acc_ref[g] = al_ref[g] * acc_ref[g] + _dot(vt, p_ref[g])

    m_ref[...] = jnp.full_like(m_ref, NEG_INF)
    acc_ref[...] = jnp.zeros_like(acc_ref)
    al_ref[...] = jnp.ones_like(al_ref)
    p1_ref[...] = jnp.zeros_like(p1_ref)
    scores(0, s0_ref)

    def att_pair(i, carry):
        j = 2 * i
        values(j - 1, p1_ref)
        softmax(s0_ref, p0_ref)
        scores(j + 1, s1_ref)
        values(j, p0_ref)
        softmax(s1_ref, p1_ref)
        scores(j + 2, s0_ref)
        return carry

    lax.fori_loop(0, nkb // 2, att_pair, 0)

    @pl.when(odd)
    def _():
        values(last - 1, p1_ref)
        softmax(s0_ref, p0_ref)
        values(last, p0_ref)

    @pl.when(jnp.logical_not(odd))
    def _():
        values(last, p1_ref)
    for g in range(ATTN_KV_HEADS):
        acc = acc_ref[g]
        ot = acc[:ATTN_HEAD_DIM] / acc[ATTN_HEAD_DIM:ATTN_HEAD_DIM + 1]
        for r in range(ATTN_REP):
            hh = g * ATTN_REP + r
            o_ref[:, hh * ATTN_HEAD_DIM:(hh + 1) * ATTN_HEAD_DIM] = ot[:, r * tq:(r + 1) * tq].T.astype(BF16)


def _dsa(aq_t, iq_t, sm_t, ak_r, sm_r, av_t, batch, seq):
    n = ak_r.shape[0]
    tq, tk = DSA_TQ, DSA_TK
    nq = seq // tq
    nkb = seq // tk
    topk = min(TOPK_MAX, seq // 4)
    qmap = lambda b, i: (b * nq + i, 0, 0)
    return pl.pallas_call(
        functools.partial(_dsa_body, topk=topk, seq=seq),
        grid=(batch, nq),
        in_specs=[
            pl.BlockSpec((None, IDX_Q, tq), qmap),
            pl.BlockSpec((None, LANES, tq), qmap),
            pl.BlockSpec((None, ATTN_Q, tq), qmap),
            pl.BlockSpec((seq, LANES), lambda b, i: (b, 0)),
            pl.BlockSpec((seq, ATTN_KV), lambda b, i: (b, 0)),
            pl.BlockSpec((nkb, ATTN_KV_HEADS * VT_ROWS, tk), lambda b, i: (b, 0, 0)),
        ],
        out_specs=pl.BlockSpec((tq, ATTN_Q), lambda b, i: (b * nq + i, 0)),
        out_shape=jax.ShapeDtypeStruct((n, ATTN_Q), BF16),
        scratch_shapes=[
            pltpu.VMEM((nkb + nkb % 2, tk, tq), jnp.int32),
            pltpu.VMEM((ATTN_KV_HEADS, 1, ATTN_REP * tq), F32),
            pltpu.VMEM((ATTN_KV_HEADS, 1, ATTN_REP * tq), F32),
            pltpu.VMEM((ATTN_KV_HEADS, VT_ROWS, ATTN_REP * tq), F32),
            pltpu.VMEM((ATTN_KV_HEADS, tk, ATTN_REP * tq), F32),
            pltpu.VMEM((ATTN_KV_HEADS, tk, ATTN_REP * tq), F32),
            pltpu.VMEM((ATTN_KV_HEADS, tk, ATTN_REP * tq), BF16),
            pltpu.VMEM((ATTN_KV_HEADS, tk, ATTN_REP * tq), BF16),
        ],
        compiler_params=_cparams(("parallel", "arbitrary")),
        name="dsa",
    )(iq_t, sm_t, aq_t, sm_r, ak_r, av_t)


GLA_ROWS = 256


def _split3(x):
    hi = x.astype(BF16)
    r1 = x - hi.astype(F32)
    mid = r1.astype(BF16)
    lo = (r1 - mid.astype(F32)).astype(BF16)
    return hi, mid, lo


def _gla_body(q_ref, k_ref, v_ref, r_ref, sm_ref, wa_ref, ba_ref, gn_ref, o_ref, st_ref):
    rows, c = GLA_ROWS, GLA_CHUNK

    @pl.when(pl.program_id(1) == 0)
    def _():
        st_ref[...] = jnp.zeros_like(st_ref)

    ga = sm_ref[:, SM_GA:SM_GA + GLA_GATE_RANK]
    z = _dot(ga, wa_ref[...]) + ba_ref[...]
    g = -(jnp.maximum(-z, 0.0) + jnp.log1p(jnp.exp(-jnp.abs(z)))) * (1.0 / GLA_GATE_NORM)
    ri = lax.broadcasted_iota(jnp.int32, (rows, rows), 0)
    ci = lax.broadcasted_iota(jnp.int32, (rows, rows), 1)
    tri = jnp.where((ri // c == ci // c) & (ci <= ri), 1.0, 0.0).astype(BF16)
    hi, mid, lo = _split3(g)
    b = _dot(tri, hi) + _dot(tri, mid) + _dot(tri, lo)
    low = lax.broadcasted_iota(jnp.int32, (c, c), 1) <= lax.broadcasted_iota(jnp.int32, (c, c), 0)
    gn = gn_ref[...]
    for ch in range(rows // c):
        sl = slice(ch * c, (ch + 1) * c)
        for h in range(GLA_HEADS):
            ks = slice(h * GLA_DK, (h + 1) * GLA_DK)
            vs = slice(h * GLA_DV, (h + 1) * GLA_DV)
            bc = b[sl, ks]
            bl = bc[c - 1:c, :]
            q = q_ref[sl, ks].astype(F32) * (GLA_DK ** -0.5)
            k = k_ref[sl, ks].astype(F32)
            v = v_ref[sl, vs]
            qe = (q * jnp.exp(bc - bl)).astype(BF16)
            kd = (k * jnp.exp(bl - bc)).astype(BF16)
            qb = (q * jnp.exp(bc)).astype(BF16)
            a = jnp.where(low, _dot_nt(qe, kd), 0.0).astype(BF16)
            st = st_ref[h]
            o = _dot(a, v) + _dot_nt(qb, st.astype(BF16))
            vt = v.astype(F32).T.astype(BF16)
            st_ref[h] = st * jnp.exp(bl) + _dot(vt, kd)
            o = _rms(o, gn)
            gr = r_ref[sl, vs].astype(F32)
            o_ref[sl, vs] = (o * (gr * jax.nn.sigmoid(gr))).astype(BF16)


def _gla(proj, wa2, ba, gn, batch, seq):
    n = proj.shape[0]
    rows = GLA_ROWS
    ng = seq // rows
    rmap = lambda off, w: (lambda b, i: (b * ng + i, off // w))
    return pl.pallas_call(
        _gla_body,
        grid=(batch, ng),
        in_specs=[
            pl.BlockSpec((rows, GLA_K), rmap(COL_GQ, GLA_K)),
            pl.BlockSpec((rows, GLA_K), rmap(COL_GK, GLA_K)),
            pl.BlockSpec((rows, GLA_V), rmap(COL_GV, GLA_V)),
            pl.BlockSpec((rows, GLA_V), rmap(COL_GR, GLA_V)),
            pl.BlockSpec((rows, LANES), rmap(COL_SM, LANES)),
            pl.BlockSpec((GLA_GATE_RANK, GLA_K), lambda b, i: (0, 0)),
            pl.BlockSpec((1, GLA_K), lambda b, i: (0, 0)),
            pl.BlockSpec((1, GLA_DV), lambda b, i: (0, 0)),
        ],
        out_specs=pl.BlockSpec((rows, GLA_V), lambda b, i: (b * ng + i, 0)),
        out_shape=jax.ShapeDtypeStruct((n, GLA_V), BF16),
        scratch_shapes=[pltpu.VMEM((GLA_HEADS, GLA_DV, GLA_DK), F32)],
        compiler_params=_cparams(("parallel", "arbitrary")),
        name="gla",
    )(proj, proj, proj, proj, proj, wa2, ba, gn)


def _merge_body(x_ref, oa_ref, og_ref, ma_ref, mg_ref, wua_ref, wug_ref, wo_ref, o_ref):
    ya = jax.nn.sigmoid(ma_ref[...].astype(F32)) * _dot(oa_ref[...], wua_ref[...])
    yg = jax.nn.sigmoid(mg_ref[...].astype(F32)) * _dot(og_ref[...], wug_ref[...])
    y = (ya + yg).astype(BF16)
    o_ref[...] = x_ref[...] + _dot(y, wo_ref[...])


def _merge(x, o_attn, o_gla, proj, wua, wug, wo, layer, *, tm=256):
    n, d = x.shape
    const = lambda i: (layer, 0, 0)
    return pl.pallas_call(
        _merge_body,
        grid=(n // tm,),
        in_specs=[
            pl.BlockSpec((tm, d), lambda i: (i, 0)),
            pl.BlockSpec((tm, ATTN_Q), lambda i: (i, 0)),
            pl.BlockSpec((tm, GLA_V), lambda i: (i, 0)),
            pl.BlockSpec((tm, d), lambda i: (i, COL_MA // D_MODEL)),
            pl.BlockSpec((tm, d), lambda i: (i, COL_MG // D_MODEL)),
            pl.BlockSpec((None, ATTN_Q, d), const),
            pl.BlockSpec((None, GLA_V, d), const),
            pl.BlockSpec((None, d, d), const),
        ],
        out_specs=pl.BlockSpec((tm, d), lambda i: (i, 0)),
        out_shape=jax.ShapeDtypeStruct((n, d), F32),
        compiler_params=_cparams(("parallel",)),
        name="merge",
    )(x, o_attn, o_gla, proj, proj, wua, wug, wo)


def kernel(x, ffn1_norm, ffn1_w_gate, ffn1_w_up, ffn1_w_down, mix_norm, w_in, gla_w_a2, gla_b_a, gla_norm, w_up_attn, w_up_gla, w_out, ffn2_norm, ffn2_w_gate, ffn2_w_up, ffn2_w_down, final_norm):
    batch, seq, d = x.shape
    depth = w_in.shape[0]
    tabs_a = _rope_tables(seq, ATTN_HEAD_DIM, ATTN_ROT)
    tabs_i = _rope_tables(seq, IDX_HEAD_DIM, IDX_ROT)
    fg = final_norm.reshape(1, d)
    g1, gm, g2 = (a.reshape(depth, 1, d) for a in (ffn1_norm, mix_norm, ffn2_norm))
    ffn1 = (_to_bf16(ffn1_w_gate, 256), _to_bf16(ffn1_w_up, 256), _to_bf16(ffn1_w_down, 512))
    ffn2 = (_to_bf16(ffn2_w_gate, 256), _to_bf16(ffn2_w_up, 256), _to_bf16(ffn2_w_down, 512))
    w_in_b = _prep_w_in(w_in)
    wua, wug, wo = _to_bf16(w_up_attn, 512), _to_bf16(w_up_gla, 512), _to_bf16(w_out, 512)
    h = x.reshape(batch * seq, d)
    for l in range(depth):
        h = _ffn(h, g1, *ffn1, fg, l, final_norm=False)
        proj = _proj(h, gm, w_in_b, l)
        aq_t, iq_t, sm_t, ak_r, sm_r, av_t = _rope(proj, seq, tabs_a, tabs_i)
        o_attn = _dsa(aq_t, iq_t, sm_t, ak_r, sm_r, av_t, batch, seq)
        o_gla = _gla(proj, gla_w_a2[l].astype(BF16), gla_b_a[l].reshape(1, GLA_K),
                     gla_norm[l].reshape(1, GLA_DV), batch, seq)
        h = _merge(h, o_attn, o_gla, proj, wua, wug, wo, l)
        h = _ffn(h, g2, *ffn2, fg, l, final_norm=(l == depth - 1))
    return h.reshape(batch, seq, d)
```

```python
import functools

import jax
import jax.numpy as jnp
import numpy as np
from jax import lax
from jax.experimental import pallas as pl
from jax.experimental.pallas import tpu as pltpu

F32 = jnp.float32
BF16 = jnp.bfloat16

D_MODEL = 2048
D_FF = 5632
FFN_RES = 0.5
ATTN_HEADS = 8
ATTN_KV_HEADS = 2
ATTN_HEAD_DIM = 128
ATTN_REP = ATTN_HEADS // ATTN_KV_HEADS
IDX_HEADS = 16
IDX_HEAD_DIM = 64
TOPK_MAX = 256
GLA_HEADS = 4
GLA_DK = 128
GLA_DV = 256
GLA_GATE_RANK = 16
GLA_GATE_NORM = 16.0
GLA_CHUNK = 64
ROPE_THETA = 500000.0
ATTN_ROT = ATTN_HEAD_DIM // 4
IDX_ROT = IDX_HEAD_DIM // 4
EPS = 1e-6
NEG_INF = -1e30
INT_MIN = -(2 ** 31)
PACK16 = 16

ATTN_Q = ATTN_HEADS * ATTN_HEAD_DIM
ATTN_KV = ATTN_KV_HEADS * ATTN_HEAD_DIM
IDX_Q = IDX_HEADS * IDX_HEAD_DIM
GLA_K = GLA_HEADS * GLA_DK
GLA_V = GLA_HEADS * GLA_DV
IN_SPLITS = (ATTN_Q, ATTN_KV, ATTN_KV, IDX_Q, IDX_HEAD_DIM, IDX_HEADS,
             GLA_K, GLA_K, GLA_V, GLA_GATE_RANK, GLA_V, D_MODEL, D_MODEL)
IN_OFFSETS = tuple(int(v) for v in np.cumsum(IN_SPLITS)[:-1])
IN_NAMES = ("aq", "ak", "av", "iq", "ik", "iw", "gq", "gk", "gv", "ga", "gr", "m_a", "m_g")

LANES = 128
COL_AQ = 0
COL_IQ = COL_AQ + ATTN_Q
COL_GV = COL_IQ + IDX_Q
COL_GR = COL_GV + GLA_V
COL_MA = COL_GR + GLA_V
COL_MG = COL_MA + D_MODEL
COL_GQ = COL_MG + D_MODEL
COL_GK = COL_GQ + GLA_K
COL_AK = COL_GK + GLA_K
COL_AV = COL_AK + ATTN_KV
COL_SM = COL_AV + ATTN_KV
SM_IK = 0
SM_IW = SM_IK + IDX_HEAD_DIM
SM_GA = SM_IW + IDX_HEADS
PROJ_TN = 1024
PROJ_COLS = -(-(COL_SM + LANES) // PROJ_TN) * PROJ_TN

DSA_TQ = 128
DSA_TK = 256
CAST_ROWS = 2048
VT_ROWS = ATTN_HEAD_DIM + PACK16
LOG2E = 1.4426950408889634

FFN_SLAB = 512

VMEM_LIMIT = 62 * 1024 * 1024


def _cparams(sem):
    return pltpu.CompilerParams(dimension_semantics=sem, vmem_limit_bytes=VMEM_LIMIT)


def _dot(a, b):
    return jnp.dot(a, b, preferred_element_type=F32)


def _dot_nt(a, b):
    return lax.dot_general(a, b, (((1,), (1,)), ((), ())), preferred_element_type=F32)


def _rms(x, g):
    return x * lax.rsqrt(jnp.mean(x * x, axis=-1, keepdims=True) + EPS) * g


def _ffn_body(x_ref, g_ref, wg_ref, wu_ref, wd_ref, fg_ref, o_ref, h_ref, *, final_norm):
    f = pl.program_id(1)

    slabs = [slice(r, r + FFN_SLAB) for r in range(0, x_ref.shape[0], FFN_SLAB)]

    @pl.when(f == 0)
    def _():
        for rows in slabs:
            h_ref[rows, :] = _rms(x_ref[rows, :], g_ref[...]).astype(BF16)
        o_ref[...] = jnp.zeros_like(o_ref)

    for rows in slabs:
        h = h_ref[rows, :]
        gate = _dot(h, wg_ref[...])
        up = _dot(h, wu_ref[...])
        act = (gate * jax.nn.sigmoid(gate) * up).astype(BF16)
        o_ref[rows, :] += _dot(act, wd_ref[...])

    @pl.when(f == pl.num_programs(1) - 1)
    def _():
        for rows in slabs:
            y = x_ref[rows, :] + FFN_RES * o_ref[rows, :]
            if final_norm:
                y = _rms(y, fg_ref[...])
            o_ref[rows, :] = y


def _ffn(x, g, wg, wu, wd, fg, layer, wlayer, *, final_norm, tm=1024, tf=512):
    n, d = x.shape
    f = wg.shape[2]
    return pl.pallas_call(
        functools.partial(_ffn_body, final_norm=final_norm),
        grid=(n // tm, f // tf),
        in_specs=[
            pl.BlockSpec((tm, d), lambda i, j: (i, 0)),
            pl.BlockSpec((None, 1, d), lambda i, j: (layer, 0, 0)),
            pl.BlockSpec((None, d, tf), lambda i, j: (wlayer, 0, j)),
            pl.BlockSpec((None, d, tf), lambda i, j: (wlayer, 0, j)),
            pl.BlockSpec((None, tf, d), lambda i, j: (wlayer, j, 0)),
            pl.BlockSpec((1, d), lambda i, j: (0, 0)),
        ],
        out_specs=pl.BlockSpec((tm, d), lambda i, j: (i, 0)),
        out_shape=jax.ShapeDtypeStruct((n, d), F32),
        scratch_shapes=[pltpu.VMEM((tm, d), BF16)],
        compiler_params=_cparams(("parallel", "arbitrary")),
        name="ffn",
    )(x, g, wg, wu, wd, fg)


def _cast_body(w_ref, o_ref):
    o_ref[...] = w_ref[...].astype(o_ref.dtype)


def _to_bf16(w, tr, layer):
    _, r, c = w.shape
    return pl.pallas_call(
        _cast_body,
        grid=(r // tr,),
        in_specs=[pl.BlockSpec((None, tr, c), lambda i: (layer, i, 0))],
        out_specs=pl.BlockSpec((None, tr, c), lambda i: (0, i, 0)),
        out_shape=jax.ShapeDtypeStruct((1, r, c), BF16),
        compiler_params=_cparams(("parallel",)),
        name="cast",
    )(w)


def _w_in_body(w_ref, o_ref):
    w = w_ref[...]
    seg = {name: w[:, off:off + width] for name, off, width in
           zip(IN_NAMES, (0,) + IN_OFFSETS, IN_SPLITS)}
    tr = w.shape[0]
    small_pad = jnp.zeros((tr, LANES - IDX_HEAD_DIM - IDX_HEADS - GLA_GATE_RANK), w.dtype)
    tail = jnp.zeros((tr, PROJ_COLS - COL_SM - LANES), w.dtype)
    order = ("aq", "iq", "gv", "gr", "m_a", "m_g", "gq", "gk", "ak", "av", "ik", "iw", "ga")
    o_ref[...] = jnp.concatenate([seg[k] for k in order] + [small_pad, tail], axis=1).astype(o_ref.dtype)


def _prep_w_in(w, tr=256):
    depth, d, c = w.shape
    return pl.pallas_call(
        _w_in_body,
        grid=(depth, d // tr),
        in_specs=[pl.BlockSpec((None, tr, c), lambda l, i: (l, i, 0))],
        out_specs=pl.BlockSpec((None, tr, PROJ_COLS), lambda l, i: (l, i, 0)),
        out_shape=jax.ShapeDtypeStruct((depth, d, PROJ_COLS), BF16),
        compiler_params=_cparams(("parallel", "parallel")),
        name="w_in_prep",
    )(w)


def _proj_body(x_ref, g_ref, w_ref, o_ref, h_ref):
    @pl.when(pl.program_id(1) == 0)
    def _():
        h_ref[...] = _rms(x_ref[...], g_ref[...]).astype(BF16)

    o_ref[...] = _dot(h_ref[...], w_ref[...]).astype(o_ref.dtype)


def _proj(x, g, w, layer, *, tm=1024, tn=2 * PROJ_TN):
    n, d = x.shape
    c = w.shape[2]
    return pl.pallas_call(
        _proj_body,
        grid=(n // tm, c // tn),
        in_specs=[
            pl.BlockSpec((tm, d), lambda i, j: (i, 0)),
            pl.BlockSpec((None, 1, d), lambda i, j: (layer, 0, 0)),
            pl.BlockSpec((None, d, tn), lambda i, j: (layer, 0, j)),
        ],
        out_specs=pl.BlockSpec((tm, tn), lambda i, j: (i, j)),
        out_shape=jax.ShapeDtypeStruct((n, c), BF16),
        scratch_shapes=[pltpu.VMEM((tm, d), BF16)],
        compiler_params=_cparams(("parallel", "arbitrary")),
        name="proj",
    )(x, g, w)


def _rope_tables(length, head_dim, rot):
    half = rot // 2
    inv = ROPE_THETA ** (-jnp.arange(0, rot, 2, dtype=F32) / rot)
    ang = jnp.arange(length, dtype=F32)[:, None] * inv[None, :]
    cos, sin = jnp.cos(ang), jnp.sin(ang)
    zeros = jnp.zeros((length, head_dim - rot), F32)
    zh = jnp.zeros((length, half), F32)
    c = jnp.concatenate([cos, cos, zeros + 1.0], axis=1)
    s1 = jnp.concatenate([-sin, zh, zeros], axis=1)
    s2 = jnp.concatenate([zh, sin, zeros], axis=1)
    reps = LANES // head_dim
    return tuple(jnp.tile(t, (1, reps)) for t in (c, s1, s2))


def _rope_lanes(x, c, s1, s2, half):
    return x * c + pltpu.roll(x, LANES - half, 1) * s1 + pltpu.roll(x, half, 1) * s2


def _store_t(dst_ref, rows, val):
    vt = val.T.astype(dst_ref.dtype)
    for c in range(vt.shape[1] // DSA_TQ):
        dst_ref[c, rows, :] = vt[:, c * DSA_TQ:(c + 1) * DSA_TQ]


def _rope_body(aq_ref, iq_ref, ak_ref, av_ref, sm_ref, ca_ref, sa1_ref, sa2_ref, ci_ref, si1_ref, si2_ref,
               aqt_ref, iqt_ref, smt_ref, ako_ref, smo_ref, avt_ref):
    ca, sa1, sa2 = ca_ref[...], sa1_ref[...], sa2_ref[...]
    ci, si1, si2 = ci_ref[...], si1_ref[...], si2_ref[...]
    ha, hi = ATTN_ROT // 2, IDX_ROT // 2
    tk = DSA_TK
    for j in range(ATTN_Q // LANES):
        sl = slice(j * LANES, (j + 1) * LANES)
        r = _rope_lanes(aq_ref[:, sl].astype(F32), ca, sa1, sa2, ha)
        _store_t(aqt_ref, sl, r * (ATTN_HEAD_DIM ** -0.5 * LOG2E))
    for j in range(IDX_Q // LANES):
        sl = slice(j * LANES, (j + 1) * LANES)
        r = _rope_lanes(iq_ref[:, sl].astype(F32), ci, si1, si2, hi)
        _store_t(iqt_ref, sl, r * (IDX_HEAD_DIM ** -0.5))
    for j in range(ATTN_KV // LANES):
        sl = slice(j * LANES, (j + 1) * LANES)
        ako_ref[:, sl] = _rope_lanes(ak_ref[:, sl].astype(F32), ca, sa1, sa2, ha).astype(BF16)
        vt = av_ref[:, sl].astype(F32).T
        for c in range(vt.shape[1] // tk):
            avt_ref[c, j * VT_ROWS:j * VT_ROWS + LANES, :] = vt[:, c * tk:(c + 1) * tk].astype(BF16)
            avt_ref[c, j * VT_ROWS + LANES:(j + 1) * VT_ROWS, :] = jnp.ones((VT_ROWS - LANES, tk), BF16)
    sm = sm_ref[...].astype(F32)
    lane = lax.broadcasted_iota(jnp.int32, sm.shape, 1)
    sm = jnp.where(lane < IDX_HEAD_DIM, _rope_lanes(sm, ci, si1, si2, hi), sm)
    smo_ref[...] = sm.astype(BF16)
    _store_t(smt_ref, slice(0, LANES), sm)


def _rope(proj, seq, tabs_a, tabs_i, *, tr=512):
    n = proj.shape[0]
    nl = seq // tr
    tk = DSA_TK
    tab = pl.BlockSpec((tr, LANES), lambda i: (i % nl, 0))
    return pl.pallas_call(
        _rope_body,
        grid=(n // tr,),
        in_specs=[
            pl.BlockSpec((tr, ATTN_Q), lambda i: (i, COL_AQ // ATTN_Q)),
            pl.BlockSpec((tr, IDX_Q), lambda i: (i, COL_IQ // IDX_Q)),
            pl.BlockSpec((tr, ATTN_KV), lambda i: (i, COL_AK // ATTN_KV)),
            pl.BlockSpec((tr, ATTN_KV), lambda i: (i, COL_AV // ATTN_KV)),
            pl.BlockSpec((tr, LANES), lambda i: (i, COL_SM // LANES)),
            tab, tab, tab, tab, tab, tab,
        ],
        out_specs=[
            pl.BlockSpec((tr // DSA_TQ, ATTN_Q, DSA_TQ), lambda i: (i, 0, 0)),
            pl.BlockSpec((tr // DSA_TQ, IDX_Q, DSA_TQ), lambda i: (i, 0, 0)),
            pl.BlockSpec((tr // DSA_TQ, LANES, DSA_TQ), lambda i: (i, 0, 0)),
            pl.BlockSpec((tr, ATTN_KV), lambda i: (i, 0)),
            pl.BlockSpec((tr, LANES), lambda i: (i, 0)),
            pl.BlockSpec((tr // tk, ATTN_KV_HEADS * VT_ROWS, tk), lambda i: (i, 0, 0)),
        ],
        out_shape=[
            jax.ShapeDtypeStruct((n // DSA_TQ, ATTN_Q, DSA_TQ), BF16),
            jax.ShapeDtypeStruct((n // DSA_TQ, IDX_Q, DSA_TQ), BF16),
            jax.ShapeDtypeStruct((n // DSA_TQ, LANES, DSA_TQ), BF16),
            jax.ShapeDtypeStruct((n, ATTN_KV), BF16),
            jax.ShapeDtypeStruct((n, LANES), BF16),
            jax.ShapeDtypeStruct((n // tk, ATTN_KV_HEADS * VT_ROWS, tk), BF16),
        ],
        compiler_params=_cparams(("parallel",)),
        name="rope",
    )(proj, proj, proj, proj, proj, *tabs_a, *tabs_i)


def _sortable(x):
    i = pltpu.bitcast(x + 0.0, jnp.int32)
    return jnp.where(i < 0, i ^ jnp.int32(0x7FFFFFFF), i)


def _tree_sum(xs):
    while len(xs) > 1:
        xs = [xs[i] + xs[i + 1] for i in range(0, len(xs) - 1, 2)] + ([xs[-1]] if len(xs) % 2 else [])
    return xs[0]


def _colsum(x):
    r, c = x.shape
    return jnp.sum(jnp.sum(x.reshape(r // 8, 8, c), axis=0), axis=0, keepdims=True)


def _dsa_body(*refs, topk, seq, n_cast):
    iqt_ref, smt_ref, aqt_ref, smk_ref, ak_ref, avt_ref = refs[:6]
    cast_in, o_ref, cast_out = refs[6:6 + n_cast], refs[6 + n_cast], refs[7 + n_cast:7 + 2 * n_cast]
    key_ref, m_ref, al_ref, acc_ref, s0_ref, s1_ref, p0_ref, p1_ref = refs[7 + 2 * n_cast:]
    for w_ref, wb_ref in zip(cast_in, cast_out):
        wb_ref[...] = w_ref[...].astype(BF16)
    tq, tk = DSA_TQ, DSA_TK
    qi = pl.program_id(1)
    t0 = qi * tq
    nkb = (t0 + tq + tk - 1) // tk
    iwt = smt_ref[SM_IW:SM_IW + IDX_HEADS, :].astype(F32) * (IDX_HEADS ** -0.5)
    qpos = lax.broadcasted_iota(jnp.int32, (tk, tq), 1) + t0
    kiota = lax.broadcasted_iota(jnp.int32, (tk, tq), 0)

    npair = (nkb + 1) // 2
    odd = nkb % 2 == 1

    def idx_block(kb):
        k0 = pl.multiple_of(kb * tk, tk)
        ik = smk_ref[pl.ds(k0, tk), :][:, SM_IK:SM_IK + IDX_HEAD_DIM]
        score = jnp.zeros((tk, tq), F32)
        for h in range(0, IDX_HEADS, 2):
            qq = jnp.concatenate([iqt_ref[(h + u) * IDX_HEAD_DIM:(h + u + 1) * IDX_HEAD_DIM, :]
                                  for u in range(2)], axis=1)
            rel = jnp.maximum(_dot(ik, qq), 0.0)
            score = score + rel[:, :tq] * iwt[h:h + 1, :] + rel[:, tq:] * iwt[h + 1:h + 2, :]
        key_ref[kb] = jnp.where(kiota + kb * tk <= qpos, _sortable(score), jnp.int32(INT_MIN))

    def idx_pair(i, carry):
        idx_block(2 * i)
        idx_block(2 * i + 1)
        return carry

    lax.fori_loop(0, nkb // 2, idx_pair, 0)

    @pl.when(odd)
    def _():
        idx_block(nkb - 1)
        key_ref[nkb] = jnp.full((tk, tq), INT_MIN, jnp.int32)

    def counts(*preds):
        def pair(i, cs):
            for u in range(2):
                kb = 2 * i + u
                key = key_ref[kb]
                hits = [jnp.where(p(key, kb), 1, 0) for p in preds]
                cs = tuple(c + _tree_sum([h[j:j + 8] for j in range(0, tk, 8)]) for c, h in zip(cs, hits))
            return cs
        cs = lax.fori_loop(0, npair, pair, tuple(jnp.zeros((8, tq), jnp.int32) for _ in preds))
        return tuple(jnp.sum(c.astype(F32), axis=0, keepdims=True) for c in cs)

    def count(pred):
        return counts(pred)[0]

    kf = float(topk)

    def bisect(it, t):
        cand = t + lax.shift_left(jnp.int32(1), jnp.int32(31) - it)
        return jnp.where(count(lambda k, kb: k >= cand) >= kf, cand, t)

    thr = lax.fori_loop(0, 32, bisect, jnp.full((1, tq), INT_MIN, jnp.int32))

    n_gt, n_ge = counts(lambda k, kb: k > thr, lambda k, kb: k >= thr)
    need = kf - n_gt
    tied = (n_ge > kf) & (thr > jnp.int32(INT_MIN))

    def tie_break():
        def step(it, x):
            cand = x + lax.shift_left(jnp.int32(1), jnp.int32(seq.bit_length() - 1) - it)
            below = count(lambda k, kb: (k == thr) & (kiota + kb * tk < cand))
            return jnp.where(below < need, cand, x)
        return lax.fori_loop(0, seq.bit_length(), step, jnp.zeros((1, tq), jnp.int32))

    jtie = lax.cond(jnp.max(jnp.where(tied, 1.0, 0.0)) > 0.0, tie_break,
                    lambda: jnp.zeros((1, tq), jnp.int32))
    jmax = jnp.where(tied, jtie, jnp.where(thr > jnp.int32(INT_MIN), jnp.int32(seq), jnp.int32(-1)))

    last = nkb - 1

    def scores(kb, s_ref):
        kb = jnp.minimum(kb, last)
        kc = jnp.minimum(kb, seq // tk - 1)
        k0 = pl.multiple_of(kc * tk, tk)
        key = key_ref[kb]
        sel = (key > thr) | ((key == thr) & (kiota + kb * tk <= jmax))
        bias = jnp.where(sel, 0.0, NEG_INF)
        bias = jnp.concatenate([bias] * ATTN_REP, axis=1)
        for g in range(ATTN_KV_HEADS):
            kg = ak_ref[pl.ds(k0, tk), g * ATTN_HEAD_DIM:(g + 1) * ATTN_HEAD_DIM]
            qt = jnp.concatenate(
                [aqt_ref[(g * ATTN_REP + r) * ATTN_HEAD_DIM:(g * ATTN_REP + r + 1) * ATTN_HEAD_DIM, :]
                 for r in range(ATTN_REP)], axis=1)
            s_ref[g] = _dot(kg, qt) + bias

    def softmax(s_ref, p_ref):
        for g in range(ATTN_KV_HEADS):
            for r in range(ATTN_REP):
                cols = slice(r * tq, (r + 1) * tq)
                s = s_ref[g, :, cols]
                m_old = m_ref[g, :, cols]
                m_new = jnp.maximum(m_old, jnp.max(s, axis=0, keepdims=True))
                p_ref[g, :, cols] = jnp.exp2(s - m_new).astype(BF16)
                al_ref[g, :, cols] = jnp.exp2(m_old - m_new)
                m_ref[g, :, cols] = m_new

    def values(kb, p_ref):
        kc = jnp.clip(kb, 0, seq // tk - 1)
        for g in range(ATTN_KV_HEADS):
            vt = avt_ref[kc, g * VT_ROWS:(g + 1) * VT_ROWS, :]
            acc_ref[g] = al_ref[g] * acc_ref[g] + _dot(vt, p_ref[g])

    m_ref[...] = jnp.full_like(m_ref, NEG_INF)
    acc_ref[...] = jnp.zeros_like(acc_ref)
    al_ref[...] = jnp.ones_like(al_ref)
    p1_ref[...] = jnp.zeros_like(p1_ref)
    scores(0, s0_ref)

    def att_pair(i, carry):
        j = 2 * i
        values(j - 1, p1_ref)
        softmax(s0_ref, p0_ref)
        scores(j + 1, s1_ref)
        values(j, p0_ref)
        softmax(s1_ref, p1_ref)
        scores(j + 2, s0_ref)
        return carry

    lax.fori_loop(0, nkb // 2, att_pair, 0)

    @pl.when(odd)
    def _():
        values(last - 1, p1_ref)
        softmax(s0_ref, p0_ref)
        values(last, p0_ref)

    @pl.when(jnp.logical_not(odd))
    def _():
        values(last, p1_ref)
    for g in range(ATTN_KV_HEADS):
        acc = acc_ref[g]
        ot = acc[:ATTN_HEAD_DIM] / acc[ATTN_HEAD_DIM:ATTN_HEAD_DIM + 1]
        for r in range(ATTN_REP):
            hh = g * ATTN_REP + r
            o_ref[:, hh * ATTN_HEAD_DIM:(hh + 1) * ATTN_HEAD_DIM] = ot[:, r * tq:(r + 1) * tq].T.astype(BF16)


def _dsa(aq_t, iq_t, sm_t, ak_r, sm_r, av_t, batch, seq, casts=()):
    n = ak_r.shape[0]
    tq, tk = DSA_TQ, DSA_TK
    nq = seq // tq
    nkb = seq // tk
    topk = min(TOPK_MAX, seq // 4)
    qmap = lambda b, i: (b * nq + i, 0, 0)
    steps = batch * nq
    cast_ops, cast_in_specs, cast_out_specs, cast_out_shapes = [], [], [], []
    for w, layer in casts:
        depth, r, c = w.shape
        rows = CAST_ROWS
        cols = r * c // rows
        slab = rows // steps
        assert r * c % rows == 0 and rows % steps == 0 and slab % PACK16 == 0 and cols % LANES == 0
        cast_ops.append(w.reshape(depth, rows, cols))
        cast_in_specs.append(pl.BlockSpec((None, slab, cols), lambda b, i, layer=layer: (layer, b * nq + i, 0)))
        cast_out_specs.append(pl.BlockSpec((slab, cols), lambda b, i: (b * nq + i, 0)))
        cast_out_shapes.append(jax.ShapeDtypeStruct((rows, cols), BF16))
    out = pl.pallas_call(
        functools.partial(_dsa_body, topk=topk, seq=seq, n_cast=len(casts)),
        grid=(batch, nq),
        in_specs=[
            pl.BlockSpec((None, IDX_Q, tq), qmap),
            pl.BlockSpec((None, LANES, tq), qmap),
            pl.BlockSpec((None, ATTN_Q, tq), qmap),
            pl.BlockSpec((seq, LANES), lambda b, i: (b, 0)),
            pl.BlockSpec((seq, ATTN_KV), lambda b, i: (b, 0)),
            pl.BlockSpec((nkb, ATTN_KV_HEADS * VT_ROWS, tk), lambda b, i: (b, 0, 0)),
        ] + cast_in_specs,
        out_specs=[pl.BlockSpec((tq, ATTN_Q), lambda b, i: (b * nq + i, 0))] + cast_out_specs,
        out_shape=[jax.ShapeDtypeStruct((n, ATTN_Q), BF16)] + cast_out_shapes,
        scratch_shapes=[
            pltpu.VMEM((nkb + nkb % 2, tk, tq), jnp.int32),
            pltpu.VMEM((ATTN_KV_HEADS, 1, ATTN_REP * tq), F32),
            pltpu.VMEM((ATTN_KV_HEADS, 1, ATTN_REP * tq), F32),
            pltpu.VMEM((ATTN_KV_HEADS, VT_ROWS, ATTN_REP * tq), F32),
            pltpu.VMEM((ATTN_KV_HEADS, tk, ATTN_REP * tq), F32),
            pltpu.VMEM((ATTN_KV_HEADS, tk, ATTN_REP * tq), F32),
            pltpu.VMEM((ATTN_KV_HEADS, tk, ATTN_REP * tq), BF16),
            pltpu.VMEM((ATTN_KV_HEADS, tk, ATTN_REP * tq), BF16),
        ],
        compiler_params=_cparams(("parallel", "arbitrary")),
        name="dsa",
    )(iq_t, sm_t, aq_t, sm_r, ak_r, av_t, *cast_ops)
    return out[0], tuple(wb.reshape((1,) + w.shape[1:]) for wb, (w, _) in zip(out[1:], casts))


GLA_ROWS = 256


def _split3(x):
    hi = x.astype(BF16)
    r1 = x - hi.astype(F32)
    mid = r1.astype(BF16)
    lo = (r1 - mid.astype(F32)).astype(BF16)
    return hi, mid, lo


def _gla_body(q_ref, k_ref, v_ref, r_ref, sm_ref, wa_ref, ba_ref, gn_ref, o_ref, st_ref):
    rows, c = GLA_ROWS, GLA_CHUNK

    @pl.when(pl.program_id(1) == 0)
    def _():
        st_ref[...] = jnp.zeros_like(st_ref)

    ga = sm_ref[:, SM_GA:SM_GA + GLA_GATE_RANK]
    z = _dot(ga, wa_ref[...]) + ba_ref[...]
    g = -(jnp.maximum(-z, 0.0) + jnp.log1p(jnp.exp(-jnp.abs(z)))) * (1.0 / GLA_GATE_NORM)
    ri = lax.broadcasted_iota(jnp.int32, (rows, rows), 0)
    ci = lax.broadcasted_iota(jnp.int32, (rows, rows), 1)
    tri = jnp.where((ri // c == ci // c) & (ci <= ri), 1.0, 0.0).astype(BF16)
    hi, mid, lo = _split3(g)
    b = _dot(tri, hi) + _dot(tri, mid) + _dot(tri, lo)
    low = lax.broadcasted_iota(jnp.int32, (c, c), 1) <= lax.broadcasted_iota(jnp.int32, (c, c), 0)
    gn = gn_ref[...]
    for ch in range(rows // c):
        sl = slice(ch * c, (ch + 1) * c)
        for h in range(GLA_HEADS):
            ks = slice(h * GLA_DK, (h + 1) * GLA_DK)
            vs = slice(h * GLA_DV, (h + 1) * GLA_DV)
            bc = b[sl, ks]
            bl = bc[c - 1:c, :]
            q = q_ref[sl, ks].astype(F32) * (GLA_DK ** -0.5)
            k = k_ref[sl, ks].astype(F32)
            v = v_ref[sl, vs]
            qe = (q * jnp.exp(bc - bl)).astype(BF16)
            kd = (k * jnp.exp(bl - bc)).astype(BF16)
            qb = (q * jnp.exp(bc)).astype(BF16)
            a = jnp.where(low, _dot_nt(qe, kd), 0.0).astype(BF16)
            st = st_ref[h]
            o = _dot(a, v) + _dot_nt(qb, st.astype(BF16))
            vt = v.astype(F32).T.astype(BF16)
            st_ref[h] = st * jnp.exp(bl) + _dot(vt, kd)
            o = _rms(o, gn)
            gr = r_ref[sl, vs].astype(F32)
            o_ref[sl, vs] = (o * (gr * jax.nn.sigmoid(gr))).astype(BF16)


def _gla(proj, wa2, ba, gn, batch, seq):
    n = proj.shape[0]
    rows = GLA_ROWS
    ng = seq // rows
    rmap = lambda off, w: (lambda b, i: (b * ng + i, off // w))
    return pl.pallas_call(
        _gla_body,
        grid=(batch, ng),
        in_specs=[
            pl.BlockSpec((rows, GLA_K), rmap(COL_GQ, GLA_K)),
            pl.BlockSpec((rows, GLA_K), rmap(COL_GK, GLA_K)),
            pl.BlockSpec((rows, GLA_V), rmap(COL_GV, GLA_V)),
            pl.BlockSpec((rows, GLA_V), rmap(COL_GR, GLA_V)),
            pl.BlockSpec((rows, LANES), rmap(COL_SM, LANES)),
            pl.BlockSpec((GLA_GATE_RANK, GLA_K), lambda b, i: (0, 0)),
            pl.BlockSpec((1, GLA_K), lambda b, i: (0, 0)),
            pl.BlockSpec((1, GLA_DV), lambda b, i: (0, 0)),
        ],
        out_specs=pl.BlockSpec((rows, GLA_V), lambda b, i: (b * ng + i, 0)),
        out_shape=jax.ShapeDtypeStruct((n, GLA_V), BF16),
        scratch_shapes=[pltpu.VMEM((GLA_HEADS, GLA_DV, GLA_DK), F32)],
        compiler_params=_cparams(("parallel", "arbitrary")),
        name="gla",
    )(proj, proj, proj, proj, proj, wa2, ba, gn)


def _merge_body(x_ref, oa_ref, og_ref, ma_ref, mg_ref, wua_ref, wug_ref, wo_ref, o_ref):
    ya = jax.nn.sigmoid(ma_ref[...].astype(F32)) * _dot(oa_ref[...], wua_ref[...])
    yg = jax.nn.sigmoid(mg_ref[...].astype(F32)) * _dot(og_ref[...], wug_ref[...])
    y = (ya + yg).astype(BF16)
    o_ref[...] = x_ref[...] + _dot(y, wo_ref[...])


def _merge(x, o_attn, o_gla, proj, wua, wug, wo, layer, *, tm=256):
    n, d = x.shape
    const = lambda i: (layer, 0, 0)
    return pl.pallas_call(
        _merge_body,
        grid=(n // tm,),
        in_specs=[
            pl.BlockSpec((tm, d), lambda i: (i, 0)),
            pl.BlockSpec((tm, ATTN_Q), lambda i: (i, 0)),
            pl.BlockSpec((tm, GLA_V), lambda i: (i, 0)),
            pl.BlockSpec((tm, d), lambda i: (i, COL_MA // D_MODEL)),
            pl.BlockSpec((tm, d), lambda i: (i, COL_MG // D_MODEL)),
            pl.BlockSpec((None, ATTN_Q, d), const),
            pl.BlockSpec((None, GLA_V, d), const),
            pl.BlockSpec((None, d, d), const),
        ],
        out_specs=pl.BlockSpec((tm, d), lambda i: (i, 0)),
        out_shape=jax.ShapeDtypeStruct((n, d), F32),
        compiler_params=_cparams(("parallel",)),
        name="merge",
    )(x, o_attn, o_gla, proj, proj, wua, wug, wo)


def kernel(x, ffn1_norm, ffn1_w_gate, ffn1_w_up, ffn1_w_down, mix_norm, w_in, gla_w_a2, gla_b_a, gla_norm, w_up_attn, w_up_gla, w_out, ffn2_norm, ffn2_w_gate, ffn2_w_up, ffn2_w_down, final_norm):
    batch, seq, d = x.shape
    depth = w_in.shape[0]
    tabs_a = _rope_tables(seq, ATTN_HEAD_DIM, ATTN_ROT)
    tabs_i = _rope_tables(seq, IDX_HEAD_DIM, IDX_ROT)
    fg = final_norm.reshape(1, d)
    g1, gm, g2 = (a.reshape(depth, 1, d) for a in (ffn1_norm, mix_norm, ffn2_norm))
    w_in_b = _prep_w_in(w_in)
    ffn1_w, ffn2_w, merge_w = (ffn1_w_gate, ffn1_w_up, ffn1_w_down), (ffn2_w_gate, ffn2_w_up, ffn2_w_down), \
        (w_up_attn, w_up_gla, w_out)
    ffn1_b = {0: (_to_bf16(ffn1_w_gate, 256, 0), _to_bf16(ffn1_w_up, 256, 0), _to_bf16(ffn1_w_down, 512, 0))}
    hosted = [(w, l) for l in range(1, depth) for w in ffn1_w] + \
             [(w, l) for l in range(depth) for w in ffn2_w + merge_w]
    ffn2_b, merge_b = {}, {}
    h = x.reshape(batch * seq, d)
    for l in range(depth):
        h = _ffn(h, g1, *ffn1_b[l], fg, l, 0, final_norm=False)
        proj = _proj(h, gm, w_in_b, l)
        aq_t, iq_t, sm_t, ak_r, sm_r, av_t = _rope(proj, seq, tabs_a, tabs_i)
        o_attn, cast = _dsa(aq_t, iq_t, sm_t, ak_r, sm_r, av_t, batch, seq, hosted if l == 0 else ())
        if l == 0:
            cast = list(cast)
            for k in range(1, depth):
                ffn1_b[k], cast = tuple(cast[:3]), cast[3:]
            for k in range(depth):
                ffn2_b[k], merge_b[k], cast = tuple(cast[:3]), tuple(cast[3:6]), cast[6:]
        o_gla = _gla(proj, gla_w_a2[l].astype(BF16), gla_b_a[l].reshape(1, GLA_K),
                     gla_norm[l].reshape(1, GLA_DV), batch, seq)
        h = _merge(h, o_attn, o_gla, proj, *merge_b[l], 0)
        h = _ffn(h, g2, *ffn2_b[l], fg, l, 0, final_norm=(l == depth - 1))
    return h.reshape(batch, seq, d)
```

```python
import functools

import jax
import jax.numpy as jnp
import numpy as np
from jax import lax
from jax.experimental import pallas as pl
from jax.experimental.pallas import tpu as pltpu

F32 = jnp.float32
BF16 = jnp.bfloat16

D_MODEL = 2048
D_FF = 5632
FFN_RES = 0.5
ATTN_HEADS = 8
ATTN_KV_HEADS = 2
ATTN_HEAD_DIM = 128
ATTN_REP = ATTN_HEADS // ATTN_KV_HEADS
IDX_HEADS = 16
IDX_HEAD_DIM = 64
TOPK_MAX = 256
GLA_HEADS = 4
GLA_DK = 128
GLA_DV = 256
GLA_GATE_RANK = 16
GLA_GATE_NORM = 16.0
GLA_CHUNK = 64
ROPE_THETA = 500000.0
ATTN_ROT = ATTN_HEAD_DIM // 4
IDX_ROT = IDX_HEAD_DIM // 4
EPS = 1e-6
NEG_INF = -1e30
INT_MIN = -(2 ** 31)
PACK16 = 16

ATTN_Q = ATTN_HEADS * ATTN_HEAD_DIM
ATTN_KV = ATTN_KV_HEADS * ATTN_HEAD_DIM
IDX_Q = IDX_HEADS * IDX_HEAD_DIM
GLA_K = GLA_HEADS * GLA_DK
GLA_V = GLA_HEADS * GLA_DV
IN_SPLITS = (ATTN_Q, ATTN_KV, ATTN_KV, IDX_Q, IDX_HEAD_DIM, IDX_HEADS,
             GLA_K, GLA_K, GLA_V, GLA_GATE_RANK, GLA_V, D_MODEL, D_MODEL)
IN_OFFSETS = tuple(int(v) for v in np.cumsum(IN_SPLITS)[:-1])
IN_NAMES = ("aq", "ak", "av", "iq", "ik", "iw", "gq", "gk", "gv", "ga", "gr", "m_a", "m_g")

LANES = 128
COL_AQ = 0
COL_IQ = COL_AQ + ATTN_Q
COL_GV = COL_IQ + IDX_Q
COL_GR = COL_GV + GLA_V
COL_MA = COL_GR + GLA_V
COL_MG = COL_MA + D_MODEL
COL_GQ = COL_MG + D_MODEL
COL_GK = COL_GQ + GLA_K
COL_AK = COL_GK + GLA_K
COL_AV = COL_AK + ATTN_KV
COL_SM = COL_AV + ATTN_KV
SM_IK = 0
SM_IW = SM_IK + IDX_HEAD_DIM
SM_GA = SM_IW + IDX_HEADS
PROJ_TN = 1024
PROJ_COLS = -(-(COL_SM + LANES) // PROJ_TN) * PROJ_TN

DSA_TQ = 128
DSA_TK = 256
VT_ROWS = ATTN_HEAD_DIM + PACK16
LOG2E = 1.4426950408889634

FFN_SLAB = 512

VMEM_LIMIT = 62 * 1024 * 1024


def _cparams(sem):
    return pltpu.CompilerParams(dimension_semantics=sem, vmem_limit_bytes=VMEM_LIMIT)


def _dot(a, b):
    return jnp.dot(a, b, preferred_element_type=F32)


def _dot_nt(a, b):
    return lax.dot_general(a, b, (((1,), (1,)), ((), ())), preferred_element_type=F32)


def _rms(x, g):
    return x * lax.rsqrt(jnp.mean(x * x, axis=-1, keepdims=True) + EPS) * g


def _ffn_body(x_ref, g_ref, wg_ref, wu_ref, wd_ref, fg_ref, o_ref, h_ref, *, final_norm):
    f = pl.program_id(1)

    slabs = [slice(r, r + FFN_SLAB) for r in range(0, x_ref.shape[0], FFN_SLAB)]

    @pl.when(f == 0)
    def _():
        for rows in slabs:
            h_ref[rows, :] = _rms(x_ref[rows, :], g_ref[...]).astype(BF16)
        o_ref[...] = jnp.zeros_like(o_ref)

    for rows in slabs:
        h = h_ref[rows, :]
        gate = _dot(h, wg_ref[...])
        up = _dot(h, wu_ref[...])
        act = (gate * jax.nn.sigmoid(gate) * up).astype(BF16)
        o_ref[rows, :] += _dot(act, wd_ref[...])

    @pl.when(f == pl.num_programs(1) - 1)
    def _():
        for rows in slabs:
            y = x_ref[rows, :] + FFN_RES * o_ref[rows, :]
            if final_norm:
                y = _rms(y, fg_ref[...])
            o_ref[rows, :] = y


def _ffn(x, g, wg, wu, wd, fg, layer, wlayer, *, final_norm, tm=1024, tf=512):
    n, d = x.shape
    f = wg.shape[2]
    return pl.pallas_call(
        functools.partial(_ffn_body, final_norm=final_norm),
        grid=(n // tm, f // tf),
        in_specs=[
            pl.BlockSpec((tm, d), lambda i, j: (i, 0)),
            pl.BlockSpec((None, 1, d), lambda i, j: (layer, 0, 0)),
            pl.BlockSpec((None, d, tf), lambda i, j: (wlayer, 0, j)),
            pl.BlockSpec((None, d, tf), lambda i, j: (wlayer, 0, j)),
            pl.BlockSpec((None, tf, d), lambda i, j: (wlayer, j, 0)),
            pl.BlockSpec((1, d), lambda i, j: (0, 0)),
        ],
        out_specs=pl.BlockSpec((tm, d), lambda i, j: (i, 0)),
        out_shape=jax.ShapeDtypeStruct((n, d), F32),
        scratch_shapes=[pltpu.VMEM((tm, d), BF16)],
        compiler_params=_cparams(("parallel", "arbitrary")),
        name="ffn",
    )(x, g, wg, wu, wd, fg)


def _cast_body(w_ref, o_ref):
    o_ref[...] = w_ref[...].astype(o_ref.dtype)


def _to_bf16(w, tr, layer):
    _, r, c = w.shape
    return pl.pallas_call(
        _cast_body,
        grid=(r // tr,),
        in_specs=[pl.BlockSpec((None, tr, c), lambda i: (layer, i, 0))],
        out_specs=pl.BlockSpec((None, tr, c), lambda i: (0, i, 0)),
        out_shape=jax.ShapeDtypeStruct((1, r, c), BF16),
        compiler_params=_cparams(("parallel",)),
        name="cast",
    )(w)


def _w_in_body(w_ref, o_ref):
    w = w_ref[...]
    seg = {name: w[:, off:off + width] for name, off, width in
           zip(IN_NAMES, (0,) + IN_OFFSETS, IN_SPLITS)}
    tr = w.shape[0]
    small_pad = jnp.zeros((tr, LANES - IDX_HEAD_DIM - IDX_HEADS - GLA_GATE_RANK), w.dtype)
    tail = jnp.zeros((tr, PROJ_COLS - COL_SM - LANES), w.dtype)
    order = ("aq", "iq", "gv", "gr", "m_a", "m_g", "gq", "gk", "ak", "av", "ik", "iw", "ga")
    o_ref[...] = jnp.concatenate([seg[k] for k in order] + [small_pad, tail], axis=1).astype(o_ref.dtype)


def _prep_w_in(w, tr=256):
    depth, d, c = w.shape
    return pl.pallas_call(
        _w_in_body,
        grid=(depth, d // tr),
        in_specs=[pl.BlockSpec((None, tr, c), lambda l, i: (l, i, 0))],
        out_specs=pl.BlockSpec((None, tr, PROJ_COLS), lambda l, i: (l, i, 0)),
        out_shape=jax.ShapeDtypeStruct((depth, d, PROJ_COLS), BF16),
        compiler_params=_cparams(("parallel", "parallel")),
        name="w_in_prep",
    )(w)


def _proj_body(x_ref, g_ref, w_ref, o_ref, h_ref):
    @pl.when(pl.program_id(1) == 0)
    def _():
        h_ref[...] = _rms(x_ref[...], g_ref[...]).astype(BF16)

    o_ref[...] = _dot(h_ref[...], w_ref[...]).astype(o_ref.dtype)


def _proj(x, g, w, layer, *, tm=1024, tn=2 * PROJ_TN):
    n, d = x.shape
    c = w.shape[2]
    return pl.pallas_call(
        _proj_body,
        grid=(n // tm, c // tn),
        in_specs=[
            pl.BlockSpec((tm, d), lambda i, j: (i, 0)),
            pl.BlockSpec((None, 1, d), lambda i, j: (layer, 0, 0)),
            pl.BlockSpec((None, d, tn), lambda i, j: (layer, 0, j)),
        ],
        out_specs=pl.BlockSpec((tm, tn), lambda i, j: (i, j)),
        out_shape=jax.ShapeDtypeStruct((n, c), BF16),
        scratch_shapes=[pltpu.VMEM((tm, d), BF16)],
        compiler_params=_cparams(("parallel", "arbitrary")),
        name="proj",
    )(x, g, w)


def _rope_tables(length, head_dim, rot):
    half = rot // 2
    inv = ROPE_THETA ** (-jnp.arange(0, rot, 2, dtype=F32) / rot)
    ang = jnp.arange(length, dtype=F32)[:, None] * inv[None, :]
    cos, sin = jnp.cos(ang), jnp.sin(ang)
    zeros = jnp.zeros((length, head_dim - rot), F32)
    zh = jnp.zeros((length, half), F32)
    c = jnp.concatenate([cos, cos, zeros + 1.0], axis=1)
    s1 = jnp.concatenate([-sin, zh, zeros], axis=1)
    s2 = jnp.concatenate([zh, sin, zeros], axis=1)
    reps = LANES // head_dim
    return tuple(jnp.tile(t, (1, reps)) for t in (c, s1, s2))


def _rope_lanes(x, c, s1, s2, half):
    return x * c + pltpu.roll(x, LANES - half, 1) * s1 + pltpu.roll(x, half, 1) * s2


def _store_t(dst_ref, rows, val):
    vt = val.T.astype(dst_ref.dtype)
    for c in range(vt.shape[1] // DSA_TQ):
        dst_ref[c, rows, :] = vt[:, c * DSA_TQ:(c + 1) * DSA_TQ]


def _rope_body(aq_ref, iq_ref, ak_ref, av_ref, sm_ref, ca_ref, sa1_ref, sa2_ref, ci_ref, si1_ref, si2_ref,
               aqt_ref, iqt_ref, smt_ref, ako_ref, smo_ref, avt_ref):
    ca, sa1, sa2 = ca_ref[...], sa1_ref[...], sa2_ref[...]
    ci, si1, si2 = ci_ref[...], si1_ref[...], si2_ref[...]
    ha, hi = ATTN_ROT // 2, IDX_ROT // 2
    tk = DSA_TK
    for j in range(ATTN_Q // LANES):
        sl = slice(j * LANES, (j + 1) * LANES)
        r = _rope_lanes(aq_ref[:, sl].astype(F32), ca, sa1, sa2, ha)
        _store_t(aqt_ref, sl, r * (ATTN_HEAD_DIM ** -0.5 * LOG2E))
    for j in range(IDX_Q // LANES):
        sl = slice(j * LANES, (j + 1) * LANES)
        r = _rope_lanes(iq_ref[:, sl].astype(F32), ci, si1, si2, hi)
        _store_t(iqt_ref, sl, r * (IDX_HEAD_DIM ** -0.5))
    for j in range(ATTN_KV // LANES):
        sl = slice(j * LANES, (j + 1) * LANES)
        ako_ref[:, sl] = _rope_lanes(ak_ref[:, sl].astype(F32), ca, sa1, sa2, ha).astype(BF16)
        vt = av_ref[:, sl].astype(F32).T
        for c in range(vt.shape[1] // tk):
            avt_ref[c, j * VT_ROWS:j * VT_ROWS + LANES, :] = vt[:, c * tk:(c + 1) * tk].astype(BF16)
            avt_ref[c, j * VT_ROWS + LANES:(j + 1) * VT_ROWS, :] = jnp.ones((VT_ROWS - LANES, tk), BF16)
    sm = sm_ref[...].astype(F32)
    lane = lax.broadcasted_iota(jnp.int32, sm.shape, 1)
    sm = jnp.where(lane < IDX_HEAD_DIM, _rope_lanes(sm, ci, si1, si2, hi), sm)
    smo_ref[...] = sm.astype(BF16)
    _store_t(smt_ref, slice(0, LANES), sm)


def _rope(proj, seq, tabs_a, tabs_i, *, tr=512):
    n = proj.shape[0]
    nl = seq // tr
    tk = DSA_TK
    tab = pl.BlockSpec((tr, LANES), lambda i: (i % nl, 0))
    return pl.pallas_call(
        _rope_body,
        grid=(n // tr,),
        in_specs=[
            pl.BlockSpec((tr, ATTN_Q), lambda i: (i, COL_AQ // ATTN_Q)),
            pl.BlockSpec((tr, IDX_Q), lambda i: (i, COL_IQ // IDX_Q)),
            pl.BlockSpec((tr, ATTN_KV), lambda i: (i, COL_AK // ATTN_KV)),
            pl.BlockSpec((tr, ATTN_KV), lambda i: (i, COL_AV // ATTN_KV)),
            pl.BlockSpec((tr, LANES), lambda i: (i, COL_SM // LANES)),
            tab, tab, tab, tab, tab, tab,
        ],
        out_specs=[
            pl.BlockSpec((tr // DSA_TQ, ATTN_Q, DSA_TQ), lambda i: (i, 0, 0)),
            pl.BlockSpec((tr // DSA_TQ, IDX_Q, DSA_TQ), lambda i: (i, 0, 0)),
            pl.BlockSpec((tr // DSA_TQ, LANES, DSA_TQ), lambda i: (i, 0, 0)),
            pl.BlockSpec((tr, ATTN_KV), lambda i: (i, 0)),
            pl.BlockSpec((tr, LANES), lambda i: (i, 0)),
            pl.BlockSpec((tr // tk, ATTN_KV_HEADS * VT_ROWS, tk), lambda i: (i, 0, 0)),
        ],
        out_shape=[
            jax.ShapeDtypeStruct((n // DSA_TQ, ATTN_Q, DSA_TQ), BF16),
            jax.ShapeDtypeStruct((n // DSA_TQ, IDX_Q, DSA_TQ), BF16),
            jax.ShapeDtypeStruct((n // DSA_TQ, LANES, DSA_TQ), BF16),
            jax.ShapeDtypeStruct((n, ATTN_KV), BF16),
            jax.ShapeDtypeStruct((n, LANES), BF16),
            jax.ShapeDtypeStruct((n // tk, ATTN_KV_HEADS * VT_ROWS, tk), BF16),
        ],
        compiler_params=_cparams(("parallel",)),
        name="rope",
    )(proj, proj, proj, proj, proj, *tabs_a, *tabs_i)


def _sortable(x):
    i = pltpu.bitcast(x + 0.0, jnp.int32)
    return jnp.where(i < 0, i ^ jnp.int32(0x7FFFFFFF), i)


def _tree_sum(xs):
    while len(xs) > 1:
        xs = [xs[i] + xs[i + 1] for i in range(0, len(xs) - 1, 2)] + ([xs[-1]] if len(xs) % 2 else [])
    return xs[0]


def _colsum(x):
    r, c = x.shape
    return jnp.sum(jnp.sum(x.reshape(r // 8, 8, c), axis=0), axis=0, keepdims=True)


def _dsa_body(*refs, topk, seq, n_cast):
    iqt_ref, smt_ref, aqt_ref, smk_ref, ak_ref, avt_ref = refs[:6]
    cast_in, o_ref, cast_out = refs[6:6 + n_cast], refs[6 + n_cast], refs[7 + n_cast:7 + 2 * n_cast]
    key_ref, m_ref, al_ref, acc_ref, s0_ref, s1_ref, p0_ref, p1_ref = refs[7 + 2 * n_cast:]
    for w_ref, wb_ref in zip(cast_in, cast_out):
        wb_ref[...] = w_ref[...].astype(BF16)
    tq, tk = DSA_TQ, DSA_TK
    qi = pl.program_id(1)
    t0 = qi * tq
    nkb = (t0 + tq + tk - 1) // tk
    iwt = smt_ref[SM_IW:SM_IW + IDX_HEADS, :].astype(F32) * (IDX_HEADS ** -0.5)
    qpos = lax.broadcasted_iota(jnp.int32, (tk, tq), 1) + t0
    kiota = lax.broadcasted_iota(jnp.int32, (tk, tq), 0)

    npair = (nkb + 1) // 2
    odd = nkb % 2 == 1

    def idx_block(kb):
        k0 = pl.multiple_of(kb * tk, tk)
        ik = smk_ref[pl.ds(k0, tk), :][:, SM_IK:SM_IK + IDX_HEAD_DIM]
        score = jnp.zeros((tk, tq), F32)
        for h in range(0, IDX_HEADS, 2):
            qq = jnp.concatenate([iqt_ref[(h + u) * IDX_HEAD_DIM:(h + u + 1) * IDX_HEAD_DIM, :]
                                  for u in range(2)], axis=1)
            rel = jnp.maximum(_dot(ik, qq), 0.0)
            score = score + rel[:, :tq] * iwt[h:h + 1, :] + rel[:, tq:] * iwt[h + 1:h + 2, :]
        key_ref[kb] = jnp.where(kiota + kb * tk <= qpos, _sortable(score), jnp.int32(INT_MIN))

    def idx_pair(i, carry):
        idx_block(2 * i)
        idx_block(2 * i + 1)
        return carry

    lax.fori_loop(0, nkb // 2, idx_pair, 0)

    @pl.when(odd)
    def _():
        idx_block(nkb - 1)
        key_ref[nkb] = jnp.full((tk, tq), INT_MIN, jnp.int32)

    def counts(*preds):
        def pair(i, cs):
            for u in range(2):
                kb = 2 * i + u
                key = key_ref[kb]
                hits = [jnp.where(p(key, kb), 1, 0) for p in preds]
                cs = tuple(c + _tree_sum([h[j:j + 8] for j in range(0, tk, 8)]) for c, h in zip(cs, hits))
            return cs
        cs = lax.fori_loop(0, npair, pair, tuple(jnp.zeros((8, tq), jnp.int32) for _ in preds))
        return tuple(jnp.sum(c.astype(F32), axis=0, keepdims=True) for c in cs)

    def count(pred):
        return counts(pred)[0]

    kf = float(topk)

    def bisect(it, t):
        cand = t + lax.shift_left(jnp.int32(1), jnp.int32(31) - it)
        return jnp.where(count(lambda k, kb: k >= cand) >= kf, cand, t)

    thr = lax.fori_loop(0, 32, bisect, jnp.full((1, tq), INT_MIN, jnp.int32))

    n_gt, n_ge = counts(lambda k, kb: k > thr, lambda k, kb: k >= thr)
    need = kf - n_gt
    tied = (n_ge > kf) & (thr > jnp.int32(INT_MIN))

    def tie_break():
        def step(it, x):
            cand = x + lax.shift_left(jnp.int32(1), jnp.int32(seq.bit_length() - 1) - it)
            below = count(lambda k, kb: (k == thr) & (kiota + kb * tk < cand))
            return jnp.where(below < need, cand, x)
        return lax.fori_loop(0, seq.bit_length(), step, jnp.zeros((1, tq), jnp.int32))

    jtie = lax.cond(jnp.max(jnp.where(tied, 1.0, 0.0)) > 0.0, tie_break,
                    lambda: jnp.zeros((1, tq), jnp.int32))
    jmax = jnp.where(tied, jtie, jnp.where(thr > jnp.int32(INT_MIN), jnp.int32(seq), jnp.int32(-1)))

    last = nkb - 1

    def scores(kb, s_ref):
        kb = jnp.minimum(kb, last)
        kc = jnp.minimum(kb, seq // tk - 1)
        k0 = pl.multiple_of(kc * tk, tk)
        key = key_ref[kb]
        sel = (key > thr) | ((key == thr) & (kiota + kb * tk <= jmax))
        bias = jnp.where(sel, 0.0, NEG_INF)
        bias = jnp.concatenate([bias] * ATTN_REP, axis=1)
        for g in range(ATTN_KV_HEADS):
            kg = ak_ref[pl.ds(k0, tk), g * ATTN_HEAD_DIM:(g + 1) * ATTN_HEAD_DIM]
            qt = jnp.concatenate(
                [aqt_ref[(g * ATTN_REP + r) * ATTN_HEAD_DIM:(g * ATTN_REP + r + 1) * ATTN_HEAD_DIM, :]
                 for r in range(ATTN_REP)], axis=1)
            s_ref[g] = _dot(kg, qt) + bias

    def softmax(s_ref, p_ref):
        for g in range(ATTN_KV_HEADS):
            for r in range(ATTN_REP):
                cols = slice(r * tq, (r + 1) * tq)
                s = s_ref[g, :, cols]
                m_old = m_ref[g, :, cols]
                m_new = jnp.maximum(m_old, jnp.max(s, axis=0, keepdims=True))
                p_ref[g, :, cols] = jnp.exp2(s - m_new).astype(BF16)
                al_ref[g, :, cols] = jnp.exp2(m_old - m_new)
                m_ref[g, :, cols] = m_new

    def values(kb, p_ref):
        kc = jnp.clip(kb, 0, seq // tk - 1)
        for g in range(ATTN_KV_HEADS):
            vt = avt_ref[kc, g * VT_ROWS:(g + 1) * VT_ROWS, :]
            acc_ref[g] = al_ref[g] * acc_ref[g] + _dot(vt, p_ref[g])

    m_ref[...] = jnp.full_like(m_ref, NEG_INF)
    acc_ref[...] = jnp.zeros_like(acc_ref)
    al_ref[...] = jnp.ones_like(al_ref)
    p1_ref[...] = jnp.zeros_like(p1_ref)
    scores(0, s0_ref)

    def att_pair(i, carry):
        j = 2 * i
        values(j - 1, p1_ref)
        softmax(s0_ref, p0_ref)
        scores(j + 1, s1_ref)
        values(j, p0_ref)
        softmax(s1_ref, p1_ref)
        scores(j + 2, s0_ref)
        return carry

    lax.fori_loop(0, nkb // 2, att_pair, 0)

    @pl.when(odd)
    def _():
        values(last - 1, p1_ref)
        softmax(s0_ref, p0_ref)
        values(last, p0_ref)

    @pl.when(jnp.logical_not(odd))
    def _():
        values(last, p1_ref)
    for g in range(ATTN_KV_HEADS):
        acc = acc_ref[g]
        ot = acc[:ATTN_HEAD_DIM] / acc[ATTN_HEAD_DIM:ATTN_HEAD_DIM + 1]
        for r in range(ATTN_REP):
            hh = g * ATTN_REP + r
            o_ref[:, hh * ATTN_HEAD_DIM:(hh + 1) * ATTN_HEAD_DIM] = ot[:, r * tq:(r + 1) * tq].T.astype(BF16)


def _dsa(aq_t, iq_t, sm_t, ak_r, sm_r, av_t, batch, seq, casts=()):
    n = ak_r.shape[0]
    tq, tk = DSA_TQ, DSA_TK
    nq = seq // tq
    nkb = seq // tk
    topk = min(TOPK_MAX, seq // 4)
    qmap = lambda b, i: (b * nq + i, 0, 0)
    steps = batch * nq
    cast_ops, cast_in_specs, cast_out_specs, cast_out_shapes = [], [], [], []
    for w, layer in casts:
        _, r, c = w.shape
        slab = next(s for s in range(PACK16, r + 1, PACK16) if r % s == 0 and s * steps >= r)
        last = r // slab - 1
        cast_ops.append(w)
        cast_in_specs.append(pl.BlockSpec(
            (None, slab, c), lambda b, i, layer=layer, last=last: (layer, jnp.minimum(b * nq + i, last), 0)))
        cast_out_specs.append(pl.BlockSpec(
            (None, slab, c), lambda b, i, last=last: (0, jnp.minimum(b * nq + i, last), 0)))
        cast_out_shapes.append(jax.ShapeDtypeStruct((1, r, c), BF16))
    out = pl.pallas_call(
        functools.partial(_dsa_body, topk=topk, seq=seq, n_cast=len(casts)),
        grid=(batch, nq),
        in_specs=[
            pl.BlockSpec((None, IDX_Q, tq), qmap),
            pl.BlockSpec((None, LANES, tq), qmap),
            pl.BlockSpec((None, ATTN_Q, tq), qmap),
            pl.BlockSpec((seq, LANES), lambda b, i: (b, 0)),
            pl.BlockSpec((seq, ATTN_KV), lambda b, i: (b, 0)),
            pl.BlockSpec((nkb, ATTN_KV_HEADS * VT_ROWS, tk), lambda b, i: (b, 0, 0)),
        ] + cast_in_specs,
        out_specs=[pl.BlockSpec((tq, ATTN_Q), lambda b, i: (b * nq + i, 0))] + cast_out_specs,
        out_shape=[jax.ShapeDtypeStruct((n, ATTN_Q), BF16)] + cast_out_shapes,
        scratch_shapes=[
            pltpu.VMEM((nkb + nkb % 2, tk, tq), jnp.int32),
            pltpu.VMEM((ATTN_KV_HEADS, 1, ATTN_REP * tq), F32),
            pltpu.VMEM((ATTN_KV_HEADS, 1, ATTN_REP * tq), F32),
            pltpu.VMEM((ATTN_KV_HEADS, VT_ROWS, ATTN_REP * tq), F32),
            pltpu.VMEM((ATTN_KV_HEADS, tk, ATTN_REP * tq), F32),
            pltpu.VMEM((ATTN_KV_HEADS, tk, ATTN_REP * tq), F32),
            pltpu.VMEM((ATTN_KV_HEADS, tk, ATTN_REP * tq), BF16),
            pltpu.VMEM((ATTN_KV_HEADS, tk, ATTN_REP * tq), BF16),
        ],
        compiler_params=_cparams(("arbitrary", "arbitrary")),
        name="dsa",
    )(iq_t, sm_t, aq_t, sm_r, ak_r, av_t, *cast_ops)
    return out[0], tuple(out[1:])


GLA_ROWS = 256


def _split3(x):
    hi = x.astype(BF16)
    r1 = x - hi.astype(F32)
    mid = r1.astype(BF16)
    lo = (r1 - mid.astype(F32)).astype(BF16)
    return hi, mid, lo


def _gla_body(q_ref, k_ref, v_ref, r_ref, sm_ref, wa_ref, ba_ref, gn_ref, o_ref, st_ref):
    rows, c = GLA_ROWS, GLA_CHUNK

    @pl.when(pl.program_id(1) == 0)
    def _():
        st_ref[...] = jnp.zeros_like(st_ref)

    ga = sm_ref[:, SM_GA:SM_GA + GLA_GATE_RANK]
    z = _dot(ga, wa_ref[...]) + ba_ref[...]
    g = -(jnp.maximum(-z, 0.0) + jnp.log1p(jnp.exp(-jnp.abs(z)))) * (1.0 / GLA_GATE_NORM)
    ri = lax.broadcasted_iota(jnp.int32, (rows, rows), 0)
    ci = lax.broadcasted_iota(jnp.int32, (rows, rows), 1)
    tri = jnp.where((ri // c == ci // c) & (ci <= ri), 1.0, 0.0).astype(BF16)
    hi, mid, lo = _split3(g)
    b = _dot(tri, hi) + _dot(tri, mid) + _dot(tri, lo)
    low = lax.broadcasted_iota(jnp.int32, (c, c), 1) <= lax.broadcasted_iota(jnp.int32, (c, c), 0)
    gn = gn_ref[...]
    for ch in range(rows // c):
        sl = slice(ch * c, (ch + 1) * c)
        for h in range(GLA_HEADS):
            ks = slice(h * GLA_DK, (h + 1) * GLA_DK)
            vs = slice(h * GLA_DV, (h + 1) * GLA_DV)
            bc = b[sl, ks]
            bl = bc[c - 1:c, :]
            q = q_ref[sl, ks].astype(F32) * (GLA_DK ** -0.5)
            k = k_ref[sl, ks].astype(F32)
            v = v_ref[sl, vs]
            qe = (q * jnp.exp(bc - bl)).astype(BF16)
            kd = (k * jnp.exp(bl - bc)).astype(BF16)
            qb = (q * jnp.exp(bc)).astype(BF16)
            a = jnp.where(low, _dot_nt(qe, kd), 0.0).astype(BF16)
            st = st_ref[h]
            o = _dot(a, v) + _dot_nt(qb, st.astype(BF16))
            vt = v.astype(F32).T.astype(BF16)
            st_ref[h] = st * jnp.exp(bl) + _dot(vt, kd)
            o = _rms(o, gn)
            gr = r_ref[sl, vs].astype(F32)
            o_ref[sl, vs] = (o * (gr * jax.nn.sigmoid(gr))).astype(BF16)


def _gla(proj, wa2, ba, gn, batch, seq):
    n = proj.shape[0]
    rows = GLA_ROWS
    ng = seq // rows
    rmap = lambda off, w: (lambda b, i: (b * ng + i, off // w))
    return pl.pallas_call(
        _gla_body,
        grid=(batch, ng),
        in_specs=[
            pl.BlockSpec((rows, GLA_K), rmap(COL_GQ, GLA_K)),
            pl.BlockSpec((rows, GLA_K), rmap(COL_GK, GLA_K)),
            pl.BlockSpec((rows, GLA_V), rmap(COL_GV, GLA_V)),
            pl.BlockSpec((rows, GLA_V), rmap(COL_GR, GLA_V)),
            pl.BlockSpec((rows, LANES), rmap(COL_SM, LANES)),
            pl.BlockSpec((GLA_GATE_RANK, GLA_K), lambda b, i: (0, 0)),
            pl.BlockSpec((1, GLA_K), lambda b, i: (0, 0)),
            pl.BlockSpec((1, GLA_DV), lambda b, i: (0, 0)),
        ],
        out_specs=pl.BlockSpec((rows, GLA_V), lambda b, i: (b * ng + i, 0)),
        out_shape=jax.ShapeDtypeStruct((n, GLA_V), BF16),
        scratch_shapes=[pltpu.VMEM((GLA_HEADS, GLA_DV, GLA_DK), F32)],
        compiler_params=_cparams(("parallel", "arbitrary")),
        name="gla",
    )(proj, proj, proj, proj, proj, wa2, ba, gn)


def _merge_body(x_ref, oa_ref, og_ref, ma_ref, mg_ref, wua_ref, wug_ref, wo_ref, o_ref):
    ya = jax.nn.sigmoid(ma_ref[...].astype(F32)) * _dot(oa_ref[...], wua_ref[...])
    yg = jax.nn.sigmoid(mg_ref[...].astype(F32)) * _dot(og_ref[...], wug_ref[...])
    y = (ya + yg).astype(BF16)
    o_ref[...] = x_ref[...] + _dot(y, wo_ref[...])


def _merge(x, o_attn, o_gla, proj, wua, wug, wo, layer, *, tm=256):
    n, d = x.shape
    const = lambda i: (layer, 0, 0)
    return pl.pallas_call(
        _merge_body,
        grid=(n // tm,),
        in_specs=[
            pl.BlockSpec((tm, d), lambda i: (i, 0)),
            pl.BlockSpec((tm, ATTN_Q), lambda i: (i, 0)),
            pl.BlockSpec((tm, GLA_V), lambda i: (i, 0)),
            pl.BlockSpec((tm, d), lambda i: (i, COL_MA // D_MODEL)),
            pl.BlockSpec((tm, d), lambda i: (i, COL_MG // D_MODEL)),
            pl.BlockSpec((None, ATTN_Q, d), const),
            pl.BlockSpec((None, GLA_V, d), const),
            pl.BlockSpec((None, d, d), const),
        ],
        out_specs=pl.BlockSpec((tm, d), lambda i: (i, 0)),
        out_shape=jax.ShapeDtypeStruct((n, d), F32),
        compiler_params=_cparams(("parallel",)),
        name="merge",
    )(x, o_attn, o_gla, proj, proj, wua, wug, wo)


def kernel(x, ffn1_norm, ffn1_w_gate, ffn1_w_up, ffn1_w_down, mix_norm, w_in, gla_w_a2, gla_b_a, gla_norm, w_up_attn, w_up_gla, w_out, ffn2_norm, ffn2_w_gate, ffn2_w_up, ffn2_w_down, final_norm):
    batch, seq, d = x.shape
    depth = w_in.shape[0]
    tabs_a = _rope_tables(seq, ATTN_HEAD_DIM, ATTN_ROT)
    tabs_i = _rope_tables(seq, IDX_HEAD_DIM, IDX_ROT)
    fg = final_norm.reshape(1, d)
    g1, gm, g2 = (a.reshape(depth, 1, d) for a in (ffn1_norm, mix_norm, ffn2_norm))
    w_in_b = _prep_w_in(w_in)
    ffn1_w, ffn2_w, merge_w = (ffn1_w_gate, ffn1_w_up, ffn1_w_down), (ffn2_w_gate, ffn2_w_up, ffn2_w_down), \
        (w_up_attn, w_up_gla, w_out)
    ffn1_b = {0: (_to_bf16(ffn1_w_gate, 256, 0), _to_bf16(ffn1_w_up, 256, 0), _to_bf16(ffn1_w_down, 512, 0))}
    hosted = [(w, l) for l in range(1, depth) for w in ffn1_w] + \
             [(w, l) for l in range(depth) for w in ffn2_w + merge_w]
    ffn2_b, merge_b = {}, {}
    h = x.reshape(batch * seq, d)
    for l in range(depth):
        h = _ffn(h, g1, *ffn1_b[l], fg, l, 0, final_norm=False)
        proj = _proj(h, gm, w_in_b, l)
        aq_t, iq_t, sm_t, ak_r, sm_r, av_t = _rope(proj, seq, tabs_a, tabs_i)
        o_attn, cast = _dsa(aq_t, iq_t, sm_t, ak_r, sm_r, av_t, batch, seq, hosted if l == 0 else ())
        if l == 0:
            cast = list(cast)
            for k in range(1, depth):
                ffn1_b[k], cast = tuple(cast[:3]), cast[3:]
            for k in range(depth):
                ffn2_b[k], merge_b[k], cast = tuple(cast[:3]), tuple(cast[3:6]), cast[6:]
        o_gla = _gla(proj, gla_w_a2[l].astype(BF16), gla_b_a[l].reshape(1, GLA_K),
                     gla_norm[l].reshape(1, GLA_DV), batch, seq)
        h = _merge(h, o_attn, o_gla, proj, *merge_b[l], 0)
        h = _ffn(h, g2, *ffn2_b[l], fg, l, 0, final_norm=(l == depth - 1))
    return h.reshape(batch, seq, d)
```

```python
import functools

import jax
import jax.numpy as jnp
import numpy as np
from jax import lax
from jax.experimental import pallas as pl
from jax.experimental.pallas import tpu as pltpu

F32 = jnp.float32
BF16 = jnp.bfloat16

D_MODEL = 2048
D_FF = 5632
FFN_RES = 0.5
ATTN_HEADS = 8
ATTN_KV_HEADS = 2
ATTN_HEAD_DIM = 128
ATTN_REP = ATTN_HEADS // ATTN_KV_HEADS
IDX_HEADS = 16
IDX_HEAD_DIM = 64
TOPK_MAX = 256
GLA_HEADS = 4
GLA_DK = 128
GLA_DV = 256
GLA_GATE_RANK = 16
GLA_GATE_NORM = 16.0
GLA_CHUNK = 64
ROPE_THETA = 500000.0
ATTN_ROT = ATTN_HEAD_DIM // 4
IDX_ROT = IDX_HEAD_DIM // 4
EPS = 1e-6
NEG_INF = -1e30
INT_MIN = -(2 ** 31)
PACK16 = 16

ATTN_Q = ATTN_HEADS * ATTN_HEAD_DIM
ATTN_KV = ATTN_KV_HEADS * ATTN_HEAD_DIM
IDX_Q = IDX_HEADS * IDX_HEAD_DIM
GLA_K = GLA_HEADS * GLA_DK
GLA_V = GLA_HEADS * GLA_DV
IN_SPLITS = (ATTN_Q, ATTN_KV, ATTN_KV, IDX_Q, IDX_HEAD_DIM, IDX_HEADS,
             GLA_K, GLA_K, GLA_V, GLA_GATE_RANK, GLA_V, D_MODEL, D_MODEL)
IN_OFFSETS = tuple(int(v) for v in np.cumsum(IN_SPLITS)[:-1])
IN_NAMES = ("aq", "ak", "av", "iq", "ik", "iw", "gq", "gk", "gv", "ga", "gr", "m_a", "m_g")

LANES = 128
COL_AQ = 0
COL_IQ = COL_AQ + ATTN_Q
COL_GV = COL_IQ + IDX_Q
COL_GR = COL_GV + GLA_V
COL_MA = COL_GR + GLA_V
COL_MG = COL_MA + D_MODEL
COL_GQ = COL_MG + D_MODEL
COL_GK = COL_GQ + GLA_K
COL_AK = COL_GK + GLA_K
COL_AV = COL_AK + ATTN_KV
COL_SM = COL_AV + ATTN_KV
SM_IK = 0
SM_IW = SM_IK + IDX_HEAD_DIM
SM_GA = SM_IW + IDX_HEADS
PROJ_TN = 1024
PROJ_COLS = -(-(COL_SM + LANES) // PROJ_TN) * PROJ_TN

DSA_TQ = 128
DSA_TK = 256
BISECT_CHECK = 28
VT_ROWS = ATTN_HEAD_DIM + PACK16
LOG2E = 1.4426950408889634

FFN_SLAB = 512

VMEM_LIMIT = 62 * 1024 * 1024


def _cparams(sem):
    return pltpu.CompilerParams(dimension_semantics=sem, vmem_limit_bytes=VMEM_LIMIT)


def _dot(a, b):
    return jnp.dot(a, b, preferred_element_type=F32)


def _dot_nt(a, b):
    return lax.dot_general(a, b, (((1,), (1,)), ((), ())), preferred_element_type=F32)


def _rms(x, g):
    return x * lax.rsqrt(jnp.mean(x * x, axis=-1, keepdims=True) + EPS) * g


def _ffn_body(x_ref, g_ref, wg_ref, wu_ref, wd_ref, fg_ref, o_ref, h_ref, *, final_norm):
    f = pl.program_id(1)

    slabs = [slice(r, r + FFN_SLAB) for r in range(0, x_ref.shape[0], FFN_SLAB)]

    @pl.when(f == 0)
    def _():
        for rows in slabs:
            h_ref[rows, :] = _rms(x_ref[rows, :], g_ref[...]).astype(BF16)
        o_ref[...] = jnp.zeros_like(o_ref)

    for rows in slabs:
        h = h_ref[rows, :]
        gate = _dot(h, wg_ref[...])
        up = _dot(h, wu_ref[...])
        act = (gate * jax.nn.sigmoid(gate) * up).astype(BF16)
        o_ref[rows, :] += _dot(act, wd_ref[...])

    @pl.when(f == pl.num_programs(1) - 1)
    def _():
        for rows in slabs:
            y = x_ref[rows, :] + FFN_RES * o_ref[rows, :]
            if final_norm:
                y = _rms(y, fg_ref[...])
            o_ref[rows, :] = y


def _ffn(x, g, wg, wu, wd, fg, layer, wlayer, *, final_norm, tm=1024, tf=512):
    n, d = x.shape
    f = wg.shape[2]
    return pl.pallas_call(
        functools.partial(_ffn_body, final_norm=final_norm),
        grid=(n // tm, f // tf),
        in_specs=[
            pl.BlockSpec((tm, d), lambda i, j: (i, 0)),
            pl.BlockSpec((None, 1, d), lambda i, j: (layer, 0, 0)),
            pl.BlockSpec((None, d, tf), lambda i, j: (wlayer, 0, j)),
            pl.BlockSpec((None, d, tf), lambda i, j: (wlayer, 0, j)),
            pl.BlockSpec((None, tf, d), lambda i, j: (wlayer, j, 0)),
            pl.BlockSpec((1, d), lambda i, j: (0, 0)),
        ],
        out_specs=pl.BlockSpec((tm, d), lambda i, j: (i, 0)),
        out_shape=jax.ShapeDtypeStruct((n, d), F32),
        scratch_shapes=[pltpu.VMEM((tm, d), BF16)],
        compiler_params=_cparams(("parallel", "arbitrary")),
        name="ffn",
    )(x, g, wg, wu, wd, fg)


def _cast_body(w_ref, o_ref):
    o_ref[...] = w_ref[...].astype(o_ref.dtype)


def _to_bf16(w, tr, layer):
    _, r, c = w.shape
    return pl.pallas_call(
        _cast_body,
        grid=(r // tr,),
        in_specs=[pl.BlockSpec((None, tr, c), lambda i: (layer, i, 0))],
        out_specs=pl.BlockSpec((None, tr, c), lambda i: (0, i, 0)),
        out_shape=jax.ShapeDtypeStruct((1, r, c), BF16),
        compiler_params=_cparams(("parallel",)),
        name="cast",
    )(w)


def _w_in_body(w_ref, o_ref):
    w = w_ref[...]
    seg = {name: w[:, off:off + width] for name, off, width in
           zip(IN_NAMES, (0,) + IN_OFFSETS, IN_SPLITS)}
    tr = w.shape[0]
    small_pad = jnp.zeros((tr, LANES - IDX_HEAD_DIM - IDX_HEADS - GLA_GATE_RANK), w.dtype)
    tail = jnp.zeros((tr, PROJ_COLS - COL_SM - LANES), w.dtype)
    order = ("aq", "iq", "gv", "gr", "m_a", "m_g", "gq", "gk", "ak", "av", "ik", "iw", "ga")
    o_ref[...] = jnp.concatenate([seg[k] for k in order] + [small_pad, tail], axis=1).astype(o_ref.dtype)


def _prep_w_in(w, tr=256):
    depth, d, c = w.shape
    return pl.pallas_call(
        _w_in_body,
        grid=(depth, d // tr),
        in_specs=[pl.BlockSpec((None, tr, c), lambda l, i: (l, i, 0))],
        out_specs=pl.BlockSpec((None, tr, PROJ_COLS), lambda l, i: (l, i, 0)),
        out_shape=jax.ShapeDtypeStruct((depth, d, PROJ_COLS), BF16),
        compiler_params=_cparams(("parallel", "parallel")),
        name="w_in_prep",
    )(w)


def _proj_body(x_ref, g_ref, w_ref, o_ref, h_ref):
    @pl.when(pl.program_id(1) == 0)
    def _():
        h_ref[...] = _rms(x_ref[...], g_ref[...]).astype(BF16)

    o_ref[...] = _dot(h_ref[...], w_ref[...]).astype(o_ref.dtype)


def _proj(x, g, w, layer, *, tm=1024, tn=2 * PROJ_TN):
    n, d = x.shape
    c = w.shape[2]
    return pl.pallas_call(
        _proj_body,
        grid=(n // tm, c // tn),
        in_specs=[
            pl.BlockSpec((tm, d), lambda i, j: (i, 0)),
            pl.BlockSpec((None, 1, d), lambda i, j: (layer, 0, 0)),
            pl.BlockSpec((None, d, tn), lambda i, j: (layer, 0, j)),
        ],
        out_specs=pl.BlockSpec((tm, tn), lambda i, j: (i, j)),
        out_shape=jax.ShapeDtypeStruct((n, c), BF16),
        scratch_shapes=[pltpu.VMEM((tm, d), BF16)],
        compiler_params=_cparams(("parallel", "arbitrary")),
        name="proj",
    )(x, g, w)


def _rope_tables(length, head_dim, rot):
    half = rot // 2
    inv = ROPE_THETA ** (-jnp.arange(0, rot, 2, dtype=F32) / rot)
    ang = jnp.arange(length, dtype=F32)[:, None] * inv[None, :]
    cos, sin = jnp.cos(ang), jnp.sin(ang)
    zeros = jnp.zeros((length, head_dim - rot), F32)
    zh = jnp.zeros((length, half), F32)
    c = jnp.concatenate([cos, cos, zeros + 1.0], axis=1)
    s1 = jnp.concatenate([-sin, zh, zeros], axis=1)
    s2 = jnp.concatenate([zh, sin, zeros], axis=1)
    reps = LANES // head_dim
    return tuple(jnp.tile(t, (1, reps)) for t in (c, s1, s2))


def _rope_lanes(x, c, s1, s2, half):
    return x * c + pltpu.roll(x, LANES - half, 1) * s1 + pltpu.roll(x, half, 1) * s2


def _store_t(dst_ref, rows, val):
    vt = val.T.astype(dst_ref.dtype)
    for c in range(vt.shape[1] // DSA_TQ):
        dst_ref[c, rows, :] = vt[:, c * DSA_TQ:(c + 1) * DSA_TQ]


def _rope_body(aq_ref, iq_ref, ak_ref, av_ref, sm_ref, ca_ref, sa1_ref, sa2_ref, ci_ref, si1_ref, si2_ref,
               aqt_ref, iqt_ref, smt_ref, ako_ref, smo_ref, avt_ref):
    ca, sa1, sa2 = ca_ref[...], sa1_ref[...], sa2_ref[...]
    ci, si1, si2 = ci_ref[...], si1_ref[...], si2_ref[...]
    ha, hi = ATTN_ROT // 2, IDX_ROT // 2
    tk = DSA_TK
    for j in range(ATTN_Q // LANES):
        sl = slice(j * LANES, (j + 1) * LANES)
        r = _rope_lanes(aq_ref[:, sl].astype(F32), ca, sa1, sa2, ha)
        _store_t(aqt_ref, sl, r * (ATTN_HEAD_DIM ** -0.5 * LOG2E))
    for j in range(IDX_Q // LANES):
        sl = slice(j * LANES, (j + 1) * LANES)
        r = _rope_lanes(iq_ref[:, sl].astype(F32), ci, si1, si2, hi)
        _store_t(iqt_ref, sl, r * (IDX_HEAD_DIM ** -0.5))
    for j in range(ATTN_KV // LANES):
        sl = slice(j * LANES, (j + 1) * LANES)
        ako_ref[:, sl] = _rope_lanes(ak_ref[:, sl].astype(F32), ca, sa1, sa2, ha).astype(BF16)
        vt = av_ref[:, sl].astype(F32).T
        for c in range(vt.shape[1] // tk):
            avt_ref[c, j * VT_ROWS:j * VT_ROWS + LANES, :] = vt[:, c * tk:(c + 1) * tk].astype(BF16)
            avt_ref[c, j * VT_ROWS + LANES:(j + 1) * VT_ROWS, :] = jnp.ones((VT_ROWS - LANES, tk), BF16)
    sm = sm_ref[...].astype(F32)
    lane = lax.broadcasted_iota(jnp.int32, sm.shape, 1)
    sm = jnp.where(lane < IDX_HEAD_DIM, _rope_lanes(sm, ci, si1, si2, hi), sm)
    smo_ref[...] = sm.astype(BF16)
    _store_t(smt_ref, slice(0, LANES), sm)


def _rope(proj, seq, tabs_a, tabs_i, *, tr=512):
    n = proj.shape[0]
    nl = seq // tr
    tk = DSA_TK
    tab = pl.BlockSpec((tr, LANES), lambda i: (i % nl, 0))
    return pl.pallas_call(
        _rope_body,
        grid=(n // tr,),
        in_specs=[
            pl.BlockSpec((tr, ATTN_Q), lambda i: (i, COL_AQ // ATTN_Q)),
            pl.BlockSpec((tr, IDX_Q), lambda i: (i, COL_IQ // IDX_Q)),
            pl.BlockSpec((tr, ATTN_KV), lambda i: (i, COL_AK // ATTN_KV)),
            pl.BlockSpec((tr, ATTN_KV), lambda i: (i, COL_AV // ATTN_KV)),
            pl.BlockSpec((tr, LANES), lambda i: (i, COL_SM // LANES)),
            tab, tab, tab, tab, tab, tab,
        ],
        out_specs=[
            pl.BlockSpec((tr // DSA_TQ, ATTN_Q, DSA_TQ), lambda i: (i, 0, 0)),
            pl.BlockSpec((tr // DSA_TQ, IDX_Q, DSA_TQ), lambda i: (i, 0, 0)),
            pl.BlockSpec((tr // DSA_TQ, LANES, DSA_TQ), lambda i: (i, 0, 0)),
            pl.BlockSpec((tr, ATTN_KV), lambda i: (i, 0)),
            pl.BlockSpec((tr, LANES), lambda i: (i, 0)),
            pl.BlockSpec((tr // tk, ATTN_KV_HEADS * VT_ROWS, tk), lambda i: (i, 0, 0)),
        ],
        out_shape=[
            jax.ShapeDtypeStruct((n // DSA_TQ, ATTN_Q, DSA_TQ), BF16),
            jax.ShapeDtypeStruct((n // DSA_TQ, IDX_Q, DSA_TQ), BF16),
            jax.ShapeDtypeStruct((n // DSA_TQ, LANES, DSA_TQ), BF16),
            jax.ShapeDtypeStruct((n, ATTN_KV), BF16),
            jax.ShapeDtypeStruct((n, LANES), BF16),
            jax.ShapeDtypeStruct((n // tk, ATTN_KV_HEADS * VT_ROWS, tk), BF16),
        ],
        compiler_params=_cparams(("parallel",)),
        name="rope",
    )(proj, proj, proj, proj, proj, *tabs_a, *tabs_i)


def _sortable(x):
    i = pltpu.bitcast(x + 0.0, jnp.int32)
    return jnp.where(i < 0, i ^ jnp.int32(0x7FFFFFFF), i)


def _tree_sum(xs):
    while len(xs) > 1:
        xs = [xs[i] + xs[i + 1] for i in range(0, len(xs) - 1, 2)] + ([xs[-1]] if len(xs) % 2 else [])
    return xs[0]


def _colsum(x):
    r, c = x.shape
    return jnp.sum(jnp.sum(x.reshape(r // 8, 8, c), axis=0), axis=0, keepdims=True)


def _dsa_body(*refs, topk, seq, n_cast):
    iqt_ref, smt_ref, aqt_ref, smk_ref, ak_ref, avt_ref = refs[:6]
    cast_in, o_ref, cast_out = refs[6:6 + n_cast], refs[6 + n_cast], refs[7 + n_cast:7 + 2 * n_cast]
    key_ref, m_ref, al_ref, acc_ref, s0_ref, s1_ref, p0_ref, p1_ref = refs[7 + 2 * n_cast:]
    for w_ref, wb_ref in zip(cast_in, cast_out):
        wb_ref[...] = w_ref[...].astype(BF16)
    tq, tk = DSA_TQ, DSA_TK
    qi = pl.program_id(1)
    t0 = qi * tq
    nkb = (t0 + tq + tk - 1) // tk
    iwt = smt_ref[SM_IW:SM_IW + IDX_HEADS, :].astype(F32) * (IDX_HEADS ** -0.5)
    qpos = lax.broadcasted_iota(jnp.int32, (tk, tq), 1) + t0
    kiota = lax.broadcasted_iota(jnp.int32, (tk, tq), 0)

    npair = (nkb + 1) // 2
    odd = nkb % 2 == 1

    def idx_block(kb):
        k0 = pl.multiple_of(kb * tk, tk)
        ik = smk_ref[pl.ds(k0, tk), :][:, SM_IK:SM_IK + IDX_HEAD_DIM]
        score = jnp.zeros((tk, tq), F32)
        for h in range(0, IDX_HEADS, 2):
            qq = jnp.concatenate([iqt_ref[(h + u) * IDX_HEAD_DIM:(h + u + 1) * IDX_HEAD_DIM, :]
                                  for u in range(2)], axis=1)
            rel = jnp.maximum(_dot(ik, qq), 0.0)
            score = score + rel[:, :tq] * iwt[h:h + 1, :] + rel[:, tq:] * iwt[h + 1:h + 2, :]
        key_ref[kb] = jnp.where(kiota + kb * tk <= qpos, _sortable(score), jnp.int32(INT_MIN))

    def idx_pair(i, carry):
        idx_block(2 * i)
        idx_block(2 * i + 1)
        return carry

    lax.fori_loop(0, nkb // 2, idx_pair, 0)

    @pl.when(odd)
    def _():
        idx_block(nkb - 1)
        key_ref[nkb] = jnp.full((tk, tq), INT_MIN, jnp.int32)

    def counts(*preds):
        def pair(i, cs):
            for u in range(2):
                kb = 2 * i + u
                key = key_ref[kb]
                hits = [jnp.where(p(key, kb), 1, 0) for p in preds]
                cs = tuple(c + _tree_sum([h[j:j + 8] for j in range(0, tk, 8)]) for c, h in zip(cs, hits))
            return cs
        cs = lax.fori_loop(0, npair, pair, tuple(jnp.zeros((8, tq), jnp.int32) for _ in preds))
        return tuple(jnp.sum(c.astype(F32), axis=0, keepdims=True) for c in cs)

    def count(pred):
        return counts(pred)[0]

    kf = float(topk)

    def bisect(it, carry):
        t, n_t = carry
        cand = t + lax.shift_left(jnp.int32(1), jnp.int32(31) - it)
        n_c = count(lambda k, kb: k >= cand)
        keep = n_c >= kf
        return jnp.where(keep, cand, t), jnp.where(keep, n_c, n_t)

    start = (jnp.full((1, tq), INT_MIN, jnp.int32), jnp.full((1, tq), float(seq + tk), F32))
    thr, n_thr = lax.fori_loop(0, BISECT_CHECK, bisect, start)
    settled = (n_thr == kf) | (thr == jnp.int32(INT_MIN))
    thr = lax.cond(jnp.min(jnp.where(settled, 1.0, 0.0)) > 0.0, lambda: thr,
                   lambda: lax.fori_loop(BISECT_CHECK, 32, bisect, (thr, n_thr))[0])

    n_gt, n_ge = counts(lambda k, kb: k > thr, lambda k, kb: k >= thr)
    need = kf - n_gt
    tied = (n_ge > kf) & (thr > jnp.int32(INT_MIN))

    def tie_break():
        def step(it, x):
            cand = x + lax.shift_left(jnp.int32(1), jnp.int32(seq.bit_length() - 1) - it)
            below = count(lambda k, kb: (k == thr) & (kiota + kb * tk < cand))
            return jnp.where(below < need, cand, x)
        return lax.fori_loop(0, seq.bit_length(), step, jnp.zeros((1, tq), jnp.int32))

    jtie = lax.cond(jnp.max(jnp.where(tied, 1.0, 0.0)) > 0.0, tie_break,
                    lambda: jnp.zeros((1, tq), jnp.int32))
    jmax = jnp.where(tied, jtie, jnp.where(thr > jnp.int32(INT_MIN), jnp.int32(seq), jnp.int32(-1)))

    last = nkb - 1

    def scores(kb, s_ref):
        kb = jnp.minimum(kb, last)
        kc = jnp.minimum(kb, seq // tk - 1)
        k0 = pl.multiple_of(kc * tk, tk)
        key = key_ref[kb]
        sel = (key > thr) | ((key == thr) & (kiota + kb * tk <= jmax))
        bias = jnp.where(sel, 0.0, NEG_INF)
        bias = jnp.concatenate([bias] * ATTN_REP, axis=1)
        for g in range(ATTN_KV_HEADS):
            kg = ak_ref[pl.ds(k0, tk), g * ATTN_HEAD_DIM:(g + 1) * ATTN_HEAD_DIM]
            qt = jnp.concatenate(
                [aqt_ref[(g * ATTN_REP + r) * ATTN_HEAD_DIM:(g * ATTN_REP + r + 1) * ATTN_HEAD_DIM, :]
                 for r in range(ATTN_REP)], axis=1)
            s_ref[g] = _dot(kg, qt) + bias

    def softmax(s_ref, p_ref):
        for g in range(ATTN_KV_HEADS):
            for r in range(ATTN_REP):
                cols = slice(r * tq, (r + 1) * tq)
                s = s_ref[g, :, cols]
                m_old = m_ref[g, :, cols]
                m_new = jnp.maximum(m_old, jnp.max(s, axis=0, keepdims=True))
                p_ref[g, :, cols] = jnp.exp2(s - m_new).astype(BF16)
                al_ref[g, :, cols] = jnp.exp2(m_old - m_new)
                m_ref[g, :, cols] = m_new

    def values(kb, p_ref):
        kc = jnp.clip(kb, 0, seq // tk - 1)
        for g in range(ATTN_KV_HEADS):
            vt = avt_ref[kc, g * VT_ROWS:(g + 1) * VT_ROWS, :]
            acc_ref[g] = al_ref[g] * acc_ref[g] + _dot(vt, p_ref[g])

    m_ref[...] = jnp.full_like(m_ref, NEG_INF)
    acc_ref[...] = jnp.zeros_like(acc_ref)
    al_ref[...] = jnp.ones_like(al_ref)
    p1_ref[...] = jnp.zeros_like(p1_ref)
    scores(0, s0_ref)

    def att_pair(i, carry):
        j = 2 * i
        values(j - 1, p1_ref)
        softmax(s0_ref, p0_ref)
        scores(j + 1, s1_ref)
        values(j, p0_ref)
        softmax(s1_ref, p1_ref)
        scores(j + 2, s0_ref)
        return carry

    lax.fori_loop(0, nkb // 2, att_pair, 0)

    @pl.when(odd)
    def _():
        values(last - 1, p1_ref)
        softmax(s0_ref, p0_ref)
        values(last, p0_ref)

    @pl.when(jnp.logical_not(odd))
    def _():
        values(last, p1_ref)
    for g in range(ATTN_KV_HEADS):
        acc = acc_ref[g]
        ot = acc[:ATTN_HEAD_DIM] / acc[ATTN_HEAD_DIM:ATTN_HEAD_DIM + 1]
        for r in range(ATTN_REP):
            hh = g * ATTN_REP + r
            o_ref[:, hh * ATTN_HEAD_DIM:(hh + 1) * ATTN_HEAD_DIM] = ot[:, r * tq:(r + 1) * tq].T.astype(BF16)


def _dsa(aq_t, iq_t, sm_t, ak_r, sm_r, av_t, batch, seq, casts=()):
    n = ak_r.shape[0]
    tq, tk = DSA_TQ, DSA_TK
    nq = seq // tq
    nkb = seq // tk
    topk = min(TOPK_MAX, seq // 4)
    qmap = lambda b, i: (b * nq + i, 0, 0)
    steps = batch * nq
    cast_ops, cast_in_specs, cast_out_specs, cast_out_shapes = [], [], [], []
    for w, layer in casts:
        _, r, c = w.shape
        slab = next(s for s in range(PACK16, r + 1, PACK16) if r % s == 0 and s * steps >= r)
        last = r // slab - 1
        cast_ops.append(w)
        cast_in_specs.append(pl.BlockSpec(
            (None, slab, c), lambda b, i, layer=layer, last=last: (layer, jnp.minimum(b * nq + i, last), 0)))
        cast_out_specs.append(pl.BlockSpec(
            (None, slab, c), lambda b, i, last=last: (0, jnp.minimum(b * nq + i, last), 0)))
        cast_out_shapes.append(jax.ShapeDtypeStruct((1, r, c), BF16))
    out = pl.pallas_call(
        functools.partial(_dsa_body, topk=topk, seq=seq, n_cast=len(casts)),
        grid=(batch, nq),
        in_specs=[
            pl.BlockSpec((None, IDX_Q, tq), qmap),
            pl.BlockSpec((None, LANES, tq), qmap),
            pl.BlockSpec((None, ATTN_Q, tq), qmap),
            pl.BlockSpec((seq, LANES), lambda b, i: (b, 0)),
            pl.BlockSpec((seq, ATTN_KV), lambda b, i: (b, 0)),
            pl.BlockSpec((nkb, ATTN_KV_HEADS * VT_ROWS, tk), lambda b, i: (b, 0, 0)),
        ] + cast_in_specs,
        out_specs=[pl.BlockSpec((tq, ATTN_Q), lambda b, i: (b * nq + i, 0))] + cast_out_specs,
        out_shape=[jax.ShapeDtypeStruct((n, ATTN_Q), BF16)] + cast_out_shapes,
        scratch_shapes=[
            pltpu.VMEM((nkb + nkb % 2, tk, tq), jnp.int32),
            pltpu.VMEM((ATTN_KV_HEADS, 1, ATTN_REP * tq), F32),
            pltpu.VMEM((ATTN_KV_HEADS, 1, ATTN_REP * tq), F32),
            pltpu.VMEM((ATTN_KV_HEADS, VT_ROWS, ATTN_REP * tq), F32),
            pltpu.VMEM((ATTN_KV_HEADS, tk, ATTN_REP * tq), F32),
            pltpu.VMEM((ATTN_KV_HEADS, tk, ATTN_REP * tq), F32),
            pltpu.VMEM((ATTN_KV_HEADS, tk, ATTN_REP * tq), BF16),
            pltpu.VMEM((ATTN_KV_HEADS, tk, ATTN_REP * tq), BF16),
        ],
        compiler_params=_cparams(("arbitrary", "arbitrary")),
        name="dsa",
    )(iq_t, sm_t, aq_t, sm_r, ak_r, av_t, *cast_ops)
    return out[0], tuple(out[1:])


GLA_ROWS = 256


def _split3(x):
    hi = x.astype(BF16)
    r1 = x - hi.astype(F32)
    mid = r1.astype(BF16)
    lo = (r1 - mid.astype(F32)).astype(BF16)
    return hi, mid, lo


def _gla_body(q_ref, k_ref, v_ref, r_ref, sm_ref, wa_ref, ba_ref, gn_ref, o_ref, st_ref):
    rows, c = GLA_ROWS, GLA_CHUNK

    @pl.when(pl.program_id(1) == 0)
    def _():
        st_ref[...] = jnp.zeros_like(st_ref)

    ga = sm_ref[:, SM_GA:SM_GA + GLA_GATE_RANK]
    z = _dot(ga, wa_ref[...]) + ba_ref[...]
    g = -(jnp.maximum(-z, 0.0) + jnp.log1p(jnp.exp(-jnp.abs(z)))) * (1.0 / GLA_GATE_NORM)
    ri = lax.broadcasted_iota(jnp.int32, (rows, rows), 0)
    ci = lax.broadcasted_iota(jnp.int32, (rows, rows), 1)
    tri = jnp.where((ri // c == ci // c) & (ci <= ri), 1.0, 0.0).astype(BF16)
    hi, mid, lo = _split3(g)
    b = _dot(tri, hi) + _dot(tri, mid) + _dot(tri, lo)
    low = lax.broadcasted_iota(jnp.int32, (c, c), 1) <= lax.broadcasted_iota(jnp.int32, (c, c), 0)
    gn = gn_ref[...]
    for ch in range(rows // c):
        sl = slice(ch * c, (ch + 1) * c)
        for h in range(GLA_HEADS):
            ks = slice(h * GLA_DK, (h + 1) * GLA_DK)
            vs = slice(h * GLA_DV, (h + 1) * GLA_DV)
            bc = b[sl, ks]
            bl = bc[c - 1:c, :]
            q = q_ref[sl, ks].astype(F32) * (GLA_DK ** -0.5)
            k = k_ref[sl, ks].astype(F32)
            v = v_ref[sl, vs]
            qe = (q * jnp.exp(bc - bl)).astype(BF16)
            kd = (k * jnp.exp(bl - bc)).astype(BF16)
            qb = (q * jnp.exp(bc)).astype(BF16)
            a = jnp.where(low, _dot_nt(qe, kd), 0.0).astype(BF16)
            st = st_ref[h]
            o = _dot(a, v) + _dot_nt(qb, st.astype(BF16))
            vt = v.astype(F32).T.astype(BF16)
            st_ref[h] = st * jnp.exp(bl) + _dot(vt, kd)
            o = _rms(o, gn)
            gr = r_ref[sl, vs].astype(F32)
            o_ref[sl, vs] = (o * (gr * jax.nn.sigmoid(gr))).astype(BF16)


def _gla(proj, wa2, ba, gn, batch, seq):
    n = proj.shape[0]
    rows = GLA_ROWS
    ng = seq // rows
    rmap = lambda off, w: (lambda b, i: (b * ng + i, off // w))
    return pl.pallas_call(
        _gla_body,
        grid=(batch, ng),
        in_specs=[
            pl.BlockSpec((rows, GLA_K), rmap(COL_GQ, GLA_K)),
            pl.BlockSpec((rows, GLA_K), rmap(COL_GK, GLA_K)),
            pl.BlockSpec((rows, GLA_V), rmap(COL_GV, GLA_V)),
            pl.BlockSpec((rows, GLA_V), rmap(COL_GR, GLA_V)),
            pl.BlockSpec((rows, LANES), rmap(COL_SM, LANES)),
            pl.BlockSpec((GLA_GATE_RANK, GLA_K), lambda b, i: (0, 0)),
            pl.BlockSpec((1, GLA_K), lambda b, i: (0, 0)),
            pl.BlockSpec((1, GLA_DV), lambda b, i: (0, 0)),
        ],
        out_specs=pl.BlockSpec((rows, GLA_V), lambda b, i: (b * ng + i, 0)),
        out_shape=jax.ShapeDtypeStruct((n, GLA_V), BF16),
        scratch_shapes=[pltpu.VMEM((GLA_HEADS, GLA_DV, GLA_DK), F32)],
        compiler_params=_cparams(("parallel", "arbitrary")),
        name="gla",
    )(proj, proj, proj, proj, proj, wa2, ba, gn)


def _merge_body(x_ref, oa_ref, og_ref, ma_ref, mg_ref, wua_ref, wug_ref, wo_ref, o_ref):
    ya = jax.nn.sigmoid(ma_ref[...].astype(F32)) * _dot(oa_ref[...], wua_ref[...])
    yg = jax.nn.sigmoid(mg_ref[...].astype(F32)) * _dot(og_ref[...], wug_ref[...])
    y = (ya + yg).astype(BF16)
    o_ref[...] = x_ref[...] + _dot(y, wo_ref[...])


def _merge(x, o_attn, o_gla, proj, wua, wug, wo, layer, *, tm=256):
    n, d = x.shape
    const = lambda i: (layer, 0, 0)
    return pl.pallas_call(
        _merge_body,
        grid=(n // tm,),
        in_specs=[
            pl.BlockSpec((tm, d), lambda i: (i, 0)),
            pl.BlockSpec((tm, ATTN_Q), lambda i: (i, 0)),
            pl.BlockSpec((tm, GLA_V), lambda i: (i, 0)),
            pl.BlockSpec((tm, d), lambda i: (i, COL_MA // D_MODEL)),
            pl.BlockSpec((tm, d), lambda i: (i, COL_MG // D_MODEL)),
            pl.BlockSpec((None, ATTN_Q, d), const),
            pl.BlockSpec((None, GLA_V, d), const),
            pl.BlockSpec((None, d, d), const),
        ],
        out_specs=pl.BlockSpec((tm, d), lambda i: (i, 0)),
        out_shape=jax.ShapeDtypeStruct((n, d), F32),
        compiler_params=_cparams(("parallel",)),
        name="merge",
    )(x, o_attn, o_gla, proj, proj, wua, wug, wo)


def kernel(x, ffn1_norm, ffn1_w_gate, ffn1_w_up, ffn1_w_down, mix_norm, w_in, gla_w_a2, gla_b_a, gla_norm, w_up_attn, w_up_gla, w_out, ffn2_norm, ffn2_w_gate, ffn2_w_up, ffn2_w_down, final_norm):
    batch, seq, d = x.shape
    depth = w_in.shape[0]
    tabs_a = _rope_tables(seq, ATTN_HEAD_DIM, ATTN_ROT)
    tabs_i = _rope_tables(seq, IDX_HEAD_DIM, IDX_ROT)
    fg = final_norm.reshape(1, d)
    g1, gm, g2 = (a.reshape(depth, 1, d) for a in (ffn1_norm, mix_norm, ffn2_norm))
    w_in_b = _prep_w_in(w_in)
    ffn1_w, ffn2_w, merge_w = (ffn1_w_gate, ffn1_w_up, ffn1_w_down), (ffn2_w_gate, ffn2_w_up, ffn2_w_down), \
        (w_up_attn, w_up_gla, w_out)
    ffn1_b = {0: (_to_bf16(ffn1_w_gate, 256, 0), _to_bf16(ffn1_w_up, 256, 0), _to_bf16(ffn1_w_down, 512, 0))}
    hosted = [(w, l) for l in range(1, depth) for w in ffn1_w] + \
             [(w, l) for l in range(depth) for w in ffn2_w + merge_w]
    ffn2_b, merge_b = {}, {}
    h = x.reshape(batch * seq, d)
    for l in range(depth):
        h = _ffn(h, g1, *ffn1_b[l], fg, l, 0, final_norm=False)
        proj = _proj(h, gm, w_in_b, l)
        aq_t, iq_t, sm_t, ak_r, sm_r, av_t = _rope(proj, seq, tabs_a, tabs_i)
        o_attn, cast = _dsa(aq_t, iq_t, sm_t, ak_r, sm_r, av_t, batch, seq, hosted if l == 0 else ())
        if l == 0:
            cast = list(cast)
            for k in range(1, depth):
                ffn1_b[k], cast = tuple(cast[:3]), cast[3:]
            for k in range(depth):
                ffn2_b[k], merge_b[k], cast = tuple(cast[:3]), tuple(cast[3:6]), cast[6:]
        o_gla = _gla(proj, gla_w_a2[l].astype(BF16), gla_b_a[l].reshape(1, GLA_K),
                     gla_norm[l].reshape(1, GLA_DV), batch, seq)
        h = _merge(h, o_attn, o_gla, proj, *merge_b[l], 0)
        h = _ffn(h, g2, *ffn2_b[l], fg, l, 0, final_norm=(l == depth - 1))
    return h.reshape(batch, seq, d)
```

```python
import functools

import jax
import jax.numpy as jnp
import numpy as np
from jax import lax
from jax.experimental import pallas as pl
from jax.experimental.pallas import tpu as pltpu

F32 = jnp.float32
BF16 = jnp.bfloat16

D_MODEL = 2048
D_FF = 5632
FFN_RES = 0.5
ATTN_HEADS = 8
ATTN_KV_HEADS = 2
ATTN_HEAD_DIM = 128
ATTN_REP = ATTN_HEADS // ATTN_KV_HEADS
IDX_HEADS = 16
IDX_HEAD_DIM = 64
TOPK_MAX = 256
GLA_HEADS = 4
GLA_DK = 128
GLA_DV = 256
GLA_GATE_RANK = 16
GLA_GATE_NORM = 16.0
GLA_CHUNK = 64
ROPE_THETA = 500000.0
ATTN_ROT = ATTN_HEAD_DIM // 4
IDX_ROT = IDX_HEAD_DIM // 4
EPS = 1e-6
NEG_INF = -1e30
INT_MIN = -(2 ** 31)
PACK16 = 16

ATTN_Q = ATTN_HEADS * ATTN_HEAD_DIM
ATTN_KV = ATTN_KV_HEADS * ATTN_HEAD_DIM
IDX_Q = IDX_HEADS * IDX_HEAD_DIM
GLA_K = GLA_HEADS * GLA_DK
GLA_V = GLA_HEADS * GLA_DV
IN_SPLITS = (ATTN_Q, ATTN_KV, ATTN_KV, IDX_Q, IDX_HEAD_DIM, IDX_HEADS,
             GLA_K, GLA_K, GLA_V, GLA_GATE_RANK, GLA_V, D_MODEL, D_MODEL)
IN_OFFSETS = tuple(int(v) for v in np.cumsum(IN_SPLITS)[:-1])
IN_NAMES = ("aq", "ak", "av", "iq", "ik", "iw", "gq", "gk", "gv", "ga", "gr", "m_a", "m_g")

LANES = 128
COL_AQ = 0
COL_IQ = COL_AQ + ATTN_Q
COL_GV = COL_IQ + IDX_Q
COL_GR = COL_GV + GLA_V
COL_MA = COL_GR + GLA_V
COL_MG = COL_MA + D_MODEL
COL_GQ = COL_MG + D_MODEL
COL_GK = COL_GQ + GLA_K
COL_AK = COL_GK + GLA_K
COL_AV = COL_AK + ATTN_KV
COL_SM = COL_AV + ATTN_KV
SM_IK = 0
SM_IW = SM_IK + IDX_HEAD_DIM
SM_GA = SM_IW + IDX_HEADS
PROJ_TN = 1024
PROJ_COLS = -(-(COL_SM + LANES) // PROJ_TN) * PROJ_TN

DSA_TQ = 128
DSA_TK = 256
BISECT_CHECK = 28
VT_ROWS = ATTN_HEAD_DIM + PACK16
LOG2E = 1.4426950408889634

FFN_SLAB = 512

VMEM_LIMIT = 62 * 1024 * 1024


def _cparams(sem):
    return pltpu.CompilerParams(dimension_semantics=sem, vmem_limit_bytes=VMEM_LIMIT)


def _dot(a, b):
    return jnp.dot(a, b, preferred_element_type=F32)


def _dot_nt(a, b):
    return lax.dot_general(a, b, (((1,), (1,)), ((), ())), preferred_element_type=F32)


def _rms(x, g):
    return x * lax.rsqrt(jnp.mean(x * x, axis=-1, keepdims=True) + EPS) * g


def _ffn_body(x_ref, g_ref, wg_ref, wu_ref, wd_ref, fg_ref, o_ref, h_ref, *, final_norm):
    f = pl.program_id(1)

    slabs = [slice(r, r + FFN_SLAB) for r in range(0, x_ref.shape[0], FFN_SLAB)]

    @pl.when(f == 0)
    def _():
        for rows in slabs:
            h_ref[rows, :] = _rms(x_ref[rows, :], g_ref[...]).astype(BF16)
        o_ref[...] = jnp.zeros_like(o_ref)

    for rows in slabs:
        h = h_ref[rows, :]
        gate = _dot(h, wg_ref[...])
        up = _dot(h, wu_ref[...])
        act = (gate * jax.nn.sigmoid(gate) * up).astype(BF16)
        o_ref[rows, :] += _dot(act, wd_ref[...])

    @pl.when(f == pl.num_programs(1) - 1)
    def _():
        for rows in slabs:
            y = x_ref[rows, :] + FFN_RES * o_ref[rows, :]
            if final_norm:
                y = _rms(y, fg_ref[...])
            o_ref[rows, :] = y


def _ffn(x, g, wg, wu, wd, fg, layer, wlayer, *, final_norm, tm=1024, tf=512):
    n, d = x.shape
    f = wg.shape[2]
    return pl.pallas_call(
        functools.partial(_ffn_body, final_norm=final_norm),
        grid=(n // tm, f // tf),
        in_specs=[
            pl.BlockSpec((tm, d), lambda i, j: (i, 0)),
            pl.BlockSpec((None, 1, d), lambda i, j: (layer, 0, 0)),
            pl.BlockSpec((None, d, tf), lambda i, j: (wlayer, 0, j)),
            pl.BlockSpec((None, d, tf), lambda i, j: (wlayer, 0, j)),
            pl.BlockSpec((None, tf, d), lambda i, j: (wlayer, j, 0)),
            pl.BlockSpec((1, d), lambda i, j: (0, 0)),
        ],
        out_specs=pl.BlockSpec((tm, d), lambda i, j: (i, 0)),
        out_shape=jax.ShapeDtypeStruct((n, d), F32),
        scratch_shapes=[pltpu.VMEM((tm, d), BF16)],
        compiler_params=_cparams(("parallel", "arbitrary")),
        name="ffn",
    )(x, g, wg, wu, wd, fg)


def _cast_body(w_ref, o_ref):
    o_ref[...] = w_ref[...].astype(o_ref.dtype)


def _to_bf16(w, tr, layer):
    _, r, c = w.shape
    return pl.pallas_call(
        _cast_body,
        grid=(r // tr,),
        in_specs=[pl.BlockSpec((None, tr, c), lambda i: (layer, i, 0))],
        out_specs=pl.BlockSpec((None, tr, c), lambda i: (0, i, 0)),
        out_shape=jax.ShapeDtypeStruct((1, r, c), BF16),
        compiler_params=_cparams(("parallel",)),
        name="cast",
    )(w)


def _w_in_body(w_ref, o_ref):
    w = w_ref[...]
    seg = {name: w[:, off:off + width] for name, off, width in
           zip(IN_NAMES, (0,) + IN_OFFSETS, IN_SPLITS)}
    tr = w.shape[0]
    small_pad = jnp.zeros((tr, LANES - IDX_HEAD_DIM - IDX_HEADS - GLA_GATE_RANK), w.dtype)
    tail = jnp.zeros((tr, PROJ_COLS - COL_SM - LANES), w.dtype)
    order = ("aq", "iq", "gv", "gr", "m_a", "m_g", "gq", "gk", "ak", "av", "ik", "iw", "ga")
    o_ref[...] = jnp.concatenate([seg[k] for k in order] + [small_pad, tail], axis=1).astype(o_ref.dtype)


def _prep_w_in(w, tr=256):
    depth, d, c = w.shape
    return pl.pallas_call(
        _w_in_body,
        grid=(depth, d // tr),
        in_specs=[pl.BlockSpec((None, tr, c), lambda l, i: (l, i, 0))],
        out_specs=pl.BlockSpec((None, tr, PROJ_COLS), lambda l, i: (l, i, 0)),
        out_shape=jax.ShapeDtypeStruct((depth, d, PROJ_COLS), BF16),
        compiler_params=_cparams(("parallel", "parallel")),
        name="w_in_prep",
    )(w)


def _proj_body(x_ref, g_ref, w_ref, o_ref, h_ref):
    @pl.when(pl.program_id(1) == 0)
    def _():
        h_ref[...] = _rms(x_ref[...], g_ref[...]).astype(BF16)

    o_ref[...] = _dot(h_ref[...], w_ref[...]).astype(o_ref.dtype)


def _proj(x, g, w, layer, *, tm=1024, tn=2 * PROJ_TN):
    n, d = x.shape
    c = w.shape[2]
    return pl.pallas_call(
        _proj_body,
        grid=(n // tm, c // tn),
        in_specs=[
            pl.BlockSpec((tm, d), lambda i, j: (i, 0)),
            pl.BlockSpec((None, 1, d), lambda i, j: (layer, 0, 0)),
            pl.BlockSpec((None, d, tn), lambda i, j: (layer, 0, j)),
        ],
        out_specs=pl.BlockSpec((tm, tn), lambda i, j: (i, j)),
        out_shape=jax.ShapeDtypeStruct((n, c), BF16),
        scratch_shapes=[pltpu.VMEM((tm, d), BF16)],
        compiler_params=_cparams(("parallel", "arbitrary")),
        name="proj",
    )(x, g, w)


def _rope_tables(length, head_dim, rot):
    half = rot // 2
    inv = ROPE_THETA ** (-jnp.arange(0, rot, 2, dtype=F32) / rot)
    ang = jnp.arange(length, dtype=F32)[:, None] * inv[None, :]
    cos, sin = jnp.cos(ang), jnp.sin(ang)
    zeros = jnp.zeros((length, head_dim - rot), F32)
    zh = jnp.zeros((length, half), F32)
    c = jnp.concatenate([cos, cos, zeros + 1.0], axis=1)
    s1 = jnp.concatenate([-sin, zh, zeros], axis=1)
    s2 = jnp.concatenate([zh, sin, zeros], axis=1)
    reps = LANES // head_dim
    return tuple(jnp.tile(t, (1, reps)) for t in (c, s1, s2))


def _rope_lanes(x, c, s1, s2, half):
    return x * c + pltpu.roll(x, LANES - half, 1) * s1 + pltpu.roll(x, half, 1) * s2


def _store_t(dst_ref, rows, val):
    vt = val.T.astype(dst_ref.dtype)
    for c in range(vt.shape[1] // DSA_TQ):
        dst_ref[c, rows, :] = vt[:, c * DSA_TQ:(c + 1) * DSA_TQ]


def _rope_body(aq_ref, iq_ref, ak_ref, av_ref, sm_ref, ca_ref, sa1_ref, sa2_ref, ci_ref, si1_ref, si2_ref,
               aqt_ref, iqt_ref, smt_ref, ako_ref, smo_ref, avt_ref):
    ca, sa1, sa2 = ca_ref[...], sa1_ref[...], sa2_ref[...]
    ci, si1, si2 = ci_ref[...], si1_ref[...], si2_ref[...]
    ha, hi = ATTN_ROT // 2, IDX_ROT // 2
    tk = DSA_TK
    for j in range(ATTN_Q // LANES):
        sl = slice(j * LANES, (j + 1) * LANES)
        r = _rope_lanes(aq_ref[:, sl].astype(F32), ca, sa1, sa2, ha)
        _store_t(aqt_ref, sl, r * (ATTN_HEAD_DIM ** -0.5 * LOG2E))
    for j in range(IDX_Q // LANES):
        sl = slice(j * LANES, (j + 1) * LANES)
        r = _rope_lanes(iq_ref[:, sl].astype(F32), ci, si1, si2, hi)
        _store_t(iqt_ref, sl, r * (IDX_HEAD_DIM ** -0.5))
    for j in range(ATTN_KV // LANES):
        sl = slice(j * LANES, (j + 1) * LANES)
        ako_ref[:, sl] = _rope_lanes(ak_ref[:, sl].astype(F32), ca, sa1, sa2, ha).astype(BF16)
        vt = av_ref[:, sl].astype(F32).T
        for c in range(vt.shape[1] // tk):
            avt_ref[c, j * VT_ROWS:j * VT_ROWS + LANES, :] = vt[:, c * tk:(c + 1) * tk].astype(BF16)
            avt_ref[c, j * VT_ROWS + LANES:(j + 1) * VT_ROWS, :] = jnp.ones((VT_ROWS - LANES, tk), BF16)
    sm = sm_ref[...].astype(F32)
    lane = lax.broadcasted_iota(jnp.int32, sm.shape, 1)
    sm = jnp.where(lane < IDX_HEAD_DIM, _rope_lanes(sm, ci, si1, si2, hi), sm)
    smo_ref[...] = sm.astype(BF16)
    _store_t(smt_ref, slice(0, LANES), sm)


def _rope(proj, seq, tabs_a, tabs_i, *, tr=512):
    n = proj.shape[0]
    nl = seq // tr
    tk = DSA_TK
    tab = pl.BlockSpec((tr, LANES), lambda i: (i % nl, 0))
    return pl.pallas_call(
        _rope_body,
        grid=(n // tr,),
        in_specs=[
            pl.BlockSpec((tr, ATTN_Q), lambda i: (i, COL_AQ // ATTN_Q)),
            pl.BlockSpec((tr, IDX_Q), lambda i: (i, COL_IQ // IDX_Q)),
            pl.BlockSpec((tr, ATTN_KV), lambda i: (i, COL_AK // ATTN_KV)),
            pl.BlockSpec((tr, ATTN_KV), lambda i: (i, COL_AV // ATTN_KV)),
            pl.BlockSpec((tr, LANES), lambda i: (i, COL_SM // LANES)),
            tab, tab, tab, tab, tab, tab,
        ],
        out_specs=[
            pl.BlockSpec((tr // DSA_TQ, ATTN_Q, DSA_TQ), lambda i: (i, 0, 0)),
            pl.BlockSpec((tr // DSA_TQ, IDX_Q, DSA_TQ), lambda i: (i, 0, 0)),
            pl.BlockSpec((tr // DSA_TQ, LANES, DSA_TQ), lambda i: (i, 0, 0)),
            pl.BlockSpec((tr, ATTN_KV), lambda i: (i, 0)),
            pl.BlockSpec((tr, LANES), lambda i: (i, 0)),
            pl.BlockSpec((tr // tk, ATTN_KV_HEADS * VT_ROWS, tk), lambda i: (i, 0, 0)),
        ],
        out_shape=[
            jax.ShapeDtypeStruct((n // DSA_TQ, ATTN_Q, DSA_TQ), BF16),
            jax.ShapeDtypeStruct((n // DSA_TQ, IDX_Q, DSA_TQ), BF16),
            jax.ShapeDtypeStruct((n // DSA_TQ, LANES, DSA_TQ), BF16),
            jax.ShapeDtypeStruct((n, ATTN_KV), BF16),
            jax.ShapeDtypeStruct((n, LANES), BF16),
            jax.ShapeDtypeStruct((n // tk, ATTN_KV_HEADS * VT_ROWS, tk), BF16),
        ],
        compiler_params=_cparams(("parallel",)),
        name="rope",
    )(proj, proj, proj, proj, proj, *tabs_a, *tabs_i)


def _sortable(x):
    i = pltpu.bitcast(x + 0.0, jnp.int32)
    return jnp.where(i < 0, i ^ jnp.int32(0x7FFFFFFF), i)


def _tree_sum(xs):
    while len(xs) > 1:
        xs = [xs[i] + xs[i + 1] for i in range(0, len(xs) - 1, 2)] + ([xs[-1]] if len(xs) % 2 else [])
    return xs[0]


def _colsum(x):
    r, c = x.shape
    return jnp.sum(jnp.sum(x.reshape(r // 8, 8, c), axis=0), axis=0, keepdims=True)


def _dsa_body(*refs, topk, seq, n_cast):
    iqt_ref, smt_ref, aqt_ref, smk_ref, ak_ref, avt_ref = refs[:6]
    cast_in, o_ref, cast_out = refs[6:6 + n_cast], refs[6 + n_cast], refs[7 + n_cast:7 + 2 * n_cast]
    key_ref, m_ref, al_ref, acc_ref, s0_ref, s1_ref, p0_ref, p1_ref = refs[7 + 2 * n_cast:]
    for w_ref, wb_ref in zip(cast_in, cast_out):
        wb_ref[...] = w_ref[...].astype(BF16)
    tq, tk = DSA_TQ, DSA_TK
    qi = pl.program_id(1)
    t0 = qi * tq
    nkb = (t0 + tq + tk - 1) // tk
    iwt = smt_ref[SM_IW:SM_IW + IDX_HEADS, :].astype(F32) * (IDX_HEADS ** -0.5)
    qpos = lax.broadcasted_iota(jnp.int32, (tk, tq), 1) + t0
    kiota = lax.broadcasted_iota(jnp.int32, (tk, tq), 0)

    npair = (nkb + 1) // 2
    odd = nkb % 2 == 1

    def idx_block(kb):
        k0 = pl.multiple_of(kb * tk, tk)
        ik = smk_ref[pl.ds(k0, tk), :][:, SM_IK:SM_IK + IDX_HEAD_DIM]
        score = jnp.zeros((tk, tq), F32)
        for h in range(0, IDX_HEADS, 2):
            qq = jnp.concatenate([iqt_ref[(h + u) * IDX_HEAD_DIM:(h + u + 1) * IDX_HEAD_DIM, :]
                                  for u in range(2)], axis=1)
            rel = jnp.maximum(_dot(ik, qq), 0.0)
            score = score + rel[:, :tq] * iwt[h:h + 1, :] + rel[:, tq:] * iwt[h + 1:h + 2, :]
        key_ref[kb] = jnp.where(kiota + kb * tk <= qpos, _sortable(score), jnp.int32(INT_MIN))

    def idx_pair(i, carry):
        idx_block(2 * i)
        idx_block(2 * i + 1)
        return carry

    lax.fori_loop(0, nkb // 2, idx_pair, 0)

    @pl.when(odd)
    def _():
        idx_block(nkb - 1)
        key_ref[nkb] = jnp.full((tk, tq), INT_MIN, jnp.int32)

    def counts(*preds):
        def pair(i, cs):
            for u in range(2):
                kb = 2 * i + u
                key = key_ref[kb]
                hits = [jnp.where(p(key, kb), 1, 0) for p in preds]
                cs = tuple(c + _tree_sum([h[j:j + 8] for j in range(0, tk, 8)]) for c, h in zip(cs, hits))
            return cs
        cs = lax.fori_loop(0, npair, pair, tuple(jnp.zeros((8, tq), jnp.int32) for _ in preds))
        return tuple(jnp.sum(c.astype(F32), axis=0, keepdims=True) for c in cs)

    def count(pred):
        return counts(pred)[0]

    kf = float(topk)

    def bisect(it, carry):
        t, n_t = carry
        cand = t + lax.shift_left(jnp.int32(1), jnp.int32(31) - it)
        n_c = count(lambda k, kb: k >= cand)
        keep = n_c >= kf
        return jnp.where(keep, cand, t), jnp.where(keep, n_c, n_t)

    start = (jnp.full((1, tq), INT_MIN, jnp.int32), jnp.full((1, tq), float(seq + tk), F32))
    thr, n_thr = lax.fori_loop(0, BISECT_CHECK, bisect, start)
    settled = (n_thr == kf) | (thr == jnp.int32(INT_MIN))

    def keep_all(t):
        return jnp.where(t > jnp.int32(INT_MIN), jnp.int32(seq), jnp.int32(-1))

    def finish():
        t = lax.fori_loop(BISECT_CHECK, 32, bisect, (thr, n_thr))[0]
        n_gt, n_ge = counts(lambda k, kb: k > t, lambda k, kb: k >= t)
        need = kf - n_gt
        tied = (n_ge > kf) & (t > jnp.int32(INT_MIN))

        def tie_break():
            def step(it, x):
                cand = x + lax.shift_left(jnp.int32(1), jnp.int32(seq.bit_length() - 1) - it)
                below = count(lambda k, kb: (k == t) & (kiota + kb * tk < cand))
                return jnp.where(below < need, cand, x)
            return lax.fori_loop(0, seq.bit_length(), step, jnp.zeros((1, tq), jnp.int32))

        jtie = lax.cond(jnp.max(jnp.where(tied, 1.0, 0.0)) > 0.0, tie_break,
                        lambda: jnp.zeros((1, tq), jnp.int32))
        return t, jnp.where(tied, jtie, keep_all(t))

    thr, jmax = lax.cond(jnp.min(jnp.where(settled, 1.0, 0.0)) > 0.0, lambda: (thr, keep_all(thr)), finish)

    last = nkb - 1

    def scores(kb, s_ref):
        kb = jnp.minimum(kb, last)
        kc = jnp.minimum(kb, seq // tk - 1)
        k0 = pl.multiple_of(kc * tk, tk)
        key = key_ref[kb]
        sel = (key > thr) | ((key == thr) & (kiota + kb * tk <= jmax))
        bias = jnp.where(sel, 0.0, NEG_INF)
        bias = jnp.concatenate([bias] * ATTN_REP, axis=1)
        for g in range(ATTN_KV_HEADS):
            kg = ak_ref[pl.ds(k0, tk), g * ATTN_HEAD_DIM:(g + 1) * ATTN_HEAD_DIM]
            qt = jnp.concatenate(
                [aqt_ref[(g * ATTN_REP + r) * ATTN_HEAD_DIM:(g * ATTN_REP + r + 1) * ATTN_HEAD_DIM, :]
                 for r in range(ATTN_REP)], axis=1)
            s_ref[g] = _dot(kg, qt) + bias

    def softmax(s_ref, p_ref):
        for g in range(ATTN_KV_HEADS):
            for r in range(ATTN_REP):
                cols = slice(r * tq, (r + 1) * tq)
                s = s_ref[g, :, cols]
                m_old = m_ref[g, :, cols]
                m_new = jnp.maximum(m_old, jnp.max(s, axis=0, keepdims=True))
                p_ref[g, :, cols] = jnp.exp2(s - m_new).astype(BF16)
                al_ref[g, :, cols] = jnp.exp2(m_old - m_new)
                m_ref[g, :, cols] = m_new

    def values(kb, p_ref):
        kc = jnp.clip(kb, 0, seq // tk - 1)
        for g in range(ATTN_KV_HEADS):
            vt = avt_ref[kc, g * VT_ROWS:(g + 1) * VT_ROWS, :]
            acc_ref[g] = al_ref[g] * acc_ref[g] + _dot(vt, p_ref[g])

    m_ref[...] = jnp.full_like(m_ref, NEG_INF)
    acc_ref[...] = jnp.zeros_like(acc_ref)
    al_ref[...] = jnp.ones_like(al_ref)
    p1_ref[...] = jnp.zeros_like(p1_ref)
    scores(0, s0_ref)

    def att_pair(i, carry):
        j = 2 * i
        values(j - 1, p1_ref)
        softmax(s0_ref, p0_ref)
        scores(j + 1, s1_ref)
        values(j, p0_ref)
        softmax(s1_ref, p1_ref)
        scores(j + 2, s0_ref)
        return carry

    lax.fori_loop(0, nkb // 2, att_pair, 0)

    @pl.when(odd)
    def _():
        values(last - 1, p1_ref)
        softmax(s0_ref, p0_ref)
        values(last, p0_ref)

    @pl.when(jnp.logical_not(odd))
    def _():
        values(last, p1_ref)
    for g in range(ATTN_KV_HEADS):
        acc = acc_ref[g]
        ot = acc[:ATTN_HEAD_DIM] / acc[ATTN_HEAD_DIM:ATTN_HEAD_DIM + 1]
        for r in range(ATTN_REP):
            hh = g * ATTN_REP + r
            o_ref[:, hh * ATTN_HEAD_DIM:(hh + 1) * ATTN_HEAD_DIM] = ot[:, r * tq:(r + 1) * tq].T.astype(BF16)


def _dsa(aq_t, iq_t, sm_t, ak_r, sm_r, av_t, batch, seq, casts=()):
    n = ak_r.shape[0]
    tq, tk = DSA_TQ, DSA_TK
    nq = seq // tq
    nkb = seq // tk
    topk = min(TOPK_MAX, seq // 4)
    qmap = lambda b, i: (b * nq + i, 0, 0)
    steps = batch * nq
    cast_ops, cast_in_specs, cast_out_specs, cast_out_shapes = [], [], [], []
    for w, layer in casts:
        _, r, c = w.shape
        slab = next(s for s in range(PACK16, r + 1, PACK16) if r % s == 0 and s * steps >= r)
        last = r // slab - 1
        cast_ops.append(w)
        cast_in_specs.append(pl.BlockSpec(
            (None, slab, c), lambda b, i, layer=layer, last=last: (layer, jnp.minimum(b * nq + i, last), 0)))
        cast_out_specs.append(pl.BlockSpec(
            (None, slab, c), lambda b, i, last=last: (0, jnp.minimum(b * nq + i, last), 0)))
        cast_out_shapes.append(jax.ShapeDtypeStruct((1, r, c), BF16))
    out = pl.pallas_call(
        functools.partial(_dsa_body, topk=topk, seq=seq, n_cast=len(casts)),
        grid=(batch, nq),
        in_specs=[
            pl.BlockSpec((None, IDX_Q, tq), qmap),
            pl.BlockSpec((None, LANES, tq), qmap),
            pl.BlockSpec((None, ATTN_Q, tq), qmap),
            pl.BlockSpec((seq, LANES), lambda b, i: (b, 0)),
            pl.BlockSpec((seq, ATTN_KV), lambda b, i: (b, 0)),
            pl.BlockSpec((nkb, ATTN_KV_HEADS * VT_ROWS, tk), lambda b, i: (b, 0, 0)),
        ] + cast_in_specs,
        out_specs=[pl.BlockSpec((tq, ATTN_Q), lambda b, i: (b * nq + i, 0))] + cast_out_specs,
        out_shape=[jax.ShapeDtypeStruct((n, ATTN_Q), BF16)] + cast_out_shapes,
        scratch_shapes=[
            pltpu.VMEM((nkb + nkb % 2, tk, tq), jnp.int32),
            pltpu.VMEM((ATTN_KV_HEADS, 1, ATTN_REP * tq), F32),
            pltpu.VMEM((ATTN_KV_HEADS, 1, ATTN_REP * tq), F32),
            pltpu.VMEM((ATTN_KV_HEADS, VT_ROWS, ATTN_REP * tq), F32),
            pltpu.VMEM((ATTN_KV_HEADS, tk, ATTN_REP * tq), F32),
            pltpu.VMEM((ATTN_KV_HEADS, tk, ATTN_REP * tq), F32),
            pltpu.VMEM((ATTN_KV_HEADS, tk, ATTN_REP * tq), BF16),
            pltpu.VMEM((ATTN_KV_HEADS, tk, ATTN_REP * tq), BF16),
        ],
        compiler_params=_cparams(("arbitrary", "arbitrary")),
        name="dsa",
    )(iq_t, sm_t, aq_t, sm_r, ak_r, av_t, *cast_ops)
    return out[0], tuple(out[1:])


GLA_ROWS = 256


def _split3(x):
    hi = x.astype(BF16)
    r1 = x - hi.astype(F32)
    mid = r1.astype(BF16)
    lo = (r1 - mid.astype(F32)).astype(BF16)
    return hi, mid, lo


def _gla_body(q_ref, k_ref, v_ref, r_ref, sm_ref, wa_ref, ba_ref, gn_ref, o_ref, st_ref):
    rows, c = GLA_ROWS, GLA_CHUNK

    @pl.when(pl.program_id(1) == 0)
    def _():
        st_ref[...] = jnp.zeros_like(st_ref)

    ga = sm_ref[:, SM_GA:SM_GA + GLA_GATE_RANK]
    z = _dot(ga, wa_ref[...]) + ba_ref[...]
    g = -(jnp.maximum(-z, 0.0) + jnp.log1p(jnp.exp(-jnp.abs(z)))) * (1.0 / GLA_GATE_NORM)
    ri = lax.broadcasted_iota(jnp.int32, (rows, rows), 0)
    ci = lax.broadcasted_iota(jnp.int32, (rows, rows), 1)
    tri = jnp.where((ri // c == ci // c) & (ci <= ri), 1.0, 0.0).astype(BF16)
    hi, mid, lo = _split3(g)
    b = _dot(tri, hi) + _dot(tri, mid) + _dot(tri, lo)
    low = lax.broadcasted_iota(jnp.int32, (c, c), 1) <= lax.broadcasted_iota(jnp.int32, (c, c), 0)
    gn = gn_ref[...]
    for ch in range(rows // c):
        sl = slice(ch * c, (ch + 1) * c)
        for h in range(GLA_HEADS):
            ks = slice(h * GLA_DK, (h + 1) * GLA_DK)
            vs = slice(h * GLA_DV, (h + 1) * GLA_DV)
            bc = b[sl, ks]
            bl = bc[c - 1:c, :]
            q = q_ref[sl, ks].astype(F32) * (GLA_DK ** -0.5)
            k = k_ref[sl, ks].astype(F32)
            v = v_ref[sl, vs]
            qe = (q * jnp.exp(bc - bl)).astype(BF16)
            kd = (k * jnp.exp(bl - bc)).astype(BF16)
            qb = (q * jnp.exp(bc)).astype(BF16)
            a = jnp.where(low, _dot_nt(qe, kd), 0.0).astype(BF16)
            st = st_ref[h]
            o = _dot(a, v) + _dot_nt(qb, st.astype(BF16))
            vt = v.astype(F32).T.astype(BF16)
            st_ref[h] = st * jnp.exp(bl) + _dot(vt, kd)
            o = _rms(o, gn)
            gr = r_ref[sl, vs].astype(F32)
            o_ref[sl, vs] = (o * (gr * jax.nn.sigmoid(gr))).astype(BF16)


def _gla(proj, wa2, ba, gn, batch, seq):
    n = proj.shape[0]
    rows = GLA_ROWS
    ng = seq // rows
    rmap = lambda off, w: (lambda b, i: (b * ng + i, off // w))
    return pl.pallas_call(
        _gla_body,
        grid=(batch, ng),
        in_specs=[
            pl.BlockSpec((rows, GLA_K), rmap(COL_GQ, GLA_K)),
            pl.BlockSpec((rows, GLA_K), rmap(COL_GK, GLA_K)),
            pl.BlockSpec((rows, GLA_V), rmap(COL_GV, GLA_V)),
            pl.BlockSpec((rows, GLA_V), rmap(COL_GR, GLA_V)),
            pl.BlockSpec((rows, LANES), rmap(COL_SM, LANES)),
            pl.BlockSpec((GLA_GATE_RANK, GLA_K), lambda b, i: (0, 0)),
            pl.BlockSpec((1, GLA_K), lambda b, i: (0, 0)),
            pl.BlockSpec((1, GLA_DV), lambda b, i: (0, 0)),
        ],
        out_specs=pl.BlockSpec((rows, GLA_V), lambda b, i: (b * ng + i, 0)),
        out_shape=jax.ShapeDtypeStruct((n, GLA_V), BF16),
        scratch_shapes=[pltpu.VMEM((GLA_HEADS, GLA_DV, GLA_DK), F32)],
        compiler_params=_cparams(("parallel", "arbitrary")),
        name="gla",
    )(proj, proj, proj, proj, proj, wa2, ba, gn)


def _merge_body(x_ref, oa_ref, og_ref, ma_ref, mg_ref, wua_ref, wug_ref, wo_ref, o_ref):
    ya = jax.nn.sigmoid(ma_ref[...].astype(F32)) * _dot(oa_ref[...], wua_ref[...])
    yg = jax.nn.sigmoid(mg_ref[...].astype(F32)) * _dot(og_ref[...], wug_ref[...])
    y = (ya + yg).astype(BF16)
    o_ref[...] = x_ref[...] + _dot(y, wo_ref[...])


def _merge(x, o_attn, o_gla, proj, wua, wug, wo, layer, *, tm=256):
    n, d = x.shape
    const = lambda i: (layer, 0, 0)
    return pl.pallas_call(
        _merge_body,
        grid=(n // tm,),
        in_specs=[
            pl.BlockSpec((tm, d), lambda i: (i, 0)),
            pl.BlockSpec((tm, ATTN_Q), lambda i: (i, 0)),
            pl.BlockSpec((tm, GLA_V), lambda i: (i, 0)),
            pl.BlockSpec((tm, d), lambda i: (i, COL_MA // D_MODEL)),
            pl.BlockSpec((tm, d), lambda i: (i, COL_MG // D_MODEL)),
            pl.BlockSpec((None, ATTN_Q, d), const),
            pl.BlockSpec((None, GLA_V, d), const),
            pl.BlockSpec((None, d, d), const),
        ],
        out_specs=pl.BlockSpec((tm, d), lambda i: (i, 0)),
        out_shape=jax.ShapeDtypeStruct((n, d), F32),
        compiler_params=_cparams(("parallel",)),
        name="merge",
    )(x, o_attn, o_gla, proj, proj, wua, wug, wo)


def kernel(x, ffn1_norm, ffn1_w_gate, ffn1_w_up, ffn1_w_down, mix_norm, w_in, gla_w_a2, gla_b_a, gla_norm, w_up_attn, w_up_gla, w_out, ffn2_norm, ffn2_w_gate, ffn2_w_up, ffn2_w_down, final_norm):
    batch, seq, d = x.shape
    depth = w_in.shape[0]
    tabs_a = _rope_tables(seq, ATTN_HEAD_DIM, ATTN_ROT)
    tabs_i = _rope_tables(seq, IDX_HEAD_DIM, IDX_ROT)
    fg = final_norm.reshape(1, d)
    g1, gm, g2 = (a.reshape(depth, 1, d) for a in (ffn1_norm, mix_norm, ffn2_norm))
    w_in_b = _prep_w_in(w_in)
    ffn1_w, ffn2_w, merge_w = (ffn1_w_gate, ffn1_w_up, ffn1_w_down), (ffn2_w_gate, ffn2_w_up, ffn2_w_down), \
        (w_up_attn, w_up_gla, w_out)
    ffn1_b = {0: (_to_bf16(ffn1_w_gate, 256, 0), _to_bf16(ffn1_w_up, 256, 0), _to_bf16(ffn1_w_down, 512, 0))}
    hosted = [(w, l) for l in range(1, depth) for w in ffn1_w] + \
             [(w, l) for l in range(depth) for w in ffn2_w + merge_w]
    ffn2_b, merge_b = {}, {}
    h = x.reshape(batch * seq, d)
    for l in range(depth):
        h = _ffn(h, g1, *ffn1_b[l], fg, l, 0, final_norm=False)
        proj = _proj(h, gm, w_in_b, l)
        aq_t, iq_t, sm_t, ak_r, sm_r, av_t = _rope(proj, seq, tabs_a, tabs_i)
        o_attn, cast = _dsa(aq_t, iq_t, sm_t, ak_r, sm_r, av_t, batch, seq, hosted if l == 0 else ())
        if l == 0:
            cast = list(cast)
            for k in range(1, depth):
                ffn1_b[k], cast = tuple(cast[:3]), cast[3:]
            for k in range(depth):
                ffn2_b[k], merge_b[k], cast = tuple(cast[:3]), tuple(cast[3:6]), cast[6:]
        o_gla = _gla(proj, gla_w_a2[l].astype(BF16), gla_b_a[l].reshape(1, GLA_K),
                     gla_norm[l].reshape(1, GLA_DV), batch, seq)
        h = _merge(h, o_attn, o_gla, proj, *merge_b[l], 0)
        h = _ffn(h, g2, *ffn2_b[l], fg, l, 0, final_norm=(l == depth - 1))
    return h.reshape(batch, seq, d)
```

```python
import functools

import jax
import jax.numpy as jnp
import numpy as np
from jax import lax
from jax.experimental import pallas as pl
from jax.experimental.pallas import tpu as pltpu

F32 = jnp.float32
BF16 = jnp.bfloat16

D_MODEL = 2048
D_FF = 5632
FFN_RES = 0.5
ATTN_HEADS = 8
ATTN_KV_HEADS = 2
ATTN_HEAD_DIM = 128
ATTN_REP = ATTN_HEADS // ATTN_KV_HEADS
IDX_HEADS = 16
IDX_HEAD_DIM = 64
TOPK_MAX = 256
GLA_HEADS = 4
GLA_DK = 128
GLA_DV = 256
GLA_GATE_RANK = 16
GLA_GATE_NORM = 16.0
GLA_CHUNK = 64
ROPE_THETA = 500000.0
ATTN_ROT = ATTN_HEAD_DIM // 4
IDX_ROT = IDX_HEAD_DIM // 4
EPS = 1e-6
NEG_INF = -1e30
INT_MIN = -(2 ** 31)
PACK16 = 16

ATTN_Q = ATTN_HEADS * ATTN_HEAD_DIM
ATTN_KV = ATTN_KV_HEADS * ATTN_HEAD_DIM
IDX_Q = IDX_HEADS * IDX_HEAD_DIM
GLA_K = GLA_HEADS * GLA_DK
GLA_V = GLA_HEADS * GLA_DV
IN_SPLITS = (ATTN_Q, ATTN_KV, ATTN_KV, IDX_Q, IDX_HEAD_DIM, IDX_HEADS,
             GLA_K, GLA_K, GLA_V, GLA_GATE_RANK, GLA_V, D_MODEL, D_MODEL)
IN_OFFSETS = tuple(int(v) for v in np.cumsum(IN_SPLITS)[:-1])
IN_NAMES = ("aq", "ak", "av", "iq", "ik", "iw", "gq", "gk", "gv", "ga", "gr", "m_a", "m_g")

LANES = 128
COL_AQ = 0
COL_IQ = COL_AQ + ATTN_Q
COL_GV = COL_IQ + IDX_Q
COL_GR = COL_GV + GLA_V
COL_MA = COL_GR + GLA_V
COL_MG = COL_MA + D_MODEL
COL_GQ = COL_MG + D_MODEL
COL_GK = COL_GQ + GLA_K
COL_AK = COL_GK + GLA_K
COL_AV = COL_AK + ATTN_KV
COL_SM = COL_AV + ATTN_KV
SM_IK = 0
SM_IW = SM_IK + IDX_HEAD_DIM
SM_GA = SM_IW + IDX_HEADS
PROJ_TN = 1024
PROJ_COLS = -(-(COL_SM + LANES) // PROJ_TN) * PROJ_TN

DSA_TQ = 128
DSA_TK = 256
BISECT_CHECK = 28
VT_ROWS = ATTN_HEAD_DIM + PACK16
LOG2E = 1.4426950408889634

FFN_SLAB = 512

VMEM_LIMIT = 62 * 1024 * 1024


def _cparams(sem):
    return pltpu.CompilerParams(dimension_semantics=sem, vmem_limit_bytes=VMEM_LIMIT)


def _dot(a, b):
    return jnp.dot(a, b, preferred_element_type=F32)


def _dot_nt(a, b):
    return lax.dot_general(a, b, (((1,), (1,)), ((), ())), preferred_element_type=F32)


def _rms(x, g):
    return x * lax.rsqrt(jnp.mean(x * x, axis=-1, keepdims=True) + EPS) * g


def _ffn_body(x_ref, g_ref, wg_ref, wu_ref, wd_ref, fg_ref, o_ref, h_ref, *, final_norm):
    f = pl.program_id(1)

    slabs = [slice(r, r + FFN_SLAB) for r in range(0, x_ref.shape[0], FFN_SLAB)]

    @pl.when(f == 0)
    def _():
        for rows in slabs:
            h_ref[rows, :] = _rms(x_ref[rows, :], g_ref[...]).astype(BF16)
        o_ref[...] = jnp.zeros_like(o_ref)

    for rows in slabs:
        h = h_ref[rows, :]
        gate = _dot(h, wg_ref[...])
        up = _dot(h, wu_ref[...])
        act = (gate * jax.nn.sigmoid(gate) * up).astype(BF16)
        o_ref[rows, :] += _dot(act, wd_ref[...])

    @pl.when(f == pl.num_programs(1) - 1)
    def _():
        for rows in slabs:
            y = x_ref[rows, :] + FFN_RES * o_ref[rows, :]
            if final_norm:
                y = _rms(y, fg_ref[...])
            o_ref[rows, :] = y


def _ffn(x, g, wg, wu, wd, fg, layer, wlayer, *, final_norm, tm=1024, tf=512):
    n, d = x.shape
    f = wg.shape[2]
    return pl.pallas_call(
        functools.partial(_ffn_body, final_norm=final_norm),
        grid=(n // tm, f // tf),
        in_specs=[
            pl.BlockSpec((tm, d), lambda i, j: (i, 0)),
            pl.BlockSpec((None, 1, d), lambda i, j: (layer, 0, 0)),
            pl.BlockSpec((None, d, tf), lambda i, j: (wlayer, 0, j)),
            pl.BlockSpec((None, d, tf), lambda i, j: (wlayer, 0, j)),
            pl.BlockSpec((None, tf, d), lambda i, j: (wlayer, j, 0)),
            pl.BlockSpec((1, d), lambda i, j: (0, 0)),
        ],
        out_specs=pl.BlockSpec((tm, d), lambda i, j: (i, 0)),
        out_shape=jax.ShapeDtypeStruct((n, d), F32),
        scratch_shapes=[pltpu.VMEM((tm, d), BF16)],
        compiler_params=_cparams(("parallel", "arbitrary")),
        name="ffn",
    )(x, g, wg, wu, wd, fg)


def _cast_body(w_ref, o_ref):
    o_ref[...] = w_ref[...].astype(o_ref.dtype)


def _to_bf16(w, tr, layer):
    _, r, c = w.shape
    return pl.pallas_call(
        _cast_body,
        grid=(r // tr,),
        in_specs=[pl.BlockSpec((None, tr, c), lambda i: (layer, i, 0))],
        out_specs=pl.BlockSpec((None, tr, c), lambda i: (0, i, 0)),
        out_shape=jax.ShapeDtypeStruct((1, r, c), BF16),
        compiler_params=_cparams(("parallel",)),
        name="cast",
    )(w)


def _w_in_body(w_ref, o_ref):
    w = w_ref[...]
    seg = {name: w[:, off:off + width] for name, off, width in
           zip(IN_NAMES, (0,) + IN_OFFSETS, IN_SPLITS)}
    tr = w.shape[0]
    small_pad = jnp.zeros((tr, LANES - IDX_HEAD_DIM - IDX_HEADS - GLA_GATE_RANK), w.dtype)
    tail = jnp.zeros((tr, PROJ_COLS - COL_SM - LANES), w.dtype)
    order = ("aq", "iq", "gv", "gr", "m_a", "m_g", "gq", "gk", "ak", "av", "ik", "iw", "ga")
    o_ref[...] = jnp.concatenate([seg[k] for k in order] + [small_pad, tail], axis=1).astype(o_ref.dtype)


def _prep_w_in(w, tr=256):
    depth, d, c = w.shape
    return pl.pallas_call(
        _w_in_body,
        grid=(depth, d // tr),
        in_specs=[pl.BlockSpec((None, tr, c), lambda l, i: (l, i, 0))],
        out_specs=pl.BlockSpec((None, tr, PROJ_COLS), lambda l, i: (l, i, 0)),
        out_shape=jax.ShapeDtypeStruct((depth, d, PROJ_COLS), BF16),
        compiler_params=_cparams(("parallel", "parallel")),
        name="w_in_prep",
    )(w)


def _proj_body(x_ref, g_ref, w_ref, o_ref, h_ref):
    @pl.when(pl.program_id(1) == 0)
    def _():
        h_ref[...] = _rms(x_ref[...], g_ref[...]).astype(BF16)

    o_ref[...] = _dot(h_ref[...], w_ref[...]).astype(o_ref.dtype)


def _proj(x, g, w, layer, *, tm=1024, tn=2 * PROJ_TN):
    n, d = x.shape
    c = w.shape[2]
    return pl.pallas_call(
        _proj_body,
        grid=(n // tm, c // tn),
        in_specs=[
            pl.BlockSpec((tm, d), lambda i, j: (i, 0)),
            pl.BlockSpec((None, 1, d), lambda i, j: (layer, 0, 0)),
            pl.BlockSpec((None, d, tn), lambda i, j: (layer, 0, j)),
        ],
        out_specs=pl.BlockSpec((tm, tn), lambda i, j: (i, j)),
        out_shape=jax.ShapeDtypeStruct((n, c), BF16),
        scratch_shapes=[pltpu.VMEM((tm, d), BF16)],
        compiler_params=_cparams(("parallel", "arbitrary")),
        name="proj",
    )(x, g, w)


def _rope_tables(length, head_dim, rot):
    half = rot // 2
    inv = ROPE_THETA ** (-jnp.arange(0, rot, 2, dtype=F32) / rot)
    ang = jnp.arange(length, dtype=F32)[:, None] * inv[None, :]
    cos, sin = jnp.cos(ang), jnp.sin(ang)
    zeros = jnp.zeros((length, head_dim - rot), F32)
    zh = jnp.zeros((length, half), F32)
    c = jnp.concatenate([cos, cos, zeros + 1.0], axis=1)
    s1 = jnp.concatenate([-sin, zh, zeros], axis=1)
    s2 = jnp.concatenate([zh, sin, zeros], axis=1)
    reps = LANES // head_dim
    return tuple(jnp.tile(t, (1, reps)) for t in (c, s1, s2))


def _rope_lanes(x, c, s1, s2, half):
    return x * c + pltpu.roll(x, LANES - half, 1) * s1 + pltpu.roll(x, half, 1) * s2


def _store_t(dst_ref, rows, val):
    vt = val.T.astype(dst_ref.dtype)
    for c in range(vt.shape[1] // DSA_TQ):
        dst_ref[c, rows, :] = vt[:, c * DSA_TQ:(c + 1) * DSA_TQ]


def _rope_body(aq_ref, iq_ref, ak_ref, av_ref, sm_ref, ca_ref, sa1_ref, sa2_ref, ci_ref, si1_ref, si2_ref,
               aqt_ref, iqt_ref, smt_ref, ako_ref, smo_ref, avt_ref):
    ca, sa1, sa2 = ca_ref[...], sa1_ref[...], sa2_ref[...]
    ci, si1, si2 = ci_ref[...], si1_ref[...], si2_ref[...]
    ha, hi = ATTN_ROT // 2, IDX_ROT // 2
    tk = DSA_TK
    for j in range(ATTN_Q // LANES):
        sl = slice(j * LANES, (j + 1) * LANES)
        r = _rope_lanes(aq_ref[:, sl].astype(F32), ca, sa1, sa2, ha)
        _store_t(aqt_ref, sl, r * (ATTN_HEAD_DIM ** -0.5 * LOG2E))
    for j in range(IDX_Q // LANES):
        sl = slice(j * LANES, (j + 1) * LANES)
        r = _rope_lanes(iq_ref[:, sl].astype(F32), ci, si1, si2, hi)
        _store_t(iqt_ref, sl, r * (IDX_HEAD_DIM ** -0.5))
    for j in range(ATTN_KV // LANES):
        sl = slice(j * LANES, (j + 1) * LANES)
        ako_ref[:, sl] = _rope_lanes(ak_ref[:, sl].astype(F32), ca, sa1, sa2, ha).astype(BF16)
        vt = av_ref[:, sl].astype(F32).T
        for c in range(vt.shape[1] // tk):
            avt_ref[c, j * VT_ROWS:j * VT_ROWS + LANES, :] = vt[:, c * tk:(c + 1) * tk].astype(BF16)
            avt_ref[c, j * VT_ROWS + LANES:(j + 1) * VT_ROWS, :] = jnp.ones((VT_ROWS - LANES, tk), BF16)
    sm = sm_ref[...].astype(F32)
    lane = lax.broadcasted_iota(jnp.int32, sm.shape, 1)
    sm = jnp.where(lane < IDX_HEAD_DIM, _rope_lanes(sm, ci, si1, si2, hi), sm)
    smo_ref[...] = sm.astype(BF16)
    _store_t(smt_ref, slice(0, LANES), sm)


def _rope(proj, seq, tabs_a, tabs_i, *, tr=512):
    n = proj.shape[0]
    nl = seq // tr
    tk = DSA_TK
    tab = pl.BlockSpec((tr, LANES), lambda i: (i % nl, 0))
    return pl.pallas_call(
        _rope_body,
        grid=(n // tr,),
        in_specs=[
            pl.BlockSpec((tr, ATTN_Q), lambda i: (i, COL_AQ // ATTN_Q)),
            pl.BlockSpec((tr, IDX_Q), lambda i: (i, COL_IQ // IDX_Q)),
            pl.BlockSpec((tr, ATTN_KV), lambda i: (i, COL_AK // ATTN_KV)),
            pl.BlockSpec((tr, ATTN_KV), lambda i: (i, COL_AV // ATTN_KV)),
            pl.BlockSpec((tr, LANES), lambda i: (i, COL_SM // LANES)),
            tab, tab, tab, tab, tab, tab,
        ],
        out_specs=[
            pl.BlockSpec((tr // DSA_TQ, ATTN_Q, DSA_TQ), lambda i: (i, 0, 0)),
            pl.BlockSpec((tr // DSA_TQ, IDX_Q, DSA_TQ), lambda i: (i, 0, 0)),
            pl.BlockSpec((tr // DSA_TQ, LANES, DSA_TQ), lambda i: (i, 0, 0)),
            pl.BlockSpec((tr, ATTN_KV), lambda i: (i, 0)),
            pl.BlockSpec((tr, LANES), lambda i: (i, 0)),
            pl.BlockSpec((tr // tk, ATTN_KV_HEADS * VT_ROWS, tk), lambda i: (i, 0, 0)),
        ],
        out_shape=[
            jax.ShapeDtypeStruct((n // DSA_TQ, ATTN_Q, DSA_TQ), BF16),
            jax.ShapeDtypeStruct((n // DSA_TQ, IDX_Q, DSA_TQ), BF16),
            jax.ShapeDtypeStruct((n // DSA_TQ, LANES, DSA_TQ), BF16),
            jax.ShapeDtypeStruct((n, ATTN_KV), BF16),
            jax.ShapeDtypeStruct((n, LANES), BF16),
            jax.ShapeDtypeStruct((n // tk, ATTN_KV_HEADS * VT_ROWS, tk), BF16),
        ],
        compiler_params=_cparams(("parallel",)),
        name="rope",
    )(proj, proj, proj, proj, proj, *tabs_a, *tabs_i)


def _sortable(x):
    i = pltpu.bitcast(x + 0.0, jnp.int32)
    return jnp.where(i < 0, i ^ jnp.int32(0x7FFFFFFF), i)


def _tree_sum(xs):
    while len(xs) > 1:
        xs = [xs[i] + xs[i + 1] for i in range(0, len(xs) - 1, 2)] + ([xs[-1]] if len(xs) % 2 else [])
    return xs[0]


def _colsum(x):
    r, c = x.shape
    return jnp.sum(jnp.sum(x.reshape(r // 8, 8, c), axis=0), axis=0, keepdims=True)


def _dsa_body(*refs, topk, seq, n_cast):
    iqt_ref, smt_ref, aqt_ref, smk_ref, ak_ref, avt_ref = refs[:6]
    cast_in, o_ref, cast_out = refs[6:6 + n_cast], refs[6 + n_cast], refs[7 + n_cast:7 + 2 * n_cast]
    key_ref, m_ref, al_ref, acc_ref, s0_ref, s1_ref, p0_ref, p1_ref = refs[7 + 2 * n_cast:]
    for w_ref, wb_ref in zip(cast_in, cast_out):
        wb_ref[...] = w_ref[...].astype(BF16)
    tq, tk = DSA_TQ, DSA_TK
    qi = pl.program_id(1)
    t0 = qi * tq
    nkb = (t0 + tq + tk - 1) // tk
    iwt = smt_ref[SM_IW:SM_IW + IDX_HEADS, :].astype(F32) * (IDX_HEADS ** -0.5)
    qpos = lax.broadcasted_iota(jnp.int32, (tk, tq), 1) + t0
    kiota = lax.broadcasted_iota(jnp.int32, (tk, tq), 0)

    npair = (nkb + 1) // 2
    odd = nkb % 2 == 1

    def idx_block(kb):
        k0 = pl.multiple_of(kb * tk, tk)
        ik = smk_ref[pl.ds(k0, tk), :][:, SM_IK:SM_IK + IDX_HEAD_DIM]
        score = jnp.zeros((tk, tq), F32)
        for h in range(0, IDX_HEADS, 2):
            qq = jnp.concatenate([iqt_ref[(h + u) * IDX_HEAD_DIM:(h + u + 1) * IDX_HEAD_DIM, :]
                                  for u in range(2)], axis=1)
            rel = jnp.maximum(_dot(ik, qq), 0.0)
            score = score + rel[:, :tq] * iwt[h:h + 1, :] + rel[:, tq:] * iwt[h + 1:h + 2, :]
        key_ref[kb] = jnp.where(kiota + kb * tk <= qpos, _sortable(score), jnp.int32(INT_MIN))

    def idx_pair(i, carry):
        idx_block(2 * i)
        idx_block(2 * i + 1)
        return carry

    lax.fori_loop(0, nkb // 2, idx_pair, 0)

    @pl.when(odd)
    def _():
        idx_block(nkb - 1)
        key_ref[nkb] = jnp.full((tk, tq), INT_MIN, jnp.int32)

    def counts(*preds):
        def pair(i, cs):
            for u in range(2):
                kb = 2 * i + u
                key = key_ref[kb]
                hits = [jnp.where(p(key, kb), 1, 0) for p in preds]
                cs = tuple(c + _tree_sum([h[j:j + 8] for j in range(0, tk, 8)]) for c, h in zip(cs, hits))
            return cs
        cs = lax.fori_loop(0, npair, pair, tuple(jnp.zeros((8, tq), jnp.int32) for _ in preds))
        return tuple(jnp.sum(c.astype(F32), axis=0, keepdims=True) for c in cs)

    def count(pred):
        return counts(pred)[0]

    kf = float(topk)

    def bisect(it, carry):
        t, n_t = carry
        cand = t + lax.shift_left(jnp.int32(1), jnp.int32(31) - it)
        n_c = count(lambda k, kb: k >= cand)
        keep = n_c >= kf
        return jnp.where(keep, cand, t), jnp.where(keep, n_c, n_t)

    start = (jnp.full((1, tq), INT_MIN, jnp.int32), jnp.full((1, tq), float(seq + tk), F32))
    thr, n_thr = lax.fori_loop(0, BISECT_CHECK, bisect, start)
    settled = (n_thr == kf) | (thr == jnp.int32(INT_MIN))

    def keep_all(t):
        return jnp.where(t > jnp.int32(INT_MIN), jnp.int32(seq), jnp.int32(-1))

    def finish():
        t = lax.fori_loop(BISECT_CHECK, 32, bisect, (thr, n_thr))[0]
        n_gt, n_ge = counts(lambda k, kb: k > t, lambda k, kb: k >= t)
        need = kf - n_gt
        tied = (n_ge > kf) & (t > jnp.int32(INT_MIN))

        def tie_break():
            def step(it, x):
                cand = x + lax.shift_left(jnp.int32(1), jnp.int32(seq.bit_length() - 1) - it)
                below = count(lambda k, kb: (k == t) & (kiota + kb * tk < cand))
                return jnp.where(below < need, cand, x)
            return lax.fori_loop(0, seq.bit_length(), step, jnp.zeros((1, tq), jnp.int32))

        jtie = lax.cond(jnp.max(jnp.where(tied, 1.0, 0.0)) > 0.0, tie_break,
                        lambda: jnp.zeros((1, tq), jnp.int32))
        return t, jnp.where(tied, jtie, keep_all(t))

    thr, jmax = lax.cond(jnp.min(jnp.where(settled, 1.0, 0.0)) > 0.0, lambda: (thr, keep_all(thr)), finish)

    last = nkb - 1

    def scores(kb, s_ref):
        kb = jnp.minimum(kb, last)
        kc = jnp.minimum(kb, seq // tk - 1)
        k0 = pl.multiple_of(kc * tk, tk)
        key = key_ref[kb]
        sel = (key > thr) | ((key == thr) & (kiota + kb * tk <= jmax))
        bias = jnp.where(sel, 0.0, NEG_INF)
        bias = jnp.concatenate([bias] * ATTN_REP, axis=1)
        for g in range(ATTN_KV_HEADS):
            kg = ak_ref[pl.ds(k0, tk), g * ATTN_HEAD_DIM:(g + 1) * ATTN_HEAD_DIM]
            qt = jnp.concatenate(
                [aqt_ref[(g * ATTN_REP + r) * ATTN_HEAD_DIM:(g * ATTN_REP + r + 1) * ATTN_HEAD_DIM, :]
                 for r in range(ATTN_REP)], axis=1)
            sg = _dot(kg, qt) + bias
            for r in range(ATTN_REP):
                s_ref[g, r] = sg[:, r * tq:(r + 1) * tq]

    def softmax(s_ref, p_ref):
        for g in range(ATTN_KV_HEADS):
            for r in range(ATTN_REP):
                cols = slice(r * tq, (r + 1) * tq)
                s = s_ref[g, r]
                m_old = m_ref[g, :, cols]
                m_new = jnp.maximum(m_old, jnp.max(s, axis=0, keepdims=True))
                p_ref[g, r] = jnp.exp2(s - m_new).astype(BF16)
                al_ref[g, :, cols] = jnp.exp2(m_old - m_new)
                m_ref[g, :, cols] = m_new

    def values(kb, p_ref):
        kc = jnp.clip(kb, 0, seq // tk - 1)
        for g in range(ATTN_KV_HEADS):
            vt = avt_ref[kc, g * VT_ROWS:(g + 1) * VT_ROWS, :]
            p = jnp.concatenate([p_ref[g, r] for r in range(ATTN_REP)], axis=1)
            acc_ref[g] = al_ref[g] * acc_ref[g] + _dot(vt, p)

    m_ref[...] = jnp.full_like(m_ref, NEG_INF)
    acc_ref[...] = jnp.zeros_like(acc_ref)
    al_ref[...] = jnp.ones_like(al_ref)
    p1_ref[...] = jnp.zeros_like(p1_ref)
    scores(0, s0_ref)

    def att_pair(i, carry):
        j = 2 * i
        values(j - 1, p1_ref)
        softmax(s0_ref, p0_ref)
        scores(j + 1, s1_ref)
        values(j, p0_ref)
        softmax(s1_ref, p1_ref)
        scores(j + 2, s0_ref)
        return carry

    lax.fori_loop(0, nkb // 2, att_pair, 0)

    @pl.when(odd)
    def _():
        values(last - 1, p1_ref)
        softmax(s0_ref, p0_ref)
        values(last, p0_ref)

    @pl.when(jnp.logical_not(odd))
    def _():
        values(last, p1_ref)
    for g in range(ATTN_KV_HEADS):
        acc = acc_ref[g]
        ot = acc[:ATTN_HEAD_DIM] / acc[ATTN_HEAD_DIM:ATTN_HEAD_DIM + 1]
        for r in range(ATTN_REP):
            hh = g * ATTN_REP + r
            o_ref[:, hh * ATTN_HEAD_DIM:(hh + 1) * ATTN_HEAD_DIM] = ot[:, r * tq:(r + 1) * tq].T.astype(BF16)


def _dsa(aq_t, iq_t, sm_t, ak_r, sm_r, av_t, batch, seq, casts=()):
    n = ak_r.shape[0]
    tq, tk = DSA_TQ, DSA_TK
    nq = seq // tq
    nkb = seq // tk
    topk = min(TOPK_MAX, seq // 4)
    qmap = lambda b, i: (b * nq + i, 0, 0)
    steps = batch * nq
    cast_ops, cast_in_specs, cast_out_specs, cast_out_shapes = [], [], [], []
    for w, layer in casts:
        _, r, c = w.shape
        slab = next(s for s in range(PACK16, r + 1, PACK16) if r % s == 0 and s * steps >= r)
        last = r // slab - 1
        cast_ops.append(w)
        cast_in_specs.append(pl.BlockSpec(
            (None, slab, c), lambda b, i, layer=layer, last=last: (layer, jnp.minimum(b * nq + i, last), 0)))
        cast_out_specs.append(pl.BlockSpec(
            (None, slab, c), lambda b, i, last=last: (0, jnp.minimum(b * nq + i, last), 0)))
        cast_out_shapes.append(jax.ShapeDtypeStruct((1, r, c), BF16))
    out = pl.pallas_call(
        functools.partial(_dsa_body, topk=topk, seq=seq, n_cast=len(casts)),
        grid=(batch, nq),
        in_specs=[
            pl.BlockSpec((None, IDX_Q, tq), qmap),
            pl.BlockSpec((None, LANES, tq), qmap),
            pl.BlockSpec((None, ATTN_Q, tq), qmap),
            pl.BlockSpec((seq, LANES), lambda b, i: (b, 0)),
            pl.BlockSpec((seq, ATTN_KV), lambda b, i: (b, 0)),
            pl.BlockSpec((nkb, ATTN_KV_HEADS * VT_ROWS, tk), lambda b, i: (b, 0, 0)),
        ] + cast_in_specs,
        out_specs=[pl.BlockSpec((tq, ATTN_Q), lambda b, i: (b * nq + i, 0))] + cast_out_specs,
        out_shape=[jax.ShapeDtypeStruct((n, ATTN_Q), BF16)] + cast_out_shapes,
        scratch_shapes=[
            pltpu.VMEM((nkb + nkb % 2, tk, tq), jnp.int32),
            pltpu.VMEM((ATTN_KV_HEADS, 1, ATTN_REP * tq), F32),
            pltpu.VMEM((ATTN_KV_HEADS, 1, ATTN_REP * tq), F32),
            pltpu.VMEM((ATTN_KV_HEADS, VT_ROWS, ATTN_REP * tq), F32),
            pltpu.VMEM((ATTN_KV_HEADS, ATTN_REP, tk, tq), F32),
            pltpu.VMEM((ATTN_KV_HEADS, ATTN_REP, tk, tq), F32),
            pltpu.VMEM((ATTN_KV_HEADS, ATTN_REP, tk, tq), BF16),
            pltpu.VMEM((ATTN_KV_HEADS, ATTN_REP, tk, tq), BF16),
        ],
        compiler_params=_cparams(("arbitrary", "arbitrary")),
        name="dsa",
    )(iq_t, sm_t, aq_t, sm_r, ak_r, av_t, *cast_ops)
    return out[0], tuple(out[1:])


GLA_ROWS = 256


def _split3(x):
    hi = x.astype(BF16)
    r1 = x - hi.astype(F32)
    mid = r1.astype(BF16)
    lo = (r1 - mid.astype(F32)).astype(BF16)
    return hi, mid, lo


def _gla_body(q_ref, k_ref, v_ref, r_ref, sm_ref, wa_ref, ba_ref, gn_ref, o_ref, st_ref):
    rows, c = GLA_ROWS, GLA_CHUNK

    @pl.when(pl.program_id(1) == 0)
    def _():
        st_ref[...] = jnp.zeros_like(st_ref)

    ga = sm_ref[:, SM_GA:SM_GA + GLA_GATE_RANK]
    z = _dot(ga, wa_ref[...]) + ba_ref[...]
    g = -(jnp.maximum(-z, 0.0) + jnp.log1p(jnp.exp(-jnp.abs(z)))) * (1.0 / GLA_GATE_NORM)
    ri = lax.broadcasted_iota(jnp.int32, (rows, rows), 0)
    ci = lax.broadcasted_iota(jnp.int32, (rows, rows), 1)
    tri = jnp.where((ri // c == ci // c) & (ci <= ri), 1.0, 0.0).astype(BF16)
    hi, mid, lo = _split3(g)
    b = _dot(tri, hi) + _dot(tri, mid) + _dot(tri, lo)
    low = lax.broadcasted_iota(jnp.int32, (c, c), 1) <= lax.broadcasted_iota(jnp.int32, (c, c), 0)
    gn = gn_ref[...]
    for ch in range(rows // c):
        sl = slice(ch * c, (ch + 1) * c)
        for h in range(GLA_HEADS):
            ks = slice(h * GLA_DK, (h + 1) * GLA_DK)
            vs = slice(h * GLA_DV, (h + 1) * GLA_DV)
            bc = b[sl, ks]
            bl = bc[c - 1:c, :]
            q = q_ref[sl, ks].astype(F32) * (GLA_DK ** -0.5)
            k = k_ref[sl, ks].astype(F32)
            v = v_ref[sl, vs]
            qe = (q * jnp.exp(bc - bl)).astype(BF16)
            kd = (k * jnp.exp(bl - bc)).astype(BF16)
            qb = (q * jnp.exp(bc)).astype(BF16)
            a = jnp.where(low, _dot_nt(qe, kd), 0.0).astype(BF16)
            st = st_ref[h]
            o = _dot(a, v) + _dot_nt(qb, st.astype(BF16))
            vt = v.astype(F32).T.astype(BF16)
            st_ref[h] = st * jnp.exp(bl) + _dot(vt, kd)
            o = _rms(o, gn)
            gr = r_ref[sl, vs].astype(F32)
            o_ref[sl, vs] = (o * (gr * jax.nn.sigmoid(gr))).astype(BF16)


def _gla(proj, wa2, ba, gn, batch, seq):
    n = proj.shape[0]
    rows = GLA_ROWS
    ng = seq // rows
    rmap = lambda off, w: (lambda b, i: (b * ng + i, off // w))
    return pl.pallas_call(
        _gla_body,
        grid=(batch, ng),
        in_specs=[
            pl.BlockSpec((rows, GLA_K), rmap(COL_GQ, GLA_K)),
            pl.BlockSpec((rows, GLA_K), rmap(COL_GK, GLA_K)),
            pl.BlockSpec((rows, GLA_V), rmap(COL_GV, GLA_V)),
            pl.BlockSpec((rows, GLA_V), rmap(COL_GR, GLA_V)),
            pl.BlockSpec((rows, LANES), rmap(COL_SM, LANES)),
            pl.BlockSpec((GLA_GATE_RANK, GLA_K), lambda b, i: (0, 0)),
            pl.BlockSpec((1, GLA_K), lambda b, i: (0, 0)),
            pl.BlockSpec((1, GLA_DV), lambda b, i: (0, 0)),
        ],
        out_specs=pl.BlockSpec((rows, GLA_V), lambda b, i: (b * ng + i, 0)),
        out_shape=jax.ShapeDtypeStruct((n, GLA_V), BF16),
        scratch_shapes=[pltpu.VMEM((GLA_HEADS, GLA_DV, GLA_DK), F32)],
        compiler_params=_cparams(("parallel", "arbitrary")),
        name="gla",
    )(proj, proj, proj, proj, proj, wa2, ba, gn)


def _merge_body(x_ref, oa_ref, og_ref, ma_ref, mg_ref, wua_ref, wug_ref, wo_ref, o_ref):
    ya = jax.nn.sigmoid(ma_ref[...].astype(F32)) * _dot(oa_ref[...], wua_ref[...])
    yg = jax.nn.sigmoid(mg_ref[...].astype(F32)) * _dot(og_ref[...], wug_ref[...])
    y = (ya + yg).astype(BF16)
    o_ref[...] = x_ref[...] + _dot(y, wo_ref[...])


def _merge(x, o_attn, o_gla, proj, wua, wug, wo, layer, *, tm=256):
    n, d = x.shape
    const = lambda i: (layer, 0, 0)
    return pl.pallas_call(
        _merge_body,
        grid=(n // tm,),
        in_specs=[
            pl.BlockSpec((tm, d), lambda i: (i, 0)),
            pl.BlockSpec((tm, ATTN_Q), lambda i: (i, 0)),
            pl.BlockSpec((tm, GLA_V), lambda i: (i, 0)),
            pl.BlockSpec((tm, d), lambda i: (i, COL_MA // D_MODEL)),
            pl.BlockSpec((tm, d), lambda i: (i, COL_MG // D_MODEL)),
            pl.BlockSpec((None, ATTN_Q, d), const),
            pl.BlockSpec((None, GLA_V, d), const),
            pl.BlockSpec((None, d, d), const),
        ],
        out_specs=pl.BlockSpec((tm, d), lambda i: (i, 0)),
        out_shape=jax.ShapeDtypeStruct((n, d), F32),
        compiler_params=_cparams(("parallel",)),
        name="merge",
    )(x, o_attn, o_gla, proj, proj, wua, wug, wo)


def kernel(x, ffn1_norm, ffn1_w_gate, ffn1_w_up, ffn1_w_down, mix_norm, w_in, gla_w_a2, gla_b_a, gla_norm, w_up_attn, w_up_gla, w_out, ffn2_norm, ffn2_w_gate, ffn2_w_up, ffn2_w_down, final_norm):
    batch, seq, d = x.shape
    depth = w_in.shape[0]
    tabs_a = _rope_tables(seq, ATTN_HEAD_DIM, ATTN_ROT)
    tabs_i = _rope_tables(seq, IDX_HEAD_DIM, IDX_ROT)
    fg = final_norm.reshape(1, d)
    g1, gm, g2 = (a.reshape(depth, 1, d) for a in (ffn1_norm, mix_norm, ffn2_norm))
    w_in_b = _prep_w_in(w_in)
    ffn1_w, ffn2_w, merge_w = (ffn1_w_gate, ffn1_w_up, ffn1_w_down), (ffn2_w_gate, ffn2_w_up, ffn2_w_down), \
        (w_up_attn, w_up_gla, w_out)
    ffn1_b = {0: (_to_bf16(ffn1_w_gate, 256, 0), _to_bf16(ffn1_w_up, 256, 0), _to_bf16(ffn1_w_down, 512, 0))}
    hosted = [(w, l) for l in range(1, depth) for w in ffn1_w] + \
             [(w, l) for l in range(depth) for w in ffn2_w + merge_w]
    ffn2_b, merge_b = {}, {}
    h = x.reshape(batch * seq, d)
    for l in range(depth):
        h = _ffn(h, g1, *ffn1_b[l], fg, l, 0, final_norm=False)
        proj = _proj(h, gm, w_in_b, l)
        aq_t, iq_t, sm_t, ak_r, sm_r, av_t = _rope(proj, seq, tabs_a, tabs_i)
        o_attn, cast = _dsa(aq_t, iq_t, sm_t, ak_r, sm_r, av_t, batch, seq, hosted if l == 0 else ())
        if l == 0:
            cast = list(cast)
            for k in range(1, depth):
                ffn1_b[k], cast = tuple(cast[:3]), cast[3:]
            for k in range(depth):
                ffn2_b[k], merge_b[k], cast = tuple(cast[:3]), tuple(cast[3:6]), cast[6:]
        o_gla = _gla(proj, gla_w_a2[l].astype(BF16), gla_b_a[l].reshape(1, GLA_K),
                     gla_norm[l].reshape(1, GLA_DV), batch, seq)
        h = _merge(h, o_attn, o_gla, proj, *merge_b[l], 0)
        h = _ffn(h, g2, *ffn2_b[l], fg, l, 0, final_norm=(l == depth - 1))
    return h.reshape(batch, seq, d)
```

```python
import functools

import jax
import jax.numpy as jnp
import numpy as np
from jax import lax
from jax.experimental import pallas as pl
from jax.experimental.pallas import tpu as pltpu

F32 = jnp.float32
BF16 = jnp.bfloat16

D_MODEL = 2048
D_FF = 5632
FFN_RES = 0.5
ATTN_HEADS = 8
ATTN_KV_HEADS = 2
ATTN_HEAD_DIM = 128
ATTN_REP = ATTN_HEADS // ATTN_KV_HEADS
IDX_HEADS = 16
IDX_HEAD_DIM = 64
TOPK_MAX = 256
GLA_HEADS = 4
GLA_DK = 128
GLA_DV = 256
GLA_GATE_RANK = 16
GLA_GATE_NORM = 16.0
GLA_CHUNK = 64
ROPE_THETA = 500000.0
ATTN_ROT = ATTN_HEAD_DIM // 4
IDX_ROT = IDX_HEAD_DIM // 4
EPS = 1e-6
NEG_INF = -1e30
INT_MIN = -(2 ** 31)
PACK16 = 16

ATTN_Q = ATTN_HEADS * ATTN_HEAD_DIM
ATTN_KV = ATTN_KV_HEADS * ATTN_HEAD_DIM
IDX_Q = IDX_HEADS * IDX_HEAD_DIM
GLA_K = GLA_HEADS * GLA_DK
GLA_V = GLA_HEADS * GLA_DV
IN_SPLITS = (ATTN_Q, ATTN_KV, ATTN_KV, IDX_Q, IDX_HEAD_DIM, IDX_HEADS,
             GLA_K, GLA_K, GLA_V, GLA_GATE_RANK, GLA_V, D_MODEL, D_MODEL)
IN_OFFSETS = tuple(int(v) for v in np.cumsum(IN_SPLITS)[:-1])
IN_NAMES = ("aq", "ak", "av", "iq", "ik", "iw", "gq", "gk", "gv", "ga", "gr", "m_a", "m_g")

LANES = 128
COL_AQ = 0
COL_IQ = COL_AQ + ATTN_Q
COL_GV = COL_IQ + IDX_Q
COL_GR = COL_GV + GLA_V
COL_MA = COL_GR + GLA_V
COL_MG = COL_MA + D_MODEL
COL_GQ = COL_MG + D_MODEL
COL_GK = COL_GQ + GLA_K
COL_AK = COL_GK + GLA_K
COL_AV = COL_AK + ATTN_KV
COL_SM = COL_AV + ATTN_KV
SM_IK = 0
SM_IW = SM_IK + IDX_HEAD_DIM
SM_GA = SM_IW + IDX_HEADS
PROJ_TN = 1024
PROJ_COLS = -(-(COL_SM + LANES) // PROJ_TN) * PROJ_TN

DSA_TQ = 128
DSA_TK = 256
BISECT_CHECK = 28
VT_ROWS = ATTN_HEAD_DIM + PACK16
LOG2E = 1.4426950408889634

FFN_HALF = 256
FFN_SLAB = 512

VMEM_LIMIT = 62 * 1024 * 1024


def _cparams(sem):
    return pltpu.CompilerParams(dimension_semantics=sem, vmem_limit_bytes=VMEM_LIMIT)


def _dot(a, b):
    return jnp.dot(a, b, preferred_element_type=F32)


def _dot_nt(a, b):
    return lax.dot_general(a, b, (((1,), (1,)), ((), ())), preferred_element_type=F32)


def _rms(x, g):
    return x * lax.rsqrt(jnp.mean(x * x, axis=-1, keepdims=True) + EPS) * g


def _ffn_body(x_ref, g_ref, wg_ref, wu_ref, wd_ref, fg_ref, o_ref, h_ref, *, final_norm):
    f = pl.program_id(1)

    slabs = [slice(r, r + FFN_SLAB) for r in range(0, x_ref.shape[0], FFN_SLAB)]

    @pl.when(f == 0)
    def _():
        for rows in slabs:
            h_ref[rows, :] = _rms(x_ref[rows, :], g_ref[...]).astype(BF16)
        o_ref[...] = jnp.zeros_like(o_ref)

    for rows in slabs:
        h = h_ref[rows, :]
        tf = wg_ref.shape[1]
        acts = []
        for c in range(0, tf, FFN_HALF):
            gate = _dot(h, wg_ref[:, c:c + FFN_HALF])
            up = _dot(h, wu_ref[:, c:c + FFN_HALF])
            acts.append((gate * jax.nn.sigmoid(gate) * up).astype(BF16))
        o_ref[rows, :] += _dot(jnp.concatenate(acts, axis=1), wd_ref[...])

    @pl.when(f == pl.num_programs(1) - 1)
    def _():
        for rows in slabs:
            y = x_ref[rows, :] + FFN_RES * o_ref[rows, :]
            if final_norm:
                y = _rms(y, fg_ref[...])
            o_ref[rows, :] = y


def _ffn(x, g, wg, wu, wd, fg, layer, wlayer, *, final_norm, tm=1024, tf=512):
    n, d = x.shape
    f = wg.shape[2]
    return pl.pallas_call(
        functools.partial(_ffn_body, final_norm=final_norm),
        grid=(n // tm, f // tf),
        in_specs=[
            pl.BlockSpec((tm, d), lambda i, j: (i, 0)),
            pl.BlockSpec((None, 1, d), lambda i, j: (layer, 0, 0)),
            pl.BlockSpec((None, d, tf), lambda i, j: (wlayer, 0, j)),
            pl.BlockSpec((None, d, tf), lambda i, j: (wlayer, 0, j)),
            pl.BlockSpec((None, tf, d), lambda i, j: (wlayer, j, 0)),
            pl.BlockSpec((1, d), lambda i, j: (0, 0)),
        ],
        out_specs=pl.BlockSpec((tm, d), lambda i, j: (i, 0)),
        out_shape=jax.ShapeDtypeStruct((n, d), F32),
        scratch_shapes=[pltpu.VMEM((tm, d), BF16)],
        compiler_params=_cparams(("parallel", "arbitrary")),
        name="ffn",
    )(x, g, wg, wu, wd, fg)


def _cast_body(w_ref, o_ref):
    o_ref[...] = w_ref[...].astype(o_ref.dtype)


def _to_bf16(w, tr, layer):
    _, r, c = w.shape
    return pl.pallas_call(
        _cast_body,
        grid=(r // tr,),
        in_specs=[pl.BlockSpec((None, tr, c), lambda i: (layer, i, 0))],
        out_specs=pl.BlockSpec((None, tr, c), lambda i: (0, i, 0)),
        out_shape=jax.ShapeDtypeStruct((1, r, c), BF16),
        compiler_params=_cparams(("parallel",)),
        name="cast",
    )(w)


def _w_in_body(w_ref, o_ref):
    w = w_ref[...]
    seg = {name: w[:, off:off + width] for name, off, width in
           zip(IN_NAMES, (0,) + IN_OFFSETS, IN_SPLITS)}
    tr = w.shape[0]
    small_pad = jnp.zeros((tr, LANES - IDX_HEAD_DIM - IDX_HEADS - GLA_GATE_RANK), w.dtype)
    tail = jnp.zeros((tr, PROJ_COLS - COL_SM - LANES), w.dtype)
    order = ("aq", "iq", "gv", "gr", "m_a", "m_g", "gq", "gk", "ak", "av", "ik", "iw", "ga")
    o_ref[...] = jnp.concatenate([seg[k] for k in order] + [small_pad, tail], axis=1).astype(o_ref.dtype)


def _prep_w_in(w, tr=256):
    depth, d, c = w.shape
    return pl.pallas_call(
        _w_in_body,
        grid=(depth, d // tr),
        in_specs=[pl.BlockSpec((None, tr, c), lambda l, i: (l, i, 0))],
        out_specs=pl.BlockSpec((None, tr, PROJ_COLS), lambda l, i: (l, i, 0)),
        out_shape=jax.ShapeDtypeStruct((depth, d, PROJ_COLS), BF16),
        compiler_params=_cparams(("parallel", "parallel")),
        name="w_in_prep",
    )(w)


def _proj_body(x_ref, g_ref, w_ref, o_ref, h_ref):
    @pl.when(pl.program_id(1) == 0)
    def _():
        h_ref[...] = _rms(x_ref[...], g_ref[...]).astype(BF16)

    o_ref[...] = _dot(h_ref[...], w_ref[...]).astype(o_ref.dtype)


def _proj(x, g, w, layer, *, tm=1024, tn=2 * PROJ_TN):
    n, d = x.shape
    c = w.shape[2]
    return pl.pallas_call(
        _proj_body,
        grid=(n // tm, c // tn),
        in_specs=[
            pl.BlockSpec((tm, d), lambda i, j: (i, 0)),
            pl.BlockSpec((None, 1, d), lambda i, j: (layer, 0, 0)),
            pl.BlockSpec((None, d, tn), lambda i, j: (layer, 0, j)),
        ],
        out_specs=pl.BlockSpec((tm, tn), lambda i, j: (i, j)),
        out_shape=jax.ShapeDtypeStruct((n, c), BF16),
        scratch_shapes=[pltpu.VMEM((tm, d), BF16)],
        compiler_params=_cparams(("parallel", "arbitrary")),
        name="proj",
    )(x, g, w)


def _rope_tables(length, head_dim, rot):
    half = rot // 2
    inv = ROPE_THETA ** (-jnp.arange(0, rot, 2, dtype=F32) / rot)
    ang = jnp.arange(length, dtype=F32)[:, None] * inv[None, :]
    cos, sin = jnp.cos(ang), jnp.sin(ang)
    zeros = jnp.zeros((length, head_dim - rot), F32)
    zh = jnp.zeros((length, half), F32)
    c = jnp.concatenate([cos, cos, zeros + 1.0], axis=1)
    s1 = jnp.concatenate([-sin, zh, zeros], axis=1)
    s2 = jnp.concatenate([zh, sin, zeros], axis=1)
    reps = LANES // head_dim
    return tuple(jnp.tile(t, (1, reps)) for t in (c, s1, s2))


def _rope_lanes(x, c, s1, s2, half):
    return x * c + pltpu.roll(x, LANES - half, 1) * s1 + pltpu.roll(x, half, 1) * s2


def _store_t(dst_ref, rows, val):
    vt = val.T.astype(dst_ref.dtype)
    for c in range(vt.shape[1] // DSA_TQ):
        dst_ref[c, rows, :] = vt[:, c * DSA_TQ:(c + 1) * DSA_TQ]


def _rope_body(aq_ref, iq_ref, ak_ref, av_ref, sm_ref, ca_ref, sa1_ref, sa2_ref, ci_ref, si1_ref, si2_ref,
               aqt_ref, iqt_ref, smt_ref, ako_ref, smo_ref, avt_ref):
    ca, sa1, sa2 = ca_ref[...], sa1_ref[...], sa2_ref[...]
    ci, si1, si2 = ci_ref[...], si1_ref[...], si2_ref[...]
    ha, hi = ATTN_ROT // 2, IDX_ROT // 2
    tk = DSA_TK
    for j in range(ATTN_Q // LANES):
        sl = slice(j * LANES, (j + 1) * LANES)
        r = _rope_lanes(aq_ref[:, sl].astype(F32), ca, sa1, sa2, ha)
        _store_t(aqt_ref, sl, r * (ATTN_HEAD_DIM ** -0.5 * LOG2E))
    for j in range(IDX_Q // LANES):
        sl = slice(j * LANES, (j + 1) * LANES)
        r = _rope_lanes(iq_ref[:, sl].astype(F32), ci, si1, si2, hi)
        _store_t(iqt_ref, sl, r * (IDX_HEAD_DIM ** -0.5))
    for j in range(ATTN_KV // LANES):
        sl = slice(j * LANES, (j + 1) * LANES)
        ako_ref[:, sl] = _rope_lanes(ak_ref[:, sl].astype(F32), ca, sa1, sa2, ha).astype(BF16)
        vt = av_ref[:, sl].astype(F32).T
        for c in range(vt.shape[1] // tk):
            avt_ref[c, j * VT_ROWS:j * VT_ROWS + LANES, :] = vt[:, c * tk:(c + 1) * tk].astype(BF16)
            avt_ref[c, j * VT_ROWS + LANES:(j + 1) * VT_ROWS, :] = jnp.ones((VT_ROWS - LANES, tk), BF16)
    sm = sm_ref[...].astype(F32)
    lane = lax.broadcasted_iota(jnp.int32, sm.shape, 1)
    sm = jnp.where(lane < IDX_HEAD_DIM, _rope_lanes(sm, ci, si1, si2, hi), sm)
    smo_ref[...] = sm.astype(BF16)
    _store_t(smt_ref, slice(0, LANES), sm)


def _rope(proj, seq, tabs_a, tabs_i, *, tr=512):
    n = proj.shape[0]
    nl = seq // tr
    tk = DSA_TK
    tab = pl.BlockSpec((tr, LANES), lambda i: (i % nl, 0))
    return pl.pallas_call(
        _rope_body,
        grid=(n // tr,),
        in_specs=[
            pl.BlockSpec((tr, ATTN_Q), lambda i: (i, COL_AQ // ATTN_Q)),
            pl.BlockSpec((tr, IDX_Q), lambda i: (i, COL_IQ // IDX_Q)),
            pl.BlockSpec((tr, ATTN_KV), lambda i: (i, COL_AK // ATTN_KV)),
            pl.BlockSpec((tr, ATTN_KV), lambda i: (i, COL_AV // ATTN_KV)),
            pl.BlockSpec((tr, LANES), lambda i: (i, COL_SM // LANES)),
            tab, tab, tab, tab, tab, tab,
        ],
        out_specs=[
            pl.BlockSpec((tr // DSA_TQ, ATTN_Q, DSA_TQ), lambda i: (i, 0, 0)),
            pl.BlockSpec((tr // DSA_TQ, IDX_Q, DSA_TQ), lambda i: (i, 0, 0)),
            pl.BlockSpec((tr // DSA_TQ, LANES, DSA_TQ), lambda i: (i, 0, 0)),
            pl.BlockSpec((tr, ATTN_KV), lambda i: (i, 0)),
            pl.BlockSpec((tr, LANES), lambda i: (i, 0)),
            pl.BlockSpec((tr // tk, ATTN_KV_HEADS * VT_ROWS, tk), lambda i: (i, 0, 0)),
        ],
        out_shape=[
            jax.ShapeDtypeStruct((n // DSA_TQ, ATTN_Q, DSA_TQ), BF16),
            jax.ShapeDtypeStruct((n // DSA_TQ, IDX_Q, DSA_TQ), BF16),
            jax.ShapeDtypeStruct((n // DSA_TQ, LANES, DSA_TQ), BF16),
            jax.ShapeDtypeStruct((n, ATTN_KV), BF16),
            jax.ShapeDtypeStruct((n, LANES), BF16),
            jax.ShapeDtypeStruct((n // tk, ATTN_KV_HEADS * VT_ROWS, tk), BF16),
        ],
        compiler_params=_cparams(("parallel",)),
        name="rope",
    )(proj, proj, proj, proj, proj, *tabs_a, *tabs_i)


def _sortable(x):
    i = pltpu.bitcast(x + 0.0, jnp.int32)
    return jnp.where(i < 0, i ^ jnp.int32(0x7FFFFFFF), i)


def _tree_sum(xs):
    while len(xs) > 1:
        xs = [xs[i] + xs[i + 1] for i in range(0, len(xs) - 1, 2)] + ([xs[-1]] if len(xs) % 2 else [])
    return xs[0]


def _colsum(x):
    r, c = x.shape
    return jnp.sum(jnp.sum(x.reshape(r // 8, 8, c), axis=0), axis=0, keepdims=True)


def _dsa_body(*refs, topk, seq, n_cast):
    iqt_ref, smt_ref, aqt_ref, smk_ref, ak_ref, avt_ref = refs[:6]
    cast_in, o_ref, cast_out = refs[6:6 + n_cast], refs[6 + n_cast], refs[7 + n_cast:7 + 2 * n_cast]
    key_ref, m_ref, al_ref, acc_ref, s0_ref, s1_ref, p0_ref, p1_ref = refs[7 + 2 * n_cast:]
    for w_ref, wb_ref in zip(cast_in, cast_out):
        wb_ref[...] = w_ref[...].astype(BF16)
    tq, tk = DSA_TQ, DSA_TK
    qi = pl.program_id(1)
    t0 = qi * tq
    nkb = (t0 + tq + tk - 1) // tk
    iwt = smt_ref[SM_IW:SM_IW + IDX_HEADS, :].astype(F32) * (IDX_HEADS ** -0.5)
    qpos = lax.broadcasted_iota(jnp.int32, (tk, tq), 1) + t0
    kiota = lax.broadcasted_iota(jnp.int32, (tk, tq), 0)

    npair = (nkb + 1) // 2
    odd = nkb % 2 == 1

    def idx_block(kb):
        k0 = pl.multiple_of(kb * tk, tk)
        ik = smk_ref[pl.ds(k0, tk), :][:, SM_IK:SM_IK + IDX_HEAD_DIM]
        score = jnp.zeros((tk, tq), F32)
        for h in range(0, IDX_HEADS, 2):
            qq = jnp.concatenate([iqt_ref[(h + u) * IDX_HEAD_DIM:(h + u + 1) * IDX_HEAD_DIM, :]
                                  for u in range(2)], axis=1)
            rel = jnp.maximum(_dot(ik, qq), 0.0)
            score = score + rel[:, :tq] * iwt[h:h + 1, :] + rel[:, tq:] * iwt[h + 1:h + 2, :]
        key_ref[kb] = jnp.where(kiota + kb * tk <= qpos, _sortable(score), jnp.int32(INT_MIN))

    def idx_pair(i, carry):
        idx_block(2 * i)
        idx_block(2 * i + 1)
        return carry

    lax.fori_loop(0, nkb // 2, idx_pair, 0)

    @pl.when(odd)
    def _():
        idx_block(nkb - 1)
        key_ref[nkb] = jnp.full((tk, tq), INT_MIN, jnp.int32)

    def counts(*preds):
        def pair(i, cs):
            for u in range(2):
                kb = 2 * i + u
                key = key_ref[kb]
                hits = [jnp.where(p(key, kb), 1, 0) for p in preds]
                cs = tuple(c + _tree_sum([h[j:j + 8] for j in range(0, tk, 8)]) for c, h in zip(cs, hits))
            return cs
        cs = lax.fori_loop(0, npair, pair, tuple(jnp.zeros((8, tq), jnp.int32) for _ in preds))
        return tuple(jnp.sum(c.astype(F32), axis=0, keepdims=True) for c in cs)

    def count(pred):
        return counts(pred)[0]

    kf = float(topk)

    def bisect(it, carry):
        t, n_t = carry
        cand = t + lax.shift_left(jnp.int32(1), jnp.int32(31) - it)
        n_c = count(lambda k, kb: k >= cand)
        keep = n_c >= kf
        return jnp.where(keep, cand, t), jnp.where(keep, n_c, n_t)

    start = (jnp.full((1, tq), INT_MIN, jnp.int32), jnp.full((1, tq), float(seq + tk), F32))
    thr, n_thr = lax.fori_loop(0, BISECT_CHECK, bisect, start)
    settled = (n_thr == kf) | (thr == jnp.int32(INT_MIN))

    def keep_all(t):
        return jnp.where(t > jnp.int32(INT_MIN), jnp.int32(seq), jnp.int32(-1))

    def finish():
        t = lax.fori_loop(BISECT_CHECK, 32, bisect, (thr, n_thr))[0]
        n_gt, n_ge = counts(lambda k, kb: k > t, lambda k, kb: k >= t)
        need = kf - n_gt
        tied = (n_ge > kf) & (t > jnp.int32(INT_MIN))

        def tie_break():
            def step(it, x):
                cand = x + lax.shift_left(jnp.int32(1), jnp.int32(seq.bit_length() - 1) - it)
                below = count(lambda k, kb: (k == t) & (kiota + kb * tk < cand))
                return jnp.where(below < need, cand, x)
            return lax.fori_loop(0, seq.bit_length(), step, jnp.zeros((1, tq), jnp.int32))

        jtie = lax.cond(jnp.max(jnp.where(tied, 1.0, 0.0)) > 0.0, tie_break,
                        lambda: jnp.zeros((1, tq), jnp.int32))
        return t, jnp.where(tied, jtie, keep_all(t))

    thr, jmax = lax.cond(jnp.min(jnp.where(settled, 1.0, 0.0)) > 0.0, lambda: (thr, keep_all(thr)), finish)

    last = nkb - 1

    def scores(kb, s_ref):
        kb = jnp.minimum(kb, last)
        kc = jnp.minimum(kb, seq // tk - 1)
        k0 = pl.multiple_of(kc * tk, tk)
        key = key_ref[kb]
        sel = (key > thr) | ((key == thr) & (kiota + kb * tk <= jmax))
        bias = jnp.where(sel, 0.0, NEG_INF)
        bias = jnp.concatenate([bias] * ATTN_REP, axis=1)
        for g in range(ATTN_KV_HEADS):
            kg = ak_ref[pl.ds(k0, tk), g * ATTN_HEAD_DIM:(g + 1) * ATTN_HEAD_DIM]
            qt = jnp.concatenate(
                [aqt_ref[(g * ATTN_REP + r) * ATTN_HEAD_DIM:(g * ATTN_REP + r + 1) * ATTN_HEAD_DIM, :]
                 for r in range(ATTN_REP)], axis=1)
            s_ref[g] = _dot(kg, qt) + bias

    def softmax(s_ref, p_ref):
        for g in range(ATTN_KV_HEADS):
            for r in range(ATTN_REP):
                cols = slice(r * tq, (r + 1) * tq)
                s = s_ref[g, :, cols]
                m_old = m_ref[g, :, cols]
                m_new = jnp.maximum(m_old, jnp.max(s, axis=0, keepdims=True))
                p_ref[g, :, cols] = jnp.exp2(s - m_new).astype(BF16)
                al_ref[g, :, cols] = jnp.exp2(m_old - m_new)
                m_ref[g, :, cols] = m_new

    def values(kb, p_ref):
        kc = jnp.clip(kb, 0, seq // tk - 1)
        for g in range(ATTN_KV_HEADS):
            vt = avt_ref[kc, g * VT_ROWS:(g + 1) * VT_ROWS, :]
            acc_ref[g] = al_ref[g] * acc_ref[g] + _dot(vt, p_ref[g])

    m_ref[...] = jnp.full_like(m_ref, NEG_INF)
    acc_ref[...] = jnp.zeros_like(acc_ref)
    al_ref[...] = jnp.ones_like(al_ref)
    p1_ref[...] = jnp.zeros_like(p1_ref)
    scores(0, s0_ref)

    def att_pair(i, carry):
        j = 2 * i
        values(j - 1, p1_ref)
        softmax(s0_ref, p0_ref)
        scores(j + 1, s1_ref)
        values(j, p0_ref)
        softmax(s1_ref, p1_ref)
        scores(j + 2, s0_ref)
        return carry

    lax.fori_loop(0, nkb // 2, att_pair, 0)

    @pl.when(odd)
    def _():
        values(last - 1, p1_ref)
        softmax(s0_ref, p0_ref)
        values(last, p0_ref)

    @pl.when(jnp.logical_not(odd))
    def _():
        values(last, p1_ref)
    for g in range(ATTN_KV_HEADS):
        acc = acc_ref[g]
        ot = acc[:ATTN_HEAD_DIM] / acc[ATTN_HEAD_DIM:ATTN_HEAD_DIM + 1]
        for r in range(ATTN_REP):
            hh = g * ATTN_REP + r
            o_ref[:, hh * ATTN_HEAD_DIM:(hh + 1) * ATTN_HEAD_DIM] = ot[:, r * tq:(r + 1) * tq].T.astype(BF16)


def _dsa(aq_t, iq_t, sm_t, ak_r, sm_r, av_t, batch, seq, casts=()):
    n = ak_r.shape[0]
    tq, tk = DSA_TQ, DSA_TK
    nq = seq // tq
    nkb = seq // tk
    topk = min(TOPK_MAX, seq // 4)
    qmap = lambda b, i: (b * nq + i, 0, 0)
    steps = batch * nq
    cast_ops, cast_in_specs, cast_out_specs, cast_out_shapes = [], [], [], []
    for w, layer in casts:
        _, r, c = w.shape
        slab = next(s for s in range(PACK16, r + 1, PACK16) if r % s == 0 and s * steps >= r)
        last = r // slab - 1
        cast_ops.append(w)
        cast_in_specs.append(pl.BlockSpec(
            (None, slab, c), lambda b, i, layer=layer, last=last: (layer, jnp.minimum(b * nq + i, last), 0)))
        cast_out_specs.append(pl.BlockSpec(
            (None, slab, c), lambda b, i, last=last: (0, jnp.minimum(b * nq + i, last), 0)))
        cast_out_shapes.append(jax.ShapeDtypeStruct((1, r, c), BF16))
    out = pl.pallas_call(
        functools.partial(_dsa_body, topk=topk, seq=seq, n_cast=len(casts)),
        grid=(batch, nq),
        in_specs=[
            pl.BlockSpec((None, IDX_Q, tq), qmap),
            pl.BlockSpec((None, LANES, tq), qmap),
            pl.BlockSpec((None, ATTN_Q, tq), qmap),
            pl.BlockSpec((seq, LANES), lambda b, i: (b, 0)),
            pl.BlockSpec((seq, ATTN_KV), lambda b, i: (b, 0)),
            pl.BlockSpec((nkb, ATTN_KV_HEADS * VT_ROWS, tk), lambda b, i: (b, 0, 0)),
        ] + cast_in_specs,
        out_specs=[pl.BlockSpec((tq, ATTN_Q), lambda b, i: (b * nq + i, 0))] + cast_out_specs,
        out_shape=[jax.ShapeDtypeStruct((n, ATTN_Q), BF16)] + cast_out_shapes,
        scratch_shapes=[
            pltpu.VMEM((nkb + nkb % 2, tk, tq), jnp.int32),
            pltpu.VMEM((ATTN_KV_HEADS, 1, ATTN_REP * tq), F32),
            pltpu.VMEM((ATTN_KV_HEADS, 1, ATTN_REP * tq), F32),
            pltpu.VMEM((ATTN_KV_HEADS, VT_ROWS, ATTN_REP * tq), F32),
            pltpu.VMEM((ATTN_KV_HEADS, tk, ATTN_REP * tq), F32),
            pltpu.VMEM((ATTN_KV_HEADS, tk, ATTN_REP * tq), F32),
            pltpu.VMEM((ATTN_KV_HEADS, tk, ATTN_REP * tq), BF16),
            pltpu.VMEM((ATTN_KV_HEADS, tk, ATTN_REP * tq), BF16),
        ],
        compiler_params=_cparams(("arbitrary", "arbitrary")),
        name="dsa",
    )(iq_t, sm_t, aq_t, sm_r, ak_r, av_t, *cast_ops)
    return out[0], tuple(out[1:])


GLA_ROWS = 256


def _split3(x):
    hi = x.astype(BF16)
    r1 = x - hi.astype(F32)
    mid = r1.astype(BF16)
    lo = (r1 - mid.astype(F32)).astype(BF16)
    return hi, mid, lo


def _gla_body(q_ref, k_ref, v_ref, r_ref, sm_ref, wa_ref, ba_ref, gn_ref, o_ref, st_ref):
    rows, c = GLA_ROWS, GLA_CHUNK

    @pl.when(pl.program_id(1) == 0)
    def _():
        st_ref[...] = jnp.zeros_like(st_ref)

    ga = sm_ref[:, SM_GA:SM_GA + GLA_GATE_RANK]
    z = _dot(ga, wa_ref[...]) + ba_ref[...]
    g = -(jnp.maximum(-z, 0.0) + jnp.log1p(jnp.exp(-jnp.abs(z)))) * (1.0 / GLA_GATE_NORM)
    ri = lax.broadcasted_iota(jnp.int32, (rows, rows), 0)
    ci = lax.broadcasted_iota(jnp.int32, (rows, rows), 1)
    tri = jnp.where((ri // c == ci // c) & (ci <= ri), 1.0, 0.0).astype(BF16)
    hi, mid, lo = _split3(g)
    b = _dot(tri, hi) + _dot(tri, mid) + _dot(tri, lo)
    low = lax.broadcasted_iota(jnp.int32, (c, c), 1) <= lax.broadcasted_iota(jnp.int32, (c, c), 0)
    gn = gn_ref[...]
    for ch in range(rows // c):
        sl = slice(ch * c, (ch + 1) * c)
        for h in range(GLA_HEADS):
            ks = slice(h * GLA_DK, (h + 1) * GLA_DK)
            vs = slice(h * GLA_DV, (h + 1) * GLA_DV)
            bc = b[sl, ks]
            bl = bc[c - 1:c, :]
            q = q_ref[sl, ks].astype(F32) * (GLA_DK ** -0.5)
            k = k_ref[sl, ks].astype(F32)
            v = v_ref[sl, vs]
            qe = (q * jnp.exp(bc - bl)).astype(BF16)
            kd = (k * jnp.exp(bl - bc)).astype(BF16)
            qb = (q * jnp.exp(bc)).astype(BF16)
            a = jnp.where(low, _dot_nt(qe, kd), 0.0).astype(BF16)
            st = st_ref[h]
            o = _dot(a, v) + _dot_nt(qb, st.astype(BF16))
            vt = v.astype(F32).T.astype(BF16)
            st_ref[h] = st * jnp.exp(bl) + _dot(vt, kd)
            o = _rms(o, gn)
            gr = r_ref[sl, vs].astype(F32)
            o_ref[sl, vs] = (o * (gr * jax.nn.sigmoid(gr))).astype(BF16)


def _gla(proj, wa2, ba, gn, batch, seq):
    n = proj.shape[0]
    rows = GLA_ROWS
    ng = seq // rows
    rmap = lambda off, w: (lambda b, i: (b * ng + i, off // w))
    return pl.pallas_call(
        _gla_body,
        grid=(batch, ng),
        in_specs=[
            pl.BlockSpec((rows, GLA_K), rmap(COL_GQ, GLA_K)),
            pl.BlockSpec((rows, GLA_K), rmap(COL_GK, GLA_K)),
            pl.BlockSpec((rows, GLA_V), rmap(COL_GV, GLA_V)),
            pl.BlockSpec((rows, GLA_V), rmap(COL_GR, GLA_V)),
            pl.BlockSpec((rows, LANES), rmap(COL_SM, LANES)),
            pl.BlockSpec((GLA_GATE_RANK, GLA_K), lambda b, i: (0, 0)),
            pl.BlockSpec((1, GLA_K), lambda b, i: (0, 0)),
            pl.BlockSpec((1, GLA_DV), lambda b, i: (0, 0)),
        ],
        out_specs=pl.BlockSpec((rows, GLA_V), lambda b, i: (b * ng + i, 0)),
        out_shape=jax.ShapeDtypeStruct((n, GLA_V), BF16),
        scratch_shapes=[pltpu.VMEM((GLA_HEADS, GLA_DV, GLA_DK), F32)],
        compiler_params=_cparams(("parallel", "arbitrary")),
        name="gla",
    )(proj, proj, proj, proj, proj, wa2, ba, gn)


def _merge_body(x_ref, oa_ref, og_ref, ma_ref, mg_ref, wua_ref, wug_ref, wo_ref, o_ref):
    ya = jax.nn.sigmoid(ma_ref[...].astype(F32)) * _dot(oa_ref[...], wua_ref[...])
    yg = jax.nn.sigmoid(mg_ref[...].astype(F32)) * _dot(og_ref[...], wug_ref[...])
    y = (ya + yg).astype(BF16)
    o_ref[...] = x_ref[...] + _dot(y, wo_ref[...])


def _merge(x, o_attn, o_gla, proj, wua, wug, wo, layer, *, tm=256):
    n, d = x.shape
    const = lambda i: (layer, 0, 0)
    return pl.pallas_call(
        _merge_body,
        grid=(n // tm,),
        in_specs=[
            pl.BlockSpec((tm, d), lambda i: (i, 0)),
            pl.BlockSpec((tm, ATTN_Q), lambda i: (i, 0)),
            pl.BlockSpec((tm, GLA_V), lambda i: (i, 0)),
            pl.BlockSpec((tm, d), lambda i: (i, COL_MA // D_MODEL)),
            pl.BlockSpec((tm, d), lambda i: (i, COL_MG // D_MODEL)),
            pl.BlockSpec((None, ATTN_Q, d), const),
            pl.BlockSpec((None, GLA_V, d), const),
            pl.BlockSpec((None, d, d), const),
        ],
        out_specs=pl.BlockSpec((tm, d), lambda i: (i, 0)),
        out_shape=jax.ShapeDtypeStruct((n, d), F32),
        compiler_params=_cparams(("parallel",)),
        name="merge",
    )(x, o_attn, o_gla, proj, proj, wua, wug, wo)


def kernel(x, ffn1_norm, ffn1_w_gate, ffn1_w_up, ffn1_w_down, mix_norm, w_in, gla_w_a2, gla_b_a, gla_norm, w_up_attn, w_up_gla, w_out, ffn2_norm, ffn2_w_gate, ffn2_w_up, ffn2_w_down, final_norm):
    batch, seq, d = x.shape
    depth = w_in.shape[0]
    tabs_a = _rope_tables(seq, ATTN_HEAD_DIM, ATTN_ROT)
    tabs_i = _rope_tables(seq, IDX_HEAD_DIM, IDX_ROT)
    fg = final_norm.reshape(1, d)
    g1, gm, g2 = (a.reshape(depth, 1, d) for a in (ffn1_norm, mix_norm, ffn2_norm))
    w_in_b = _prep_w_in(w_in)
    ffn1_w, ffn2_w, merge_w = (ffn1_w_gate, ffn1_w_up, ffn1_w_down), (ffn2_w_gate, ffn2_w_up, ffn2_w_down), \
        (w_up_attn, w_up_gla, w_out)
    ffn1_b = {0: (_to_bf16(ffn1_w_gate, 256, 0), _to_bf16(ffn1_w_up, 256, 0), _to_bf16(ffn1_w_down, 512, 0))}
    hosted = [(w, l) for l in range(1, depth) for w in ffn1_w] + \
             [(w, l) for l in range(depth) for w in ffn2_w + merge_w]
    ffn2_b, merge_b = {}, {}
    h = x.reshape(batch * seq, d)
    for l in range(depth):
        h = _ffn(h, g1, *ffn1_b[l], fg, l, 0, final_norm=False)
        proj = _proj(h, gm, w_in_b, l)
        aq_t, iq_t, sm_t, ak_r, sm_r, av_t = _rope(proj, seq, tabs_a, tabs_i)
        o_attn, cast = _dsa(aq_t, iq_t, sm_t, ak_r, sm_r, av_t, batch, seq, hosted if l == 0 else ())
        if l == 0:
            cast = list(cast)
            for k in range(1, depth):
                ffn1_b[k], cast = tuple(cast[:3]), cast[3:]
            for k in range(depth):
                ffn2_b[k], merge_b[k], cast = tuple(cast[:3]), tuple(cast[3:6]), cast[6:]
        o_gla = _gla(proj, gla_w_a2[l].astype(BF16), gla_b_a[l].reshape(1, GLA_K),
                     gla_norm[l].reshape(1, GLA_DV), batch, seq)
        h = _merge(h, o_attn, o_gla, proj, *merge_b[l], 0)
        h = _ffn(h, g2, *ffn2_b[l], fg, l, 0, final_norm=(l == depth - 1))
    return h.reshape(batch, seq, d)
```
